```python
import math
import jax, jax.numpy as jnp
from jax import lax
import numpy as np

D_MODEL = 2048
BATCH = 16
SEQ = 2048
DEPTH = 2

N_HEADS = 16
N_KV_HEADS = 2
HEAD_DIM = 64
WINDOW = 128
BLOCK = 128
ATTN_WIDTH = N_HEADS * HEAD_DIM
KV_WIDTH = N_KV_HEADS * HEAD_DIM
SSM_WIDTH = D_MODEL // 4
SSM_GROUP = 16
N_SSM_GROUPS = SSM_WIDTH // SSM_GROUP
SSM_STATE = 64
DT_MIN = 1e-3
DT_MAX = 1e-1
N_MEM = 256
MEM_HEADS = 4
MEM_HEAD_DIM = 128
MEM_WIDTH = MEM_HEADS * MEM_HEAD_DIM
N_BRANCHES = 3
MIX_WIDTH = ATTN_WIDTH + SSM_WIDTH + MEM_WIDTH
Q_END = ATTN_WIDTH
K_END = Q_END + KV_WIDTH
V_END = K_END + KV_WIDTH
U_END = V_END + SSM_WIDTH
QM_END = U_END + MEM_WIDTH
IN_WIDTH = QM_END + N_BRANCHES * D_MODEL
N_EXPERTS = 32
TOP_K = 4
D_FF = D_MODEL // 2
MOE_BLOCK = 512
SWIGLU_ALPHA = 1.702
SWIGLU_LIMIT = 7.0
LN_EPS = 1e-5
DEEPNORM_ALPHA = (2.0 * DEPTH) ** 0.25
DEEPNORM_BETA = (8.0 * DEPTH) ** -0.25

kernel_name = "hybrid_swa_s5_memxattn_moe_deepnorm"


def layer_norm(x, g, b):
    xf = x.astype(jnp.float32)
    mu = xf.mean(-1, keepdims=True)
    var = jnp.square(xf - mu).mean(-1, keepdims=True)
    y = (xf - mu) * lax.rsqrt(var + LN_EPS) * g.astype(jnp.float32) + b.astype(jnp.float32)
    return y.astype(x.dtype)


def sliding_window_attention(q, k, v, sinks):
    b, s = q.shape[:2]
    nb = s // BLOCK
    grp = N_HEADS // N_KV_HEADS
    qb = q.reshape(b, nb, BLOCK, N_KV_HEADS, grp, HEAD_DIM)

    def band(t):
        tb = t.reshape(b, nb, BLOCK, N_KV_HEADS, HEAD_DIM)
        prev = jnp.pad(tb[:, :-1], ((0, 0), (1, 0), (0, 0), (0, 0), (0, 0)))
        return jnp.concatenate([prev, tb], axis=2)

    kb, vb = band(k), band(v)
    scores = jnp.einsum('bnqhgd,bnkhd->bnhgqk', qb, kb,
                        preferred_element_type=jnp.float32) * (HEAD_DIM ** -0.5)
    qi = jnp.arange(BLOCK)[:, None]
    kj = jnp.arange(2 * BLOCK)[None, :]
    rel = qi + BLOCK - kj
    band_ok = (rel >= 0) & (rel < WINDOW)
    not_first = (jnp.arange(nb) > 0)[:, None, None]
    valid = band_ok[None] & (not_first | (kj >= BLOCK)[None])
    scores = jnp.where(valid[None, :, None, None], scores, -jnp.inf)
    sink = sinks.astype(jnp.float32).reshape(1, 1, N_KV_HEADS, grp, 1, 1)
    m = jnp.maximum(scores.max(-1, keepdims=True), sink)
    e = jnp.exp(scores - m)
    probs = e / (e.sum(-1, keepdims=True) + jnp.exp(sink - m))
    out = jnp.einsum('bnhgqk,bnkhd->bnqhgd', probs.astype(v.dtype), vb)
    return out.reshape(b, s, ATTN_WIDTH)


def s5_ssm(u, lambda_re, lambda_im, log_dt, b_re, b_im, c_re, c_im, d_skip):
    b, s = u.shape[:2]
    f32 = jnp.float32
    uf = u.astype(f32).reshape(b, s, N_SSM_GROUPS, SSM_GROUP)
    lam = lax.complex(lambda_re.astype(f32), lambda_im.astype(f32))
    dt = jnp.exp(log_dt.astype(f32))[:, None]
    lam_bar = jnp.exp(lam * dt)
    b_mat = lax.complex(b_re.astype(f32), b_im.astype(f32))
    b_bar = ((lam_bar - 1.0) / lam)[..., None] * b_mat
    bu = jnp.einsum('gph,bsgh->bsgp', b_bar, uf.astype(jnp.complex64))

    def combine(left, right):
        a_l, x_l = left
        a_r, x_r = right
        return a_r * a_l, a_r * x_l + x_r

    a = jnp.broadcast_to(lam_bar, bu.shape)
    _, states = lax.associative_scan(combine, (a, bu), axis=1)
    c_mat = lax.complex(c_re.astype(f32), c_im.astype(f32))
    y = jnp.einsum('ghp,bsgp->bsgh', c_mat, states).real + d_skip.astype(f32) * uf
    return y.reshape(b, s, SSM_WIDTH).astype(u.dtype)


def memory_attention(q, mem_k, mem_v):
    b, s = q.shape[:2]
    scores = jnp.einsum('bshd,bmhd->bhsm', q, mem_k,
                        preferred_element_type=jnp.float32) * (MEM_HEAD_DIM ** -0.5)
    probs = jax.nn.softmax(scores, axis=-1).astype(mem_v.dtype)
    return jnp.einsum('bhsm,bmhd->bshd', probs, mem_v).reshape(b, s, MEM_WIDTH)


def mixer(h, mem, w_in, sinks, lambda_re, lambda_im, log_dt, b_re, b_im, c_re, c_im,
          d_skip, w_glu, w_mem_kv, w_branch, w_out):
    b, s, _ = h.shape
    proj = h @ w_in
    q = proj[..., :Q_END].reshape(b, s, N_HEADS, HEAD_DIM)
    k = proj[..., Q_END:K_END].reshape(b, s, N_KV_HEADS, HEAD_DIM)
    v = proj[..., K_END:V_END].reshape(b, s, N_KV_HEADS, HEAD_DIM)
    u = proj[..., V_END:U_END]
    qm = proj[..., U_END:QM_END].reshape(b, s, MEM_HEADS, MEM_HEAD_DIM)
    gates = jax.nn.sigmoid(proj[..., QM_END:].reshape(b, s, N_BRANCHES, D_MODEL))

    attn_out = sliding_window_attention(q, k, v, sinks)

    z = jax.nn.gelu(s5_ssm(u, lambda_re, lambda_im, log_dt, b_re, b_im, c_re, c_im, d_skip))
    zg = z @ w_glu
    ssm_out = zg[..., :SSM_WIDTH] * jax.nn.sigmoid(zg[..., SSM_WIDTH:])

    mem_kv = (mem @ w_mem_kv).reshape(b, N_MEM, 2, MEM_HEADS, MEM_HEAD_DIM)
    mem_out = memory_attention(qm, mem_kv[:, :, 0], mem_kv[:, :, 1])

    br_attn = attn_out @ w_branch[:ATTN_WIDTH]
    br_ssm = ssm_out @ w_branch[ATTN_WIDTH:ATTN_WIDTH + SSM_WIDTH]
    br_mem = mem_out @ w_branch[ATTN_WIDTH + SSM_WIDTH:]
    merged = gates[:, :, 0] * br_attn + gates[:, :, 1] * br_ssm + gates[:, :, 2] * br_mem
    return merged @ w_out


def moe_ffn(h, w_router, b_router, w_up, b_up, w_down, b_down):
    b, s, d = h.shape
    hf = h.reshape(-1, d)
    n_tok = hf.shape[0]
    n_pairs = n_tok * TOP_K
    n_blocks = -(-n_pairs // MOE_BLOCK) + N_EXPERTS
    logits = (hf @ w_router).astype(jnp.float32) + b_router.astype(jnp.float32)
    top_logits, top_idx = lax.top_k(logits, TOP_K)
    gate = jax.nn.softmax(top_logits, axis=-1)
    flat_e = top_idx.reshape(-1)
    order = jnp.argsort(flat_e)
    sorted_e = flat_e[order]
    tok = order // TOP_K
    counts = jnp.bincount(flat_e, length=N_EXPERTS).astype(jnp.int32)
    padded = (counts + MOE_BLOCK - 1) // MOE_BLOCK * MOE_BLOCK
    start = jnp.cumsum(counts) - counts
    pad_end = jnp.cumsum(padded)
    pad_start = pad_end - padded
    dest = pad_start[sorted_e] + jnp.arange(n_pairs, dtype=jnp.int32) - start[sorted_e]
    x_pad = jnp.zeros((n_blocks * MOE_BLOCK, d), hf.dtype).at[dest].set(hf[tok])
    block_rows = jnp.arange(n_blocks, dtype=jnp.int32) * MOE_BLOCK
    block_e = jnp.minimum(jnp.searchsorted(pad_end, block_rows, side='right'),
                          N_EXPERTS - 1).astype(jnp.int32)

    def expert_block(args):
        xb, e = args
        up = xb @ w_up[e] + b_up[e]
        x_glu = jnp.minimum(up[:, ::2], SWIGLU_LIMIT)
        x_lin = jnp.clip(up[:, 1::2], -SWIGLU_LIMIT, SWIGLU_LIMIT)
        act = x_glu * jax.nn.sigmoid(SWIGLU_ALPHA * x_glu) * (x_lin + 1.0)
        return act @ w_down[e] + b_down[e]

    out_pad = lax.map(expert_block, (x_pad.reshape(n_blocks, MOE_BLOCK, d), block_e))
    out = out_pad.reshape(-1, d)[dest] * gate.reshape(-1)[order][:, None].astype(hf.dtype)
    y = jax.ops.segment_sum(out, tok, num_segments=n_tok)
    return y.reshape(b, s, d)


def setup_inputs(seed: int = 0) -> dict:
    key = jax.random.key(seed)
    ks = jax.random.split(key, 32)
    L, D, G, P, H = DEPTH, D_MODEL, N_SSM_GROUPS, SSM_STATE, SSM_GROUP
    nrm = jax.random.normal
    branch_row_scale = jnp.concatenate([
        jnp.full((ATTN_WIDTH,), ATTN_WIDTH ** -0.5, jnp.float32),
        jnp.full((SSM_WIDTH,), SSM_WIDTH ** -0.5, jnp.float32),
        jnp.full((MEM_WIDTH,), MEM_WIDTH ** -0.5, jnp.float32)])
    return {
        "x": nrm(ks[0], (BATCH, SEQ, D), jnp.float32),
        "mem": nrm(ks[1], (BATCH, N_MEM, D), jnp.float32),
        "ln_in_g": 1.0 + 0.02 * nrm(ks[2], (D,), jnp.float32),
        "ln_in_b": 0.02 * nrm(ks[3], (D,), jnp.float32),
        "w_in": nrm(ks[4], (L, D, IN_WIDTH), jnp.float32) * D ** -0.5,
        "attn_sinks": 0.5 * nrm(ks[5], (L, N_HEADS), jnp.float32),
        "ssm_lambda_re": -0.5 + 0.01 * nrm(ks[6], (L, G, P), jnp.float32),
        "ssm_lambda_im": jnp.pi * jnp.arange(P, dtype=jnp.float32) + 0.01 * nrm(ks[7], (L, G, P), jnp.float32),
        "ssm_log_dt": jax.random.uniform(ks[8], (L, G), jnp.float32, math.log(DT_MIN), math.log(DT_MAX)),
        "ssm_b_re": nrm(ks[9], (L, G, P, H), jnp.float32) * (2 * H) ** -0.5,
        "ssm_b_im": nrm(ks[10], (L, G, P, H), jnp.float32) * (2 * H) ** -0.5,
        "ssm_c_re": nrm(ks[11], (L, G, H, P), jnp.float32) * P ** -0.5,
        "ssm_c_im": nrm(ks[12], (L, G, H, P), jnp.float32) * P ** -0.5,
        "ssm_d": nrm(ks[13], (L, G, H), jnp.float32),
        "w_glu": nrm(ks[14], (L, SSM_WIDTH, 2 * SSM_WIDTH), jnp.float32) * SSM_WIDTH ** -0.5,
        "w_mem_kv": nrm(ks[15], (L, D, 2 * MEM_WIDTH), jnp.float32) * D ** -0.5,
        "w_branch": nrm(ks[16], (L, MIX_WIDTH, D), jnp.float32) * branch_row_scale[None, :, None],
        "w_out": nrm(ks[17], (L, D, D), jnp.float32) * (D ** -0.5 * DEEPNORM_BETA),
        "ln1_g": 1.0 + 0.02 * nrm(ks[18], (L, D), jnp.float32),
        "ln1_b": 0.02 * nrm(ks[19], (L, D), jnp.float32),
        "w_router": nrm(ks[20], (L, D, N_EXPERTS), jnp.float32) * D ** -0.5,
        "b_router": 0.01 * nrm(ks[21], (L, N_EXPERTS), jnp.float32),
        "w_up": nrm(ks[22], (L, N_EXPERTS, D, 2 * D_FF), jnp.float32) * D ** -0.5,
        "b_up": 0.01 * nrm(ks[23], (L, N_EXPERTS, 2 * D_FF), jnp.float32),
        "w_down": nrm(ks[24], (L, N_EXPERTS, D_FF, D), jnp.float32) * (D_FF ** -0.5 * DEEPNORM_BETA),
        "b_down": 0.01 * nrm(ks[25], (L, N_EXPERTS, D), jnp.float32),
        "ln2_g": 1.0 + 0.02 * nrm(ks[26], (L, D), jnp.float32),
        "ln2_b": 0.02 * nrm(ks[27], (L, D), jnp.float32),
    }


def reference(x, mem, ln_in_g, ln_in_b, w_in, attn_sinks, ssm_lambda_re, ssm_lambda_im,
              ssm_log_dt, ssm_b_re, ssm_b_im, ssm_c_re, ssm_c_im, ssm_d, w_glu, w_mem_kv,
              w_branch, w_out, ln1_g, ln1_b, w_router, b_router, w_up, b_up, w_down, b_down,
              ln2_g, ln2_b):
    h = layer_norm(x, ln_in_g, ln_in_b)
    for l in range(DEPTH):
        mix = mixer(h, mem, w_in[l], attn_sinks[l], ssm_lambda_re[l], ssm_lambda_im[l],
                    ssm_log_dt[l], ssm_b_re[l], ssm_b_im[l], ssm_c_re[l], ssm_c_im[l],
                    ssm_d[l], w_glu[l], w_mem_kv[l], w_branch[l], w_out[l])
        h = layer_norm(DEEPNORM_ALPHA * h + mix, ln1_g[l], ln1_b[l])
        ffn = moe_ffn(h, w_router[l], b_router[l], w_up[l], b_up[l], w_down[l], b_down[l])
        h = layer_norm(DEEPNORM_ALPHA * h + ffn, ln2_g[l], ln2_b[l])
    return h
```

```python
import functools
import math

import jax
import jax.numpy as jnp
from jax import lax
from jax.experimental import pallas as pl
from jax.experimental.pallas import tpu as pltpu

F32 = jnp.float32
BF16 = jnp.bfloat16
I32 = jnp.int32

D_MODEL = 2048
DEPTH = 2
N_HEADS = 16
N_KV_HEADS = 2
HEAD_DIM = 64
WINDOW = 128
BLOCK = 128
ATTN_WIDTH = N_HEADS * HEAD_DIM
KV_WIDTH = N_KV_HEADS * HEAD_DIM
SSM_WIDTH = D_MODEL // 4
SSM_GROUP = 16
N_SSM_GROUPS = SSM_WIDTH // SSM_GROUP
SSM_STATE = 64
N_MEM = 256
MEM_HEADS = 4
MEM_HEAD_DIM = 128
MEM_WIDTH = MEM_HEADS * MEM_HEAD_DIM
N_BRANCHES = 3
Q_END = ATTN_WIDTH
K_END = Q_END + KV_WIDTH
V_END = K_END + KV_WIDTH
U_END = V_END + SSM_WIDTH
QM_END = U_END + MEM_WIDTH
IN_WIDTH = QM_END + N_BRANCHES * D_MODEL
N_EXPERTS = 32
TOP_K = 4
D_FF = D_MODEL // 2
MOE_BLOCK = 512
SWIGLU_ALPHA = 1.702
SWIGLU_LIMIT = 7.0
LN_EPS = 1e-5
DEEPNORM_ALPHA = (2.0 * DEPTH) ** 0.25

G_OFF = 0
Q_OFF = N_BRANCHES * D_MODEL
U_OFF = Q_OFF + ATTN_WIDTH
QM_OFF = U_OFF + SSM_WIDTH
K_OFF = QM_OFF + MEM_WIDTH
V_OFF = K_OFF + KV_WIDTH

SSM_COMPLEX = N_SSM_GROUPS * SSM_STATE
SSM_HALVES = 2
SSM_SCAN_STRIP = 512

VMEM_LIMIT = 56 * 1024 * 1024


def _cparams(sem):
    return pltpu.CompilerParams(dimension_semantics=sem, vmem_limit_bytes=VMEM_LIMIT)


def _layer_norm_rows(x, g, b):
    mu = jnp.mean(x, axis=-1, keepdims=True)
    xc = x - mu
    var = jnp.mean(xc * xc, axis=-1, keepdims=True)
    return xc * lax.rsqrt(var + LN_EPS) * g + b


def _ln_kernel(x_ref, g_ref, b_ref, o32_ref, o16_ref):
    y = _layer_norm_rows(x_ref[...], g_ref[...], b_ref[...])
    o32_ref[...] = y
    o16_ref[...] = y.astype(BF16)


def _layer_norm_in(x, g, b, tm=512):
    t, d = x.shape
    row = pl.BlockSpec((tm, d), lambda i: (i, 0))
    vec = pl.BlockSpec((1, d), lambda i: (0, 0))
    return pl.pallas_call(
        _ln_kernel,
        grid=(t // tm,),
        in_specs=[row, vec, vec],
        out_specs=[row, row],
        out_shape=[jax.ShapeDtypeStruct((t, d), F32), jax.ShapeDtypeStruct((t, d), BF16)],
        compiler_params=_cparams(("parallel",)),
        name="ln_in",
    )(x, g.reshape(1, d), b.reshape(1, d))


def _mm_kernel(a_ref, w_ref, o_ref):
    o_ref[...] = jnp.dot(a_ref[...], w_ref[...], preferred_element_type=F32).astype(o_ref.dtype)


def _matmul(a, w, tm, tn, name):
    m, k = a.shape
    n = w.shape[1]
    return pl.pallas_call(
        _mm_kernel,
        grid=(m // tm, n // tn),
        in_specs=[pl.BlockSpec((tm, k), lambda i, j: (i, 0)), pl.BlockSpec((k, tn), lambda i, j: (0, j))],
        out_specs=pl.BlockSpec((tm, tn), lambda i, j: (i, j)),
        out_shape=jax.ShapeDtypeStruct((m, n), BF16),
        compiler_params=_cparams(("parallel", "parallel")),
        name=name,
    )(a, w)


def _swa_kernel(sink_ref, q_ref, kc_ref, vc_ref, kp_ref, vp_ref, o_ref, *, nblk):
    i = pl.program_id(1)
    grp = N_HEADS // N_KV_HEADS
    scale = HEAD_DIM ** -0.5
    qi = lax.broadcasted_iota(I32, (BLOCK, 2 * BLOCK), 0)
    kj = lax.broadcasted_iota(I32, (BLOCK, 2 * BLOCK), 1)
    rel = qi + BLOCK - kj
    band_ok = (rel >= 0) & (rel < WINDOW)
    first_lo = jnp.where(i > 0, 0, BLOCK)
    for j in range(nblk):
        rows = slice(j * BLOCK, (j + 1) * BLOCK)
        if j == 0:
            k_prev, v_prev = kp_ref[...], vp_ref[...]
            valid = band_ok & (kj >= first_lo)
        else:
            prev = slice((j - 1) * BLOCK, j * BLOCK)
            k_prev, v_prev = kc_ref[prev, :], vc_ref[prev, :]
            valid = band_ok
        k_band = jnp.concatenate([k_prev, kc_ref[rows, :]], axis=0)
        v_band = jnp.concatenate([v_prev, vc_ref[rows, :]], axis=0)
        outs = []
        for h in range(N_HEADS):
            kv = slice((h // grp) * HEAD_DIM, (h // grp + 1) * HEAD_DIM)
            q_h = q_ref[rows, h * HEAD_DIM:(h + 1) * HEAD_DIM]
            s = lax.dot_general(q_h, k_band[:, kv], (((1,), (1,)), ((), ())),
                                preferred_element_type=F32) * scale
            s = jnp.where(valid, s, -jnp.inf)
            sink = sink_ref[h]
            m = jnp.maximum(jnp.max(s, axis=-1, keepdims=True), sink)
            e = jnp.exp(s - m)
            den = jnp.sum(e, axis=-1, keepdims=True) + jnp.exp(sink - m)
            p = (e * (1.0 / den)).astype(BF16)
            outs.append(jnp.dot(p, v_band[:, kv], preferred_element_type=F32))
        o_ref[rows, :] = jnp.concatenate(outs, axis=-1).astype(o_ref.dtype)


def _sliding_window_attention(proj, sinks, b, s, tq=512):
    t = b * s
    nq = s // tq
    nblk = tq // BLOCK
    sb = s // BLOCK
    cur = lambda col: (lambda bi, i: (bi * nq + i, col))
    prev = lambda col: (lambda bi, i: (bi * sb + jnp.maximum(i * nblk - 1, 0), col))
    return pl.pallas_call(
        functools.partial(_swa_kernel, nblk=nblk),
        grid=(b, nq),
        in_specs=[
            pl.BlockSpec(memory_space=pltpu.SMEM),
            pl.BlockSpec((tq, ATTN_WIDTH), cur(Q_OFF // ATTN_WIDTH)),
            pl.BlockSpec((tq, KV_WIDTH), cur(K_OFF // KV_WIDTH)),
            pl.BlockSpec((tq, KV_WIDTH), cur(V_OFF // KV_WIDTH)),
            pl.BlockSpec((BLOCK, KV_WIDTH), prev(K_OFF // KV_WIDTH)),
            pl.BlockSpec((BLOCK, KV_WIDTH), prev(V_OFF // KV_WIDTH)),
        ],
        out_specs=pl.BlockSpec((tq, ATTN_WIDTH), lambda bi, i: (bi * nq + i, 0)),
        out_shape=jax.ShapeDtypeStruct((t, ATTN_WIDTH), BF16),
        compiler_params=_cparams(("parallel", "parallel")),
        name="swa",
    )(sinks, proj, proj, proj, proj, proj)


def _mem_attn_kernel(q_ref, k_ref, v_ref, o_ref):
    s = lax.dot_general(q_ref[...], k_ref[...], (((1,), (1,)), ((), ())),
                        preferred_element_type=F32) * (MEM_HEAD_DIM ** -0.5)
    m = jnp.max(s, axis=-1, keepdims=True)
    e = jnp.exp(s - m)
    p = (e * (1.0 / jnp.sum(e, axis=-1, keepdims=True))).astype(BF16)
    o_ref[...] = jnp.dot(p, v_ref[...], preferred_element_type=F32).astype(o_ref.dtype)


def _memory_attention(proj, mem_kv, b, s):
    t = b * s
    qcol = QM_OFF // MEM_HEAD_DIM
    return pl.pallas_call(
        _mem_attn_kernel,
        grid=(b, MEM_HEADS),
        in_specs=[
            pl.BlockSpec((s, MEM_HEAD_DIM), lambda bi, h: (bi, qcol + h)),
            pl.BlockSpec((N_MEM, MEM_HEAD_DIM), lambda bi, h: (bi, h)),
            pl.BlockSpec((N_MEM, MEM_HEAD_DIM), lambda bi, h: (bi, MEM_HEADS + h)),
        ],
        out_specs=pl.BlockSpec((s, MEM_HEAD_DIM), lambda bi, h: (bi, h)),
        out_shape=jax.ShapeDtypeStruct((t, MEM_WIDTH), BF16),
        compiler_params=_cparams(("parallel", "parallel")),
        name="mem_attn",
    )(proj, mem_kv, mem_kv)


def _gelu_tanh(x):
    return 0.5 * x * (1.0 + jnp.tanh(math.sqrt(2.0 / math.pi) * (x + 0.044715 * (x * x * x))))


def _ssm_kernel(u_ref, wbr_ref, wbi_ref, lr_ref, li_ref, wcr_ref, wci_ref, d_ref, wglu_ref, o_ref,
                xr_ref, xi_ref, sr_ref, si_ref, *, nb, tc):
    @pl.when(pl.program_id(0) == 0)
    def _():
        sr_ref[...] = jnp.zeros_like(sr_ref)
        si_ref[...] = jnp.zeros_like(si_ref)

    u = u_ref[...]
    uw = SSM_WIDTH // SSM_HALVES
    xw = SSM_COMPLEX // SSM_HALVES
    for hf in range(SSM_HALVES):
        uh = u[:, hf * uw:(hf + 1) * uw]
        xr_ref[:, hf * xw:(hf + 1) * xw] = jnp.dot(uh, wbr_ref[hf], preferred_element_type=F32)
        xi_ref[:, hf * xw:(hf + 1) * xw] = jnp.dot(uh, wbi_ref[hf], preferred_element_type=F32)

    for st in range(SSM_COMPLEX // SSM_SCAN_STRIP):
        cols = slice(st * SSM_SCAN_STRIP, (st + 1) * SSM_SCAN_STRIP)
        lr = jnp.broadcast_to(lr_ref[:, cols], (nb, SSM_SCAN_STRIP))
        li = jnp.broadcast_to(li_ref[:, cols], (nb, SSM_SCAN_STRIP))

        def step(t, carry):
            sr, si = carry
            r0 = pl.multiple_of(t * nb, nb)
            nr = lr * sr - li * si + xr_ref[pl.ds(r0, nb), cols]
            ni = lr * si + li * sr + xi_ref[pl.ds(r0, nb), cols]
            xr_ref[pl.ds(r0, nb), cols] = nr
            xi_ref[pl.ds(r0, nb), cols] = ni
            return nr, ni

        sr, si = lax.fori_loop(0, tc, step, (sr_ref[:, cols], si_ref[:, cols]), unroll=8)
        sr_ref[:, cols] = sr
        si_ref[:, cols] = si

    ys = []
    for hf in range(SSM_HALVES):
        xs = slice(hf * xw, (hf + 1) * xw)
        yr = jnp.dot(xr_ref[:, xs].astype(BF16), wcr_ref[hf], preferred_element_type=F32)
        yi = jnp.dot(xi_ref[:, xs].astype(BF16), wci_ref[hf], preferred_element_type=F32)
        ys.append(yr - yi)
    y = jnp.concatenate(ys, axis=-1) + d_ref[...] * u.astype(F32)
    zg = jnp.dot(_gelu_tanh(y).astype(BF16), wglu_ref[...], preferred_element_type=F32)
    o_ref[...] = (zg[:, :SSM_WIDTH] * jax.nn.sigmoid(zg[:, SSM_WIDTH:])).astype(o_ref.dtype)


def _ssm_weights(lambda_re, lambda_im, log_dt, b_re, b_im, c_re, c_im, d_skip):
    g, p, h = N_SSM_GROUPS, SSM_STATE, SSM_GROUP
    gh = g // SSM_HALVES
    lam = lax.complex(lambda_re.astype(F32), lambda_im.astype(F32))
    dt = jnp.exp(log_dt.astype(F32))[:, None]
    lam_bar = jnp.exp(lam * dt)
    b_bar = ((lam_bar - 1.0) / lam)[..., None] * lax.complex(b_re.astype(F32), b_im.astype(F32))
    eye = jnp.eye(gh, dtype=F32)

    def blockdiag_in(m):
        m = m.reshape(SSM_HALVES, gh, p, h)
        return jnp.einsum('xgph,gk->xghkp', m, eye).reshape(SSM_HALVES, gh * h, gh * p).astype(BF16)

    def blockdiag_out(m):
        m = m.reshape(SSM_HALVES, gh, h, p)
        return jnp.einsum('xghp,gk->xgpkh', m, eye).reshape(SSM_HALVES, gh * p, gh * h).astype(BF16)

    return (blockdiag_in(jnp.real(b_bar)), blockdiag_in(jnp.imag(b_bar)),
            jnp.real(lam_bar).reshape(1, g * p), jnp.imag(lam_bar).reshape(1, g * p),
            blockdiag_out(c_re.astype(F32)), blockdiag_out(c_im.astype(F32)),
            d_skip.astype(F32).reshape(1, g * h))


def _ssm_branch(u_tm, ssm_w, w_glu, nb, s, tc=64):
    wbr, wbi, lr, li, wcr, wci, d = ssm_w
    rows = tc * nb
    full = lambda a: pl.BlockSpec(a.shape, lambda i: (0,) * a.ndim)
    return pl.pallas_call(
        functools.partial(_ssm_kernel, nb=nb, tc=tc),
        grid=(s // tc,),
        in_specs=[pl.BlockSpec((rows, SSM_WIDTH), lambda i: (i, 0)),
                  full(wbr), full(wbi), full(lr), full(li), full(wcr), full(wci), full(d), full(w_glu)],
        out_specs=pl.BlockSpec((rows, SSM_WIDTH), lambda i: (i, 0)),
        out_shape=jax.ShapeDtypeStruct((s * nb, SSM_WIDTH), BF16),
        scratch_shapes=[pltpu.VMEM((rows, SSM_COMPLEX), F32), pltpu.VMEM((rows, SSM_COMPLEX), F32),
                        pltpu.VMEM((nb, SSM_COMPLEX), F32), pltpu.VMEM((nb, SSM_COMPLEX), F32)],
        compiler_params=_cparams(("arbitrary",)),
        name="ssm",
    )(u_tm, wbr, wbi, lr, li, wcr, wci, d, w_glu)


def _merge_kernel(attn_ref, ssm_ref, mem_ref, g0_ref, g1_ref, g2_ref, h_ref, wb_ref, wo_ref, lg_ref, lb_ref,
                  o32_ref):
    def gated(x_ref, g_ref, lo, hi):
        br = jnp.dot(x_ref[...], wb_ref[lo:hi, :], preferred_element_type=F32)
        return jax.nn.sigmoid(g_ref[...].astype(F32)) * br

    merged = gated(attn_ref, g0_ref, 0, ATTN_WIDTH)
    merged += gated(ssm_ref, g1_ref, ATTN_WIDTH, ATTN_WIDTH + SSM_WIDTH)
    merged += gated(mem_ref, g2_ref, ATTN_WIDTH + SSM_WIDTH, ATTN_WIDTH + SSM_WIDTH + MEM_WIDTH)
    mix = jnp.dot(merged.astype(BF16), wo_ref[...], preferred_element_type=F32)
    o32_ref[...] = _layer_norm_rows(DEEPNORM_ALPHA * h_ref[...] + mix, lg_ref[...], lb_ref[...])


def _merge_out_ln(attn, ssm, mem, proj, h, w_branch, w_out, ln_g, ln_b, tm=256):
    t, d = h.shape
    row = lambda w, col=0: pl.BlockSpec((tm, w), lambda i: (i, col))
    const = lambda shape: pl.BlockSpec(shape, lambda i: (0, 0))
    return pl.pallas_call(
        _merge_kernel,
        grid=(t // tm,),
        in_specs=[row(ATTN_WIDTH), row(SSM_WIDTH), row(MEM_WIDTH),
                  row(d, G_OFF // d), row(d, G_OFF // d + 1), row(d, G_OFF // d + 2),
                  row(d), const(w_branch.shape), const(w_out.shape), const((1, d)), const((1, d))],
        out_specs=row(d),
        out_shape=jax.ShapeDtypeStruct((t, d), F32),
        compiler_params=_cparams(("parallel",)),
        name="merge_out_ln1",
    )(attn, ssm, mem, proj, proj, proj, h, w_branch, w_out, ln_g.reshape(1, d), ln_b.reshape(1, d))


def _split_bf16(x):
    hi = x.astype(BF16)
    return hi, (x - hi.astype(F32)).astype(BF16)


def _router_kernel(h_ref, wt_ref, b_ref, idx_ref, gate_ref, rank_ref, cnt_ref, carry_ref, *, tm):
    @pl.when(pl.program_id(0) == 0)
    def _():
        carry_ref[...] = jnp.zeros_like(carry_ref)

    nt = (((1,), (1,)), ((), ()))
    h_hi, h_lo = _split_bf16(h_ref[...])
    w_hi, w_lo = _split_bf16(wt_ref[...])
    lg = (lax.dot_general(w_hi, h_hi, nt, preferred_element_type=F32)
          + lax.dot_general(w_hi, h_lo, nt, preferred_element_type=F32)
          + lax.dot_general(w_lo, h_hi, nt, preferred_element_type=F32)) + b_ref[...]

    e_iota = lax.broadcasted_iota(I32, (N_EXPERTS, tm), 0)
    chosen = jnp.zeros((N_EXPERTS, tm), F32)
    vals, sels = [], []
    for k in range(TOP_K):
        m = jnp.max(lg, axis=0, keepdims=True)
        idx = jnp.min(jnp.where(lg == m, e_iota, N_EXPERTS), axis=0, keepdims=True)
        sel = e_iota == idx
        idx_ref[k:k + 1, :] = idx
        vals.append(m)
        sels.append(sel)
        chosen = jnp.where(sel, 1.0, chosen)
        lg = jnp.where(sel, -jnp.inf, lg)

    ex = [jnp.exp(v - vals[0]) for v in vals]
    inv = 1.0 / (ex[0] + ex[1] + ex[2] + ex[3])
    for k in range(TOP_K):
        gate_ref[k:k + 1, :] = ex[k] * inv

    r = lax.broadcasted_iota(I32, (tm, tm), 0)
    c = lax.broadcasted_iota(I32, (tm, tm), 1)
    before = jnp.where(r < c, 1.0, 0.0).astype(BF16)
    earlier = jnp.dot(chosen.astype(BF16), before, preferred_element_type=F32) + carry_ref[...]
    for k in range(TOP_K):
        rank_ref[k:k + 1, :] = jnp.sum(jnp.where(sels[k], earlier, 0.0), axis=0, keepdims=True).astype(I32)
    carry_ref[...] += jnp.sum(chosen, axis=1, keepdims=True)
    cnt_ref[...] = carry_ref[...].astype(I32)


def _router(h, w_router, b_router, tm=512):
    t, d = h.shape
    out = lambda dt: jax.ShapeDtypeStruct((TOP_K, t), dt)
    tok = pl.BlockSpec((TOP_K, tm), lambda i: (0, i))
    return pl.pallas_call(
        functools.partial(_router_kernel, tm=tm),
        grid=(t // tm,),
        in_specs=[pl.BlockSpec((tm, d), lambda i: (i, 0)),
                  pl.BlockSpec((N_EXPERTS, d), lambda i: (0, 0)),
                  pl.BlockSpec((N_EXPERTS, 1), lambda i: (0, 0))],
        out_specs=[tok, tok, tok, pl.BlockSpec((N_EXPERTS, 1), lambda i: (0, 0))],
        out_shape=[out(I32), out(F32), out(I32), jax.ShapeDtypeStruct((N_EXPERTS, 1), I32)],
        scratch_shapes=[pltpu.VMEM((N_EXPERTS, 1), F32)],
        compiler_params=_cparams(("arbitrary",)),
        name="router",
    )(h, w_router.T, b_router.reshape(N_EXPERTS, 1))


def _dispatch_kernel(dest_ref, h_ref, zero_ref, x_ref, sem, *, tm):
    del zero_ref

    def copy(t, k):
        return pltpu.make_async_copy(h_ref.at[pl.ds(t, 1)], x_ref.at[pl.ds(dest_ref[k, t], 1)], sem)

    def start(t, c):
        for k in range(TOP_K):
            copy(t, k).start()
        return c

    def wait(t, c):
        for k in range(TOP_K):
            copy(t, k).wait()
        return c

    lax.fori_loop(0, tm, start, 0)
    lax.fori_loop(0, tm, wait, 0)


def _dispatch(h, dest, n_rows, tm=256):
    t, d = h.shape
    return pl.pallas_call(
        functools.partial(_dispatch_kernel, tm=tm),
        grid=(t // tm,),
        in_specs=[pl.BlockSpec((TOP_K, tm), lambda i: (0, i), memory_space=pltpu.SMEM),
                  pl.BlockSpec((tm, d), lambda i: (i, 0)),
                  pl.BlockSpec(memory_space=pl.ANY)],
        out_specs=pl.BlockSpec(memory_space=pl.ANY),
        out_shape=jax.ShapeDtypeStruct((n_rows, d), h.dtype),
        scratch_shapes=[pltpu.SemaphoreType.DMA(())],
        input_output_aliases={2: 0},
        compiler_params=_cparams(("arbitrary",)),
        name="moe_dispatch",
    )(dest, h, jnp.zeros((n_rows, d), h.dtype))


def _expert_kernel(be_ref, nu_ref, x_ref, wg_ref, wl_ref, bg_ref, bl_ref, wd_ref, bd_ref, o_ref):
    del be_ref
    i = pl.program_id(0)

    @pl.when(i < nu_ref[0])
    def _():
        x = x_ref[...].astype(BF16)
        up_g = jnp.dot(x, wg_ref[0], preferred_element_type=F32) + bg_ref[0]
        up_l = jnp.dot(x, wl_ref[0], preferred_element_type=F32) + bl_ref[0]
        x_glu = jnp.minimum(up_g, SWIGLU_LIMIT)
        x_lin = jnp.clip(up_l, -SWIGLU_LIMIT, SWIGLU_LIMIT)
        act = x_glu * jax.nn.sigmoid(SWIGLU_ALPHA * x_glu) * (x_lin + 1.0)
        o_ref[...] = jnp.dot(act.astype(BF16), wd_ref[0], preferred_element_type=F32) + bd_ref[0]

    @pl.when(i >= nu_ref[0])
    def _():
        o_ref[...] = jnp.zeros_like(o_ref)


def _experts(x_pad, block_e, n_used, wg, wl, bg, bl, wd, bd):
    n_rows, d = x_pad.shape
    n_blocks = n_rows // MOE_BLOCK
    row = pl.BlockSpec((MOE_BLOCK, d), lambda i, be, nu: (i, 0))
    per_e = lambda a: pl.BlockSpec((1,) + a.shape[1:], lambda i, be, nu: (be[i], 0, 0))
    return pl.pallas_call(
        _expert_kernel,
        grid_spec=pltpu.PrefetchScalarGridSpec(
            num_scalar_prefetch=2,
            grid=(n_blocks,),
            in_specs=[row, per_e(wg), per_e(wl), per_e(bg), per_e(bl), per_e(wd), per_e(bd)],
            out_specs=row,
        ),
        out_shape=jax.ShapeDtypeStruct((n_rows, d), F32),
        compiler_params=_cparams(("arbitrary",)),
        name="moe_experts",
    )(block_e, n_used, x_pad, wg, wl, bg, bl, wd, bd)


def _combine_kernel(dest_ref, y_ref, gate_ref, h_ref, lg_ref, lb_ref, o32_ref, o16_ref, buf_ref, sem, *, tm):
    def copy(t, k):
        return pltpu.make_async_copy(y_ref.at[pl.ds(dest_ref[k, t], 1)], buf_ref.at[k, pl.ds(t, 1)], sem)

    def start(t, c):
        for k in range(TOP_K):
            copy(t, k).start()
        return c

    def wait(t, c):
        for k in range(TOP_K):
            copy(t, k).wait()
        return c

    lax.fori_loop(0, tm, start, 0)
    lax.fori_loop(0, tm, wait, 0)
    g = gate_ref[...]
    ffn = buf_ref[0] * g[:, 0:1]
    for k in range(1, TOP_K):
        ffn += buf_ref[k] * g[:, k:k + 1]
    y = _layer_norm_rows(DEEPNORM_ALPHA * h_ref[...] + ffn, lg_ref[...], lb_ref[...])
    o32_ref[...] = y
    o16_ref[...] = y.astype(BF16)


def _combine_ln(out_pad, dest, gate_t, h, ln_g, ln_b, tm=256):
    t, d = h.shape
    row = pl.BlockSpec((tm, d), lambda i: (i, 0))
    vec = pl.BlockSpec((1, d), lambda i: (0, 0))
    return pl.pallas_call(
        functools.partial(_combine_kernel, tm=tm),
        grid=(t // tm,),
        in_specs=[pl.BlockSpec((TOP_K, tm), lambda i: (0, i), memory_space=pltpu.SMEM),
                  pl.BlockSpec(memory_space=pl.ANY),
                  pl.BlockSpec((tm, TOP_K), lambda i: (i, 0)),
                  row, vec, vec],
        out_specs=[row, row],
        out_shape=[jax.ShapeDtypeStruct((t, d), F32), jax.ShapeDtypeStruct((t, d), BF16)],
        scratch_shapes=[pltpu.VMEM((TOP_K, tm, d), out_pad.dtype), pltpu.SemaphoreType.DMA(())],
        compiler_params=_cparams(("arbitrary",)),
        name="moe_combine_ln2",
    )(dest, out_pad, gate_t, h, ln_g.reshape(1, d), ln_b.reshape(1, d))


def _moe_layout(idx, rank, counts, n_blocks):
    counts = counts.reshape(N_EXPERTS)
    padded = (counts + MOE_BLOCK - 1) // MOE_BLOCK * MOE_BLOCK
    pad_end = jnp.cumsum(padded)
    pad_start = pad_end - padded
    dest = pad_start[idx] + rank
    block_rows = jnp.arange(n_blocks, dtype=I32) * MOE_BLOCK
    block_e = jnp.minimum(jnp.searchsorted(pad_end, block_rows, side='right'), N_EXPERTS - 1).astype(I32)
    n_used = (pad_end[-1:] // MOE_BLOCK).astype(I32)
    return dest.astype(I32), block_e, n_used


def _permute_w_in(w):
    pieces = [w[:, QM_END:], w[:, :Q_END], w[:, V_END:U_END], w[:, U_END:QM_END], w[:, Q_END:K_END], w[:, K_END:V_END]]
    return jnp.concatenate(pieces, axis=1).astype(BF16)


def kernel(x, mem, ln_in_g, ln_in_b, w_in, attn_sinks, ssm_lambda_re, ssm_lambda_im, ssm_log_dt, ssm_b_re, ssm_b_im, ssm_c_re, ssm_c_im, ssm_d, w_glu, w_mem_kv, w_branch, w_out, ln1_g, ln1_b, w_router, b_router, w_up, b_up, w_down, b_down, ln2_g, ln2_b):
    b, s, d = x.shape
    t = b * s
    n_blocks = -(-(t * TOP_K) // MOE_BLOCK) + N_EXPERTS
    mem16 = mem.reshape(b * N_MEM, d).astype(BF16)

    h32, h16 = _layer_norm_in(x.reshape(t, d), ln_in_g, ln_in_b)
    for l in range(DEPTH):
        proj = _matmul(h16, _permute_w_in(w_in[l]), tm=1024, tn=768, name="in_proj")
        mem_kv = _matmul(mem16, w_mem_kv[l].astype(BF16), tm=1024, tn=512, name="mem_kv")

        attn = _sliding_window_attention(proj, attn_sinks[l].astype(F32), b, s)
        mem_out = _memory_attention(proj, mem_kv, b, s)
        ssm_w = _ssm_weights(ssm_lambda_re[l], ssm_lambda_im[l], ssm_log_dt[l], ssm_b_re[l], ssm_b_im[l],
                             ssm_c_re[l], ssm_c_im[l], ssm_d[l])
        u_tm = proj[:, U_OFF:U_OFF + SSM_WIDTH].reshape(b, s, SSM_WIDTH).transpose(1, 0, 2).reshape(t, SSM_WIDTH)
        ssm_tm = _ssm_branch(u_tm, ssm_w, w_glu[l].astype(BF16), b, s)
        ssm_out = ssm_tm.reshape(s, b, SSM_WIDTH).transpose(1, 0, 2).reshape(t, SSM_WIDTH)

        h1 = _merge_out_ln(attn, ssm_out, mem_out, proj, h32, w_branch[l].astype(BF16), w_out[l].astype(BF16),
                           ln1_g[l], ln1_b[l])

        idx, gate, rank, counts = _router(h1, w_router[l], b_router[l])
        dest, block_e, n_used = _moe_layout(idx, rank, counts, n_blocks)
        x_pad = _dispatch(h1, dest, n_blocks * MOE_BLOCK)
        wg = w_up[l][:, :, 0::2].astype(BF16)
        wl = w_up[l][:, :, 1::2].astype(BF16)
        bg = b_up[l][:, None, 0::2].astype(F32)
        bl = b_up[l][:, None, 1::2].astype(F32)
        out_pad = _experts(x_pad, block_e, n_used, wg, wl, bg, bl, w_down[l].astype(BF16),
                           b_down[l][:, None, :].astype(F32))
        h32, h16 = _combine_ln(out_pad, dest, gate.T, h1, ln2_g[l], ln2_b[l])
    return h32.reshape(b, s, d)
```

```python
import functools
import math

import jax
import jax.numpy as jnp
from jax import lax
from jax.experimental import pallas as pl
from jax.experimental.pallas import tpu as pltpu

F32 = jnp.float32
BF16 = jnp.bfloat16
I32 = jnp.int32

D_MODEL = 2048
DEPTH = 2
N_HEADS = 16
N_KV_HEADS = 2
HEAD_DIM = 64
WINDOW = 128
BLOCK = 128
ATTN_WIDTH = N_HEADS * HEAD_DIM
KV_WIDTH = N_KV_HEADS * HEAD_DIM
SSM_WIDTH = D_MODEL // 4
SSM_GROUP = 16
N_SSM_GROUPS = SSM_WIDTH // SSM_GROUP
SSM_STATE = 64
N_MEM = 256
MEM_HEADS = 4
MEM_HEAD_DIM = 128
MEM_WIDTH = MEM_HEADS * MEM_HEAD_DIM
N_BRANCHES = 3
Q_END = ATTN_WIDTH
K_END = Q_END + KV_WIDTH
V_END = K_END + KV_WIDTH
U_END = V_END + SSM_WIDTH
QM_END = U_END + MEM_WIDTH
IN_WIDTH = QM_END + N_BRANCHES * D_MODEL
N_EXPERTS = 32
TOP_K = 4
D_FF = D_MODEL // 2
MOE_BLOCK = 512
SWIGLU_ALPHA = 1.702
SWIGLU_LIMIT = 7.0
LN_EPS = 1e-5
DEEPNORM_ALPHA = (2.0 * DEPTH) ** 0.25

G_OFF = 0
Q_OFF = N_BRANCHES * D_MODEL
U_OFF = Q_OFF + ATTN_WIDTH
QM_OFF = U_OFF + SSM_WIDTH
K_OFF = QM_OFF + MEM_WIDTH
V_OFF = K_OFF + KV_WIDTH

SSM_COMPLEX = N_SSM_GROUPS * SSM_STATE
SSM_HALVES = 2
SSM_SCAN_STRIP = 512

VMEM_LIMIT = 56 * 1024 * 1024


def _cparams(sem):
    return pltpu.CompilerParams(dimension_semantics=sem, vmem_limit_bytes=VMEM_LIMIT)


def _layer_norm_rows(x, g, b):
    mu = jnp.mean(x, axis=-1, keepdims=True)
    xc = x - mu
    var = jnp.mean(xc * xc, axis=-1, keepdims=True)
    return xc * lax.rsqrt(var + LN_EPS) * g + b


def _ln_kernel(x_ref, g_ref, b_ref, o32_ref, o16_ref):
    y = _layer_norm_rows(x_ref[...], g_ref[...], b_ref[...])
    o32_ref[...] = y
    o16_ref[...] = y.astype(BF16)


def _layer_norm_in(x, g, b, tm=512):
    t, d = x.shape
    row = pl.BlockSpec((tm, d), lambda i: (i, 0))
    vec = pl.BlockSpec((1, d), lambda i: (0, 0))
    return pl.pallas_call(
        _ln_kernel,
        grid=(t // tm,),
        in_specs=[row, vec, vec],
        out_specs=[row, row],
        out_shape=[jax.ShapeDtypeStruct((t, d), F32), jax.ShapeDtypeStruct((t, d), BF16)],
        compiler_params=_cparams(("parallel",)),
        name="ln_in",
    )(x, g.reshape(1, d), b.reshape(1, d))


def _mm_kernel(a_ref, w_ref, o_ref):
    o_ref[...] = jnp.dot(a_ref[...], w_ref[...], preferred_element_type=F32).astype(o_ref.dtype)


def _matmul(a, w, tm, tn, name):
    m, k = a.shape
    n = w.shape[1]
    return pl.pallas_call(
        _mm_kernel,
        grid=(m // tm, n // tn),
        in_specs=[pl.BlockSpec((tm, k), lambda i, j: (i, 0)), pl.BlockSpec((k, tn), lambda i, j: (0, j))],
        out_specs=pl.BlockSpec((tm, tn), lambda i, j: (i, j)),
        out_shape=jax.ShapeDtypeStruct((m, n), BF16),
        compiler_params=_cparams(("parallel", "parallel")),
        name=name,
    )(a, w)


def _swa_kernel(sink_ref, q_ref, kc_ref, vc_ref, kp_ref, vp_ref, o_ref, *, nblk):
    i = pl.program_id(1)
    grp = N_HEADS // N_KV_HEADS
    scale = HEAD_DIM ** -0.5
    qi = lax.broadcasted_iota(I32, (BLOCK, 2 * BLOCK), 0)
    kj = lax.broadcasted_iota(I32, (BLOCK, 2 * BLOCK), 1)
    rel = qi + BLOCK - kj
    band_ok = (rel >= 0) & (rel < WINDOW)
    first_lo = jnp.where(i > 0, 0, BLOCK)
    for j in range(nblk):
        rows = slice(j * BLOCK, (j + 1) * BLOCK)
        if j == 0:
            k_prev, v_prev = kp_ref[...], vp_ref[...]
            valid = band_ok & (kj >= first_lo)
        else:
            prev = slice((j - 1) * BLOCK, j * BLOCK)
            k_prev, v_prev = kc_ref[prev, :], vc_ref[prev, :]
            valid = band_ok
        k_band = jnp.concatenate([k_prev, kc_ref[rows, :]], axis=0)
        v_band = jnp.concatenate([v_prev, vc_ref[rows, :]], axis=0)
        outs = []
        for h in range(N_HEADS):
            kv = slice((h // grp) * HEAD_DIM, (h // grp + 1) * HEAD_DIM)
            q_h = q_ref[rows, h * HEAD_DIM:(h + 1) * HEAD_DIM]
            s = lax.dot_general(q_h, k_band[:, kv], (((1,), (1,)), ((), ())),
                                preferred_element_type=F32) * scale
            s = jnp.where(valid, s, -jnp.inf)
            sink = sink_ref[h]
            m = jnp.maximum(jnp.max(s, axis=-1, keepdims=True), sink)
            e = jnp.exp(s - m)
            den = jnp.sum(e, axis=-1, keepdims=True) + jnp.exp(sink - m)
            p = (e * (1.0 / den)).astype(BF16)
            outs.append(jnp.dot(p, v_band[:, kv], preferred_element_type=F32))
        o_ref[rows, :] = jnp.concatenate(outs, axis=-1).astype(o_ref.dtype)


def _sliding_window_attention(proj, sinks, b, s, tq=512):
    t = b * s
    nq = s // tq
    nblk = tq // BLOCK
    sb = s // BLOCK
    cur = lambda col: (lambda bi, i: (bi * nq + i, col))
    prev = lambda col: (lambda bi, i: (bi * sb + jnp.maximum(i * nblk - 1, 0), col))
    return pl.pallas_call(
        functools.partial(_swa_kernel, nblk=nblk),
        grid=(b, nq),
        in_specs=[
            pl.BlockSpec(memory_space=pltpu.SMEM),
            pl.BlockSpec((tq, ATTN_WIDTH), cur(Q_OFF // ATTN_WIDTH)),
            pl.BlockSpec((tq, KV_WIDTH), cur(K_OFF // KV_WIDTH)),
            pl.BlockSpec((tq, KV_WIDTH), cur(V_OFF // KV_WIDTH)),
            pl.BlockSpec((BLOCK, KV_WIDTH), prev(K_OFF // KV_WIDTH)),
            pl.BlockSpec((BLOCK, KV_WIDTH), prev(V_OFF // KV_WIDTH)),
        ],
        out_specs=pl.BlockSpec((tq, ATTN_WIDTH), lambda bi, i: (bi * nq + i, 0)),
        out_shape=jax.ShapeDtypeStruct((t, ATTN_WIDTH), BF16),
        compiler_params=_cparams(("parallel", "parallel")),
        name="swa",
    )(sinks, proj, proj, proj, proj, proj)


def _mem_attn_kernel(q_ref, k_ref, v_ref, o_ref):
    s = lax.dot_general(q_ref[...], k_ref[...], (((1,), (1,)), ((), ())),
                        preferred_element_type=F32) * (MEM_HEAD_DIM ** -0.5)
    m = jnp.max(s, axis=-1, keepdims=True)
    e = jnp.exp(s - m)
    p = (e * (1.0 / jnp.sum(e, axis=-1, keepdims=True))).astype(BF16)
    o_ref[...] = jnp.dot(p, v_ref[...], preferred_element_type=F32).astype(o_ref.dtype)


def _memory_attention(proj, mem_kv, b, s):
    t = b * s
    qcol = QM_OFF // MEM_HEAD_DIM
    return pl.pallas_call(
        _mem_attn_kernel,
        grid=(b, MEM_HEADS),
        in_specs=[
            pl.BlockSpec((s, MEM_HEAD_DIM), lambda bi, h: (bi, qcol + h)),
            pl.BlockSpec((N_MEM, MEM_HEAD_DIM), lambda bi, h: (bi, h)),
            pl.BlockSpec((N_MEM, MEM_HEAD_DIM), lambda bi, h: (bi, MEM_HEADS + h)),
        ],
        out_specs=pl.BlockSpec((s, MEM_HEAD_DIM), lambda bi, h: (bi, h)),
        out_shape=jax.ShapeDtypeStruct((t, MEM_WIDTH), BF16),
        compiler_params=_cparams(("parallel", "parallel")),
        name="mem_attn",
    )(proj, mem_kv, mem_kv)


def _gelu_tanh(x):
    return 0.5 * x * (1.0 + jnp.tanh(math.sqrt(2.0 / math.pi) * (x + 0.044715 * (x * x * x))))


def _ssm_kernel(u_ref, wbr_ref, wbi_ref, lr_ref, li_ref, wcr_ref, wci_ref, d_ref, wglu_ref, o_ref,
                xr_ref, xi_ref, sr_ref, si_ref, *, nb, tc):
    @pl.when(pl.program_id(0) == 0)
    def _():
        sr_ref[...] = jnp.zeros_like(sr_ref)
        si_ref[...] = jnp.zeros_like(si_ref)

    u = u_ref[...]
    uw = SSM_WIDTH // SSM_HALVES
    xw = SSM_COMPLEX // SSM_HALVES
    for hf in range(SSM_HALVES):
        uh = u[:, hf * uw:(hf + 1) * uw]
        xr_ref[:, hf * xw:(hf + 1) * xw] = jnp.dot(uh, wbr_ref[hf], preferred_element_type=F32)
        xi_ref[:, hf * xw:(hf + 1) * xw] = jnp.dot(uh, wbi_ref[hf], preferred_element_type=F32)

    for st in range(SSM_COMPLEX // SSM_SCAN_STRIP):
        cols = slice(st * SSM_SCAN_STRIP, (st + 1) * SSM_SCAN_STRIP)
        lr = jnp.broadcast_to(lr_ref[:, cols], (nb, SSM_SCAN_STRIP))
        li = jnp.broadcast_to(li_ref[:, cols], (nb, SSM_SCAN_STRIP))

        def step(t, carry):
            sr, si = carry
            r0 = pl.multiple_of(t * nb, nb)
            nr = lr * sr - li * si + xr_ref[pl.ds(r0, nb), cols]
            ni = lr * si + li * sr + xi_ref[pl.ds(r0, nb), cols]
            xr_ref[pl.ds(r0, nb), cols] = nr
            xi_ref[pl.ds(r0, nb), cols] = ni
            return nr, ni

        sr, si = lax.fori_loop(0, tc, step, (sr_ref[:, cols], si_ref[:, cols]), unroll=8)
        sr_ref[:, cols] = sr
        si_ref[:, cols] = si

    ys = []
    for hf in range(SSM_HALVES):
        xs = slice(hf * xw, (hf + 1) * xw)
        yr = jnp.dot(xr_ref[:, xs].astype(BF16), wcr_ref[hf], preferred_element_type=F32)
        yi = jnp.dot(xi_ref[:, xs].astype(BF16), wci_ref[hf], preferred_element_type=F32)
        ys.append(yr - yi)
    y = jnp.concatenate(ys, axis=-1) + d_ref[...] * u.astype(F32)
    zg = jnp.dot(_gelu_tanh(y).astype(BF16), wglu_ref[...], preferred_element_type=F32)
    o_ref[...] = (zg[:, :SSM_WIDTH] * jax.nn.sigmoid(zg[:, SSM_WIDTH:])).astype(o_ref.dtype)


def _ssm_weights(lambda_re, lambda_im, log_dt, b_re, b_im, c_re, c_im, d_skip):
    g, p, h = N_SSM_GROUPS, SSM_STATE, SSM_GROUP
    gh = g // SSM_HALVES
    lam = lax.complex(lambda_re.astype(F32), lambda_im.astype(F32))
    dt = jnp.exp(log_dt.astype(F32))[:, None]
    lam_bar = jnp.exp(lam * dt)
    b_bar = ((lam_bar - 1.0) / lam)[..., None] * lax.complex(b_re.astype(F32), b_im.astype(F32))
    eye = jnp.eye(gh, dtype=F32)

    def blockdiag_in(m):
        m = m.reshape(SSM_HALVES, gh, p, h)
        return jnp.einsum('xgph,gk->xghkp', m, eye).reshape(SSM_HALVES, gh * h, gh * p).astype(BF16)

    def blockdiag_out(m):
        m = m.reshape(SSM_HALVES, gh, h, p)
        return jnp.einsum('xghp,gk->xgpkh', m, eye).reshape(SSM_HALVES, gh * p, gh * h).astype(BF16)

    return (blockdiag_in(jnp.real(b_bar)), blockdiag_in(jnp.imag(b_bar)),
            jnp.real(lam_bar).reshape(1, g * p), jnp.imag(lam_bar).reshape(1, g * p),
            blockdiag_out(c_re.astype(F32)), blockdiag_out(c_im.astype(F32)),
            d_skip.astype(F32).reshape(1, g * h))


def _ssm_branch(u_tm, ssm_w, w_glu, nb, s, tc=64):
    wbr, wbi, lr, li, wcr, wci, d = ssm_w
    rows = tc * nb
    full = lambda a: pl.BlockSpec(a.shape, lambda i: (0,) * a.ndim)
    return pl.pallas_call(
        functools.partial(_ssm_kernel, nb=nb, tc=tc),
        grid=(s // tc,),
        in_specs=[pl.BlockSpec((rows, SSM_WIDTH), lambda i: (i, 0)),
                  full(wbr), full(wbi), full(lr), full(li), full(wcr), full(wci), full(d), full(w_glu)],
        out_specs=pl.BlockSpec((rows, SSM_WIDTH), lambda i: (i, 0)),
        out_shape=jax.ShapeDtypeStruct((s * nb, SSM_WIDTH), BF16),
        scratch_shapes=[pltpu.VMEM((rows, SSM_COMPLEX), F32), pltpu.VMEM((rows, SSM_COMPLEX), F32),
                        pltpu.VMEM((nb, SSM_COMPLEX), F32), pltpu.VMEM((nb, SSM_COMPLEX), F32)],
        compiler_params=_cparams(("arbitrary",)),
        name="ssm",
    )(u_tm, wbr, wbi, lr, li, wcr, wci, d, w_glu)


def _merge_kernel(attn_ref, ssm_ref, mem_ref, g0_ref, g1_ref, g2_ref, h_ref, wb_ref, wo_ref, lg_ref, lb_ref,
                  o32_ref):
    def gated(x_ref, g_ref, lo, hi):
        br = jnp.dot(x_ref[...], wb_ref[lo:hi, :], preferred_element_type=F32)
        return jax.nn.sigmoid(g_ref[...].astype(F32)) * br

    merged = gated(attn_ref, g0_ref, 0, ATTN_WIDTH)
    merged += gated(ssm_ref, g1_ref, ATTN_WIDTH, ATTN_WIDTH + SSM_WIDTH)
    merged += gated(mem_ref, g2_ref, ATTN_WIDTH + SSM_WIDTH, ATTN_WIDTH + SSM_WIDTH + MEM_WIDTH)
    mix = jnp.dot(merged.astype(BF16), wo_ref[...], preferred_element_type=F32)
    o32_ref[...] = _layer_norm_rows(DEEPNORM_ALPHA * h_ref[...] + mix, lg_ref[...], lb_ref[...])


def _merge_out_ln(attn, ssm, mem, proj, h, w_branch, w_out, ln_g, ln_b, tm=256):
    t, d = h.shape
    row = lambda w, col=0: pl.BlockSpec((tm, w), lambda i: (i, col))
    const = lambda shape: pl.BlockSpec(shape, lambda i: (0, 0))
    return pl.pallas_call(
        _merge_kernel,
        grid=(t // tm,),
        in_specs=[row(ATTN_WIDTH), row(SSM_WIDTH), row(MEM_WIDTH),
                  row(d, G_OFF // d), row(d, G_OFF // d + 1), row(d, G_OFF // d + 2),
                  row(d), const(w_branch.shape), const(w_out.shape), const((1, d)), const((1, d))],
        out_specs=row(d),
        out_shape=jax.ShapeDtypeStruct((t, d), F32),
        compiler_params=_cparams(("parallel",)),
        name="merge_out_ln1",
    )(attn, ssm, mem, proj, proj, proj, h, w_branch, w_out, ln_g.reshape(1, d), ln_b.reshape(1, d))


def _split_bf16(x):
    hi = x.astype(BF16)
    return hi, (x - hi.astype(F32)).astype(BF16)


def _router_kernel(h_ref, wt_ref, b_ref, idx_ref, gate_ref, rank_ref, cnt_ref, carry_ref, *, tm):
    @pl.when(pl.program_id(0) == 0)
    def _():
        carry_ref[...] = jnp.zeros_like(carry_ref)

    nt = (((1,), (1,)), ((), ()))
    h_hi, h_lo = _split_bf16(h_ref[...])
    w_hi, w_lo = _split_bf16(wt_ref[...])
    lg = (lax.dot_general(w_hi, h_hi, nt, preferred_element_type=F32)
          + lax.dot_general(w_hi, h_lo, nt, preferred_element_type=F32)
          + lax.dot_general(w_lo, h_hi, nt, preferred_element_type=F32)) + b_ref[...]

    e_iota = lax.broadcasted_iota(I32, (N_EXPERTS, tm), 0)
    chosen = jnp.zeros((N_EXPERTS, tm), F32)
    vals, sels = [], []
    for k in range(TOP_K):
        m = jnp.max(lg, axis=0, keepdims=True)
        idx = jnp.min(jnp.where(lg == m, e_iota, N_EXPERTS), axis=0, keepdims=True)
        sel = e_iota == idx
        idx_ref[k:k + 1, :] = idx
        vals.append(m)
        sels.append(sel)
        chosen = jnp.where(sel, 1.0, chosen)
        lg = jnp.where(sel, -jnp.inf, lg)

    ex = [jnp.exp(v - vals[0]) for v in vals]
    inv = 1.0 / (ex[0] + ex[1] + ex[2] + ex[3])
    for k in range(TOP_K):
        gate_ref[k:k + 1, :] = ex[k] * inv

    r = lax.broadcasted_iota(I32, (tm, tm), 0)
    c = lax.broadcasted_iota(I32, (tm, tm), 1)
    before = jnp.where(r < c, 1.0, 0.0).astype(BF16)
    earlier = jnp.dot(chosen.astype(BF16), before, preferred_element_type=F32) + carry_ref[...]
    for k in range(TOP_K):
        rank_ref[k:k + 1, :] = jnp.sum(jnp.where(sels[k], earlier, 0.0), axis=0, keepdims=True).astype(I32)
    carry_ref[...] += jnp.sum(chosen, axis=1, keepdims=True)
    cnt_ref[...] = carry_ref[...].astype(I32)


def _router(h, w_router, b_router, tm=512):
    t, d = h.shape
    out = lambda dt: jax.ShapeDtypeStruct((TOP_K, t), dt)
    tok = pl.BlockSpec((TOP_K, tm), lambda i: (0, i))
    return pl.pallas_call(
        functools.partial(_router_kernel, tm=tm),
        grid=(t // tm,),
        in_specs=[pl.BlockSpec((tm, d), lambda i: (i, 0)),
                  pl.BlockSpec((N_EXPERTS, d), lambda i: (0, 0)),
                  pl.BlockSpec((N_EXPERTS, 1), lambda i: (0, 0))],
        out_specs=[tok, tok, tok, pl.BlockSpec((N_EXPERTS, 1), lambda i: (0, 0))],
        out_shape=[out(I32), out(F32), out(I32), jax.ShapeDtypeStruct((N_EXPERTS, 1), I32)],
        scratch_shapes=[pltpu.VMEM((N_EXPERTS, 1), F32)],
        compiler_params=_cparams(("arbitrary",)),
        name="router",
    )(h, w_router.T, b_router.reshape(N_EXPERTS, 1))


def _dispatch_kernel(dest_ref, h_ref, zero_ref, x_ref, sem, *, tm):
    del zero_ref

    def copy(t, k):
        return pltpu.make_async_copy(h_ref.at[pl.ds(t, 1)], x_ref.at[pl.ds(dest_ref[k, t], 1)], sem)

    def start(t, c):
        for k in range(TOP_K):
            copy(t, k).start()
        return c

    def wait(t, c):
        for k in range(TOP_K):
            copy(t, k).wait()
        return c

    lax.fori_loop(0, tm, start, 0)
    lax.fori_loop(0, tm, wait, 0)


def _dispatch(h, dest, n_rows, tm=256):
    t, d = h.shape
    return pl.pallas_call(
        functools.partial(_dispatch_kernel, tm=tm),
        grid=(t // tm,),
        in_specs=[pl.BlockSpec((TOP_K, tm), lambda i: (0, i), memory_space=pltpu.SMEM),
                  pl.BlockSpec((tm, d), lambda i: (i, 0)),
                  pl.BlockSpec(memory_space=pl.ANY)],
        out_specs=pl.BlockSpec(memory_space=pl.ANY),
        out_shape=jax.ShapeDtypeStruct((n_rows, d), h.dtype),
        scratch_shapes=[pltpu.SemaphoreType.DMA(())],
        input_output_aliases={2: 0},
        compiler_params=_cparams(("arbitrary",)),
        name="moe_dispatch",
    )(dest, h, jnp.zeros((n_rows, d), h.dtype))


UP_CHUNK = 256
UP_HALF = UP_CHUNK // 2


def _pair_split_kernel(w_ref, o_ref):
    r = lax.broadcasted_iota(I32, (UP_CHUNK, UP_CHUNK), 0)
    c = lax.broadcasted_iota(I32, (UP_CHUNK, UP_CHUNK), 1)
    src = jnp.where(c < UP_HALF, 2 * c, 2 * (c - UP_HALF) + 1)
    perm = jnp.where(r == src, 1.0, 0.0).astype(BF16)
    for ch in range(w_ref.shape[-1] // UP_CHUNK):
        cols = slice(ch * UP_CHUNK, (ch + 1) * UP_CHUNK)
        o_ref[0, :, cols] = jnp.dot(w_ref[0, :, cols].astype(BF16), perm, preferred_element_type=F32).astype(BF16)


def _pair_split_up(w_up, tn=1024):
    e, d, n = w_up.shape
    blk = pl.BlockSpec((1, d, tn), lambda i, j: (i, 0, j))
    return pl.pallas_call(
        _pair_split_kernel,
        grid=(e, n // tn),
        in_specs=[blk],
        out_specs=blk,
        out_shape=jax.ShapeDtypeStruct((e, d, n), BF16),
        compiler_params=_cparams(("parallel", "parallel")),
        name="moe_up_pair_split",
    )(w_up)


def _pair_split_bias(b_up):
    e, n = b_up.shape
    return b_up.reshape(e, n // UP_CHUNK, UP_HALF, 2).transpose(0, 1, 3, 2).reshape(e, 1, n).astype(F32)


def _expert_kernel(be_ref, nu_ref, x_ref, wu_ref, bu_ref, wd_ref, bd_ref, o_ref):
    del be_ref
    i = pl.program_id(0)

    @pl.when(i < nu_ref[0])
    def _():
        up = jnp.dot(x_ref[...].astype(BF16), wu_ref[0], preferred_element_type=F32) + bu_ref[0]
        acts = []
        for ch in range(up.shape[-1] // UP_CHUNK):
            x_glu = jnp.minimum(up[:, ch * UP_CHUNK:ch * UP_CHUNK + UP_HALF], SWIGLU_LIMIT)
            x_lin = jnp.clip(up[:, ch * UP_CHUNK + UP_HALF:(ch + 1) * UP_CHUNK], -SWIGLU_LIMIT, SWIGLU_LIMIT)
            acts.append((x_glu * jax.nn.sigmoid(SWIGLU_ALPHA * x_glu) * (x_lin + 1.0)).astype(BF16))
        act = jnp.concatenate(acts, axis=-1)
        o_ref[...] = jnp.dot(act, wd_ref[0], preferred_element_type=F32) + bd_ref[0]

    @pl.when(i >= nu_ref[0])
    def _():
        o_ref[...] = jnp.zeros_like(o_ref)


def _experts(x_pad, block_e, n_used, wu, bu, wd, bd):
    n_rows, d = x_pad.shape
    n_blocks = n_rows // MOE_BLOCK
    row = pl.BlockSpec((MOE_BLOCK, d), lambda i, be, nu: (i, 0))
    per_e = lambda a: pl.BlockSpec((1,) + a.shape[1:], lambda i, be, nu: (be[i], 0, 0))
    return pl.pallas_call(
        _expert_kernel,
        grid_spec=pltpu.PrefetchScalarGridSpec(
            num_scalar_prefetch=2,
            grid=(n_blocks,),
            in_specs=[row, per_e(wu), per_e(bu), per_e(wd), per_e(bd)],
            out_specs=row,
        ),
        out_shape=jax.ShapeDtypeStruct((n_rows, d), F32),
        compiler_params=_cparams(("arbitrary",)),
        name="moe_experts",
    )(block_e, n_used, x_pad, wu, bu, wd, bd)


def _combine_kernel(dest_ref, y_ref, gate_ref, h_ref, lg_ref, lb_ref, o32_ref, o16_ref, buf_ref, sem, *, tm):
    def copy(t, k):
        return pltpu.make_async_copy(y_ref.at[pl.ds(dest_ref[k, t], 1)], buf_ref.at[k, pl.ds(t, 1)], sem)

    def start(t, c):
        for k in range(TOP_K):
            copy(t, k).start()
        return c

    def wait(t, c):
        for k in range(TOP_K):
            copy(t, k).wait()
        return c

    lax.fori_loop(0, tm, start, 0)
    lax.fori_loop(0, tm, wait, 0)
    g = gate_ref[...]
    ffn = buf_ref[0] * g[:, 0:1]
    for k in range(1, TOP_K):
        ffn += buf_ref[k] * g[:, k:k + 1]
    y = _layer_norm_rows(DEEPNORM_ALPHA * h_ref[...] + ffn, lg_ref[...], lb_ref[...])
    o32_ref[...] = y
    o16_ref[...] = y.astype(BF16)


def _combine_ln(out_pad, dest, gate_t, h, ln_g, ln_b, tm=256):
    t, d = h.shape
    row = pl.BlockSpec((tm, d), lambda i: (i, 0))
    vec = pl.BlockSpec((1, d), lambda i: (0, 0))
    return pl.pallas_call(
        functools.partial(_combine_kernel, tm=tm),
        grid=(t // tm,),
        in_specs=[pl.BlockSpec((TOP_K, tm), lambda i: (0, i), memory_space=pltpu.SMEM),
                  pl.BlockSpec(memory_space=pl.ANY),
                  pl.BlockSpec((tm, TOP_K), lambda i: (i, 0)),
                  row, vec, vec],
        out_specs=[row, row],
        out_shape=[jax.ShapeDtypeStruct((t, d), F32), jax.ShapeDtypeStruct((t, d), BF16)],
        scratch_shapes=[pltpu.VMEM((TOP_K, tm, d), out_pad.dtype), pltpu.SemaphoreType.DMA(())],
        compiler_params=_cparams(("arbitrary",)),
        name="moe_combine_ln2",
    )(dest, out_pad, gate_t, h, ln_g.reshape(1, d), ln_b.reshape(1, d))


def _moe_layout(idx, rank, counts, n_blocks):
    counts = counts.reshape(N_EXPERTS)
    padded = (counts + MOE_BLOCK - 1) // MOE_BLOCK * MOE_BLOCK
    pad_end = jnp.cumsum(padded)
    pad_start = pad_end - padded
    experts = jnp.arange(N_EXPERTS, dtype=I32)[:, None, None]
    dest = rank + jnp.sum(jnp.where(idx[None] == experts, pad_start[:, None, None], 0), axis=0)
    block_rows = jnp.arange(n_blocks, dtype=I32) * MOE_BLOCK
    block_e = jnp.minimum(jnp.sum(block_rows[:, None] >= pad_end[None, :], axis=1), N_EXPERTS - 1).astype(I32)
    n_used = (pad_end[-1:] // MOE_BLOCK).astype(I32)
    return dest.astype(I32), block_e, n_used


def _permute_w_in(w):
    pieces = [w[:, QM_END:], w[:, :Q_END], w[:, V_END:U_END], w[:, U_END:QM_END], w[:, Q_END:K_END], w[:, K_END:V_END]]
    return jnp.concatenate(pieces, axis=1).astype(BF16)


def kernel(x, mem, ln_in_g, ln_in_b, w_in, attn_sinks, ssm_lambda_re, ssm_lambda_im, ssm_log_dt, ssm_b_re, ssm_b_im, ssm_c_re, ssm_c_im, ssm_d, w_glu, w_mem_kv, w_branch, w_out, ln1_g, ln1_b, w_router, b_router, w_up, b_up, w_down, b_down, ln2_g, ln2_b):
    b, s, d = x.shape
    t = b * s
    n_blocks = -(-(t * TOP_K) // MOE_BLOCK) + N_EXPERTS
    mem16 = mem.reshape(b * N_MEM, d).astype(BF16)

    h32, h16 = _layer_norm_in(x.reshape(t, d), ln_in_g, ln_in_b)
    for l in range(DEPTH):
        proj = _matmul(h16, _permute_w_in(w_in[l]), tm=1024, tn=768, name="in_proj")
        mem_kv = _matmul(mem16, w_mem_kv[l].astype(BF16), tm=1024, tn=512, name="mem_kv")

        attn = _sliding_window_attention(proj, attn_sinks[l].astype(F32), b, s)
        mem_out = _memory_attention(proj, mem_kv, b, s)
        ssm_w = _ssm_weights(ssm_lambda_re[l], ssm_lambda_im[l], ssm_log_dt[l], ssm_b_re[l], ssm_b_im[l],
                             ssm_c_re[l], ssm_c_im[l], ssm_d[l])
        u_tm = proj[:, U_OFF:U_OFF + SSM_WIDTH].reshape(b, s, SSM_WIDTH).transpose(1, 0, 2).reshape(t, SSM_WIDTH)
        ssm_tm = _ssm_branch(u_tm, ssm_w, w_glu[l].astype(BF16), b, s)
        ssm_out = ssm_tm.reshape(s, b, SSM_WIDTH).transpose(1, 0, 2).reshape(t, SSM_WIDTH)

        h1 = _merge_out_ln(attn, ssm_out, mem_out, proj, h32, w_branch[l].astype(BF16), w_out[l].astype(BF16),
                           ln1_g[l], ln1_b[l])

        idx, gate, rank, counts = _router(h1, w_router[l], b_router[l])
        dest, block_e, n_used = _moe_layout(idx, rank, counts, n_blocks)
        x_pad = _dispatch(h1, dest, n_blocks * MOE_BLOCK)
        out_pad = _experts(x_pad, block_e, n_used, _pair_split_up(w_up[l]), _pair_split_bias(b_up[l]),
                           w_down[l].astype(BF16), b_down[l][:, None, :].astype(F32))
        h32, h16 = _combine_ln(out_pad, dest, gate.T, h1, ln2_g[l], ln2_b[l])
    return h32.reshape(b, s, d)
```

```python
import functools
import math

import jax
import jax.numpy as jnp
from jax import lax
from jax.experimental import pallas as pl
from jax.experimental.pallas import tpu as pltpu

F32 = jnp.float32
BF16 = jnp.bfloat16
I32 = jnp.int32

D_MODEL = 2048
DEPTH = 2
N_HEADS = 16
N_KV_HEADS = 2
HEAD_DIM = 64
WINDOW = 128
BLOCK = 128
ATTN_WIDTH = N_HEADS * HEAD_DIM
KV_WIDTH = N_KV_HEADS * HEAD_DIM
SSM_WIDTH = D_MODEL // 4
SSM_GROUP = 16
N_SSM_GROUPS = SSM_WIDTH // SSM_GROUP
SSM_STATE = 64
N_MEM = 256
MEM_HEADS = 4
MEM_HEAD_DIM = 128
MEM_WIDTH = MEM_HEADS * MEM_HEAD_DIM
N_BRANCHES = 3
Q_END = ATTN_WIDTH
K_END = Q_END + KV_WIDTH
V_END = K_END + KV_WIDTH
U_END = V_END + SSM_WIDTH
QM_END = U_END + MEM_WIDTH
IN_WIDTH = QM_END + N_BRANCHES * D_MODEL
N_EXPERTS = 32
TOP_K = 4
D_FF = D_MODEL // 2
MOE_BLOCK = 512
SWIGLU_ALPHA = 1.702
SWIGLU_LIMIT = 7.0
LN_EPS = 1e-5
DEEPNORM_ALPHA = (2.0 * DEPTH) ** 0.25

G_OFF = 0
Q_OFF = N_BRANCHES * D_MODEL
U_OFF = Q_OFF + ATTN_WIDTH
QM_OFF = U_OFF + SSM_WIDTH
K_OFF = QM_OFF + MEM_WIDTH
V_OFF = K_OFF + KV_WIDTH

SSM_COMPLEX = N_SSM_GROUPS * SSM_STATE
SSM_HALVES = 2
SSM_SCAN_STRIP = 512

VMEM_LIMIT = 56 * 1024 * 1024


def _cparams(sem):
    return pltpu.CompilerParams(dimension_semantics=sem, vmem_limit_bytes=VMEM_LIMIT)


def _layer_norm_rows(x, g, b):
    mu = jnp.mean(x, axis=-1, keepdims=True)
    xc = x - mu
    var = jnp.mean(xc * xc, axis=-1, keepdims=True)
    return xc * lax.rsqrt(var + LN_EPS) * g + b


LANES = 128
SLAB_ROWS = D_MODEL // 2 // LANES
DMA_GROUP = 4
U32 = jnp.uint32
HIGH_HALF = 0xFFFF0000


def _pack_pair(lo, hi):
    lo = lax.bitcast_convert_type(lo.astype(BF16).astype(F32), U32) >> 16
    hi = lax.bitcast_convert_type(hi.astype(BF16).astype(F32), U32) & U32(HIGH_HALF)
    return lo | hi


def _unpack_pair(w):
    return lax.bitcast_convert_type(w << 16, F32), lax.bitcast_convert_type(w & U32(HIGH_HALF), F32)


def _store_slabs(slab_ref, x, n):
    half = D_MODEL // 2
    for c in range(SLAB_ROWS):
        lo = x[:, c * LANES:(c + 1) * LANES]
        hi = x[:, half + c * LANES:half + (c + 1) * LANES]
        slab_ref[pl.ds(c, n, stride=SLAB_ROWS), :] = _pack_pair(lo, hi)


def _load_slabs(slab_ref, first_row, n):
    los, his = [], []
    for c in range(SLAB_ROWS):
        lo, hi = _unpack_pair(slab_ref[pl.ds(first_row + c, n, stride=SLAB_ROWS), :])
        los.append(lo)
        his.append(hi)
    return los, his


def _ln_kernel(x_ref, g_ref, b_ref, o32_ref, o16_ref):
    y = _layer_norm_rows(x_ref[...], g_ref[...], b_ref[...])
    o32_ref[...] = y
    o16_ref[...] = y.astype(BF16)


def _layer_norm_in(x, g, b, tm=512):
    t, d = x.shape
    row = pl.BlockSpec((tm, d), lambda i: (i, 0))
    vec = pl.BlockSpec((1, d), lambda i: (0, 0))
    return pl.pallas_call(
        _ln_kernel,
        grid=(t // tm,),
        in_specs=[row, vec, vec],
        out_specs=[row, row],
        out_shape=[jax.ShapeDtypeStruct((t, d), F32), jax.ShapeDtypeStruct((t, d), BF16)],
        compiler_params=_cparams(("parallel",)),
        name="ln_in",
    )(x, g.reshape(1, d), b.reshape(1, d))


def _mm_kernel(a_ref, w_ref, o_ref):
    o_ref[...] = jnp.dot(a_ref[...], w_ref[...], preferred_element_type=F32).astype(o_ref.dtype)


def _matmul(a, w, tm, tn, name):
    m, k = a.shape
    n = w.shape[1]
    return pl.pallas_call(
        _mm_kernel,
        grid=(m // tm, n // tn),
        in_specs=[pl.BlockSpec((tm, k), lambda i, j: (i, 0)), pl.BlockSpec((k, tn), lambda i, j: (0, j))],
        out_specs=pl.BlockSpec((tm, tn), lambda i, j: (i, j)),
        out_shape=jax.ShapeDtypeStruct((m, n), BF16),
        compiler_params=_cparams(("parallel", "parallel")),
        name=name,
    )(a, w)


def _swa_kernel(sink_ref, q_ref, kc_ref, vc_ref, kp_ref, vp_ref, o_ref, *, nblk):
    i = pl.program_id(1)
    grp = N_HEADS // N_KV_HEADS
    pairs = grp // 2
    pw = 2 * HEAD_DIM
    nrow = pairs * BLOCK
    nkey = 2 * BLOCK
    log2e = math.log2(math.e)
    scale = HEAD_DIM ** -0.5 * log2e
    row = lax.broadcasted_iota(I32, (nrow, nkey), 0)
    kj = lax.broadcasted_iota(I32, (nrow, nkey), 1)
    rel = (row & (BLOCK - 1)) + BLOCK - kj
    band_ok = (rel >= 0) & (rel < WINDOW)
    row_pair = lax.broadcasted_iota(I32, (nrow, 1), 0) // BLOCK
    low_lanes = lax.broadcasted_iota(I32, (nkey, pw), 1) < HEAD_DIM
    first_lo = jnp.where(i > 0, 0, BLOCK)

    def block_diag(band, g):
        band = band.astype(F32)
        swapped = pltpu.roll(band, HEAD_DIM, axis=1)
        top = jnp.where(low_lanes, band if g == 0 else swapped, 0.0)
        bottom = jnp.where(low_lanes, 0.0, swapped if g == 0 else band)
        return jnp.concatenate([top, bottom], axis=0).astype(BF16)

    for j in range(nblk):
        rows = slice(j * BLOCK, (j + 1) * BLOCK)
        if j == 0:
            k_prev, v_prev = kp_ref[...], vp_ref[...]
            valid = band_ok & (kj >= first_lo)
        else:
            prev = slice((j - 1) * BLOCK, j * BLOCK)
            k_prev, v_prev = kc_ref[prev, :], vc_ref[prev, :]
            valid = band_ok
        k_band = jnp.concatenate([k_prev, kc_ref[rows, :]], axis=0)
        v_band = jnp.concatenate([v_prev, vc_ref[rows, :]], axis=0)
        for g in range(N_KV_HEADS):
            k2 = block_diag(k_band, g)
            v2 = block_diag(v_band, g)
            cols = [slice((g * pairs + p) * pw, (g * pairs + p + 1) * pw) for p in range(pairs)]
            q2 = jnp.concatenate([q_ref[rows, c] for c in cols], axis=0)
            s = lax.dot_general(q2, k2, (((1,), (1,)), ((), ())), preferred_element_type=F32) * scale
            probs = []
            for half in range(2):
                sh = jnp.where(valid, s[:, half * nkey:(half + 1) * nkey], -jnp.inf)
                sink = sink_ref[g * grp + half] * log2e
                for p in range(1, pairs):
                    sink = jnp.where(row_pair == p, sink_ref[g * grp + 2 * p + half] * log2e, sink)
                m = jnp.maximum(jnp.max(sh, axis=-1, keepdims=True), sink)
                e = jnp.exp2(sh - m)
                den = jnp.sum(e, axis=-1, keepdims=True) + jnp.exp2(sink - m)
                probs.append((e * (1.0 / den)).astype(BF16))
            out = jnp.dot(jnp.concatenate(probs, axis=-1), v2, preferred_element_type=F32)
            for p, c in enumerate(cols):
                o_ref[rows, c] = out[p * BLOCK:(p + 1) * BLOCK, :].astype(o_ref.dtype)


def _sliding_window_attention(proj, sinks, b, s, tq=512):
    t = b * s
    nq = s // tq
    nblk = tq // BLOCK
    sb = s // BLOCK
    cur = lambda col: (lambda bi, i: (bi * nq + i, col))
    prev = lambda col: (lambda bi, i: (bi * sb + jnp.maximum(i * nblk - 1, 0), col))
    return pl.pallas_call(
        functools.partial(_swa_kernel, nblk=nblk),
        grid=(b, nq),
        in_specs=[
            pl.BlockSpec(memory_space=pltpu.SMEM),
            pl.BlockSpec((tq, ATTN_WIDTH), cur(Q_OFF // ATTN_WIDTH)),
            pl.BlockSpec((tq, KV_WIDTH), cur(K_OFF // KV_WIDTH)),
            pl.BlockSpec((tq, KV_WIDTH), cur(V_OFF // KV_WIDTH)),
            pl.BlockSpec((BLOCK, KV_WIDTH), prev(K_OFF // KV_WIDTH)),
            pl.BlockSpec((BLOCK, KV_WIDTH), prev(V_OFF // KV_WIDTH)),
        ],
        out_specs=pl.BlockSpec((tq, ATTN_WIDTH), lambda bi, i: (bi * nq + i, 0)),
        out_shape=jax.ShapeDtypeStruct((t, ATTN_WIDTH), BF16),
        compiler_params=_cparams(("parallel", "parallel")),
        name="swa",
    )(sinks, proj, proj, proj, proj, proj)


def _mem_attn_kernel(q_ref, k_ref, v_ref, o_ref):
    s = lax.dot_general(q_ref[...], k_ref[...], (((1,), (1,)), ((), ())),
                        preferred_element_type=F32) * (MEM_HEAD_DIM ** -0.5)
    m = jnp.max(s, axis=-1, keepdims=True)
    e = jnp.exp(s - m)
    p = (e * (1.0 / jnp.sum(e, axis=-1, keepdims=True))).astype(BF16)
    o_ref[...] = jnp.dot(p, v_ref[...], preferred_element_type=F32).astype(o_ref.dtype)


def _memory_attention(proj, mem_kv, b, s):
    t = b * s
    qcol = QM_OFF // MEM_HEAD_DIM
    return pl.pallas_call(
        _mem_attn_kernel,
        grid=(b, MEM_HEADS),
        in_specs=[
            pl.BlockSpec((s, MEM_HEAD_DIM), lambda bi, h: (bi, qcol + h)),
            pl.BlockSpec((N_MEM, MEM_HEAD_DIM), lambda bi, h: (bi, h)),
            pl.BlockSpec((N_MEM, MEM_HEAD_DIM), lambda bi, h: (bi, MEM_HEADS + h)),
        ],
        out_specs=pl.BlockSpec((s, MEM_HEAD_DIM), lambda bi, h: (bi, h)),
        out_shape=jax.ShapeDtypeStruct((t, MEM_WIDTH), BF16),
        compiler_params=_cparams(("parallel", "parallel")),
        name="mem_attn",
    )(proj, mem_kv, mem_kv)


def _gelu_tanh(x):
    return 0.5 * x * (1.0 + jnp.tanh(math.sqrt(2.0 / math.pi) * (x + 0.044715 * (x * x * x))))


def _ssm_kernel(u_ref, wbr_ref, wbi_ref, lr_ref, li_ref, wcr_ref, wci_ref, d_ref, wglu_ref, o_ref,
                xr_ref, xi_ref, sr_ref, si_ref, *, nb, tc):
    @pl.when(pl.program_id(0) == 0)
    def _():
        sr_ref[...] = jnp.zeros_like(sr_ref)
        si_ref[...] = jnp.zeros_like(si_ref)

    u = u_ref[...]
    uw = SSM_WIDTH // SSM_HALVES
    xw = SSM_COMPLEX // SSM_HALVES
    for hf in range(SSM_HALVES):
        uh = u[:, hf * uw:(hf + 1) * uw]
        xr_ref[:, hf * xw:(hf + 1) * xw] = jnp.dot(uh, wbr_ref[hf], preferred_element_type=F32)
        xi_ref[:, hf * xw:(hf + 1) * xw] = jnp.dot(uh, wbi_ref[hf], preferred_element_type=F32)

    for st in range(SSM_COMPLEX // SSM_SCAN_STRIP):
        cols = slice(st * SSM_SCAN_STRIP, (st + 1) * SSM_SCAN_STRIP)
        lr = jnp.broadcast_to(lr_ref[:, cols], (nb, SSM_SCAN_STRIP))
        li = jnp.broadcast_to(li_ref[:, cols], (nb, SSM_SCAN_STRIP))

        def step(t, carry):
            sr, si = carry
            r0 = pl.multiple_of(t * nb, nb)
            nr = lr * sr - li * si + xr_ref[pl.ds(r0, nb), cols]
            ni = lr * si + li * sr + xi_ref[pl.ds(r0, nb), cols]
            xr_ref[pl.ds(r0, nb), cols] = nr
            xi_ref[pl.ds(r0, nb), cols] = ni
            return nr, ni

        sr, si = lax.fori_loop(0, tc, step, (sr_ref[:, cols], si_ref[:, cols]), unroll=8)
        sr_ref[:, cols] = sr
        si_ref[:, cols] = si

    ys = []
    for hf in range(SSM_HALVES):
        xs = slice(hf * xw, (hf + 1) * xw)
        yr = jnp.dot(xr_ref[:, xs].astype(BF16), wcr_ref[hf], preferred_element_type=F32)
        yi = jnp.dot(xi_ref[:, xs].astype(BF16), wci_ref[hf], preferred_element_type=F32)
        ys.append(yr - yi)
    y = jnp.concatenate(ys, axis=-1) + d_ref[...] * u.astype(F32)
    zg = jnp.dot(_gelu_tanh(y).astype(BF16), wglu_ref[...], preferred_element_type=F32)
    o_ref[...] = (zg[:, :SSM_WIDTH] * jax.nn.sigmoid(zg[:, SSM_WIDTH:])).astype(o_ref.dtype)


def _ssm_weights(lambda_re, lambda_im, log_dt, b_re, b_im, c_re, c_im, d_skip):
    g, p, h = N_SSM_GROUPS, SSM_STATE, SSM_GROUP
    gh = g // SSM_HALVES
    lam = lax.complex(lambda_re.astype(F32), lambda_im.astype(F32))
    dt = jnp.exp(log_dt.astype(F32))[:, None]
    lam_bar = jnp.exp(lam * dt)
    b_bar = ((lam_bar - 1.0) / lam)[..., None] * lax.complex(b_re.astype(F32), b_im.astype(F32))
    eye = jnp.eye(gh, dtype=F32)

    def blockdiag_in(m):
        m = m.reshape(SSM_HALVES, gh, p, h)
        return jnp.einsum('xgph,gk->xghkp', m, eye).reshape(SSM_HALVES, gh * h, gh * p).astype(BF16)

    def blockdiag_out(m):
        m = m.reshape(SSM_HALVES, gh, h, p)
        return jnp.einsum('xghp,gk->xgpkh', m, eye).reshape(SSM_HALVES, gh * p, gh * h).astype(BF16)

    return (blockdiag_in(jnp.real(b_bar)), blockdiag_in(jnp.imag(b_bar)),
            jnp.real(lam_bar).reshape(1, g * p), jnp.imag(lam_bar).reshape(1, g * p),
            blockdiag_out(c_re.astype(F32)), blockdiag_out(c_im.astype(F32)),
            d_skip.astype(F32).reshape(1, g * h))


def _ssm_branch(u_tm, ssm_w, w_glu, nb, s, tc=64):
    wbr, wbi, lr, li, wcr, wci, d = ssm_w
    rows = tc * nb
    full = lambda a: pl.BlockSpec(a.shape, lambda i: (0,) * a.ndim)
    return pl.pallas_call(
        functools.partial(_ssm_kernel, nb=nb, tc=tc),
        grid=(s // tc,),
        in_specs=[pl.BlockSpec((rows, SSM_WIDTH), lambda i: (i, 0)),
                  full(wbr), full(wbi), full(lr), full(li), full(wcr), full(wci), full(d), full(w_glu)],
        out_specs=pl.BlockSpec((rows, SSM_WIDTH), lambda i: (i, 0)),
        out_shape=jax.ShapeDtypeStruct((s * nb, SSM_WIDTH), BF16),
        scratch_shapes=[pltpu.VMEM((rows, SSM_COMPLEX), F32), pltpu.VMEM((rows, SSM_COMPLEX), F32),
                        pltpu.VMEM((nb, SSM_COMPLEX), F32), pltpu.VMEM((nb, SSM_COMPLEX), F32)],
        compiler_params=_cparams(("arbitrary",)),
        name="ssm",
    )(u_tm, wbr, wbi, lr, li, wcr, wci, d, w_glu)


def _merge_kernel(attn_ref, ssm_ref, mem_ref, g0_ref, g1_ref, g2_ref, h_ref, wb_ref, wo_ref, lg_ref, lb_ref,
                  o32_ref, slab_ref, *, tm):
    def gated(x_ref, g_ref, lo, hi):
        br = jnp.dot(x_ref[...], wb_ref[lo:hi, :], preferred_element_type=F32)
        return jax.nn.sigmoid(g_ref[...].astype(F32)) * br

    merged = gated(attn_ref, g0_ref, 0, ATTN_WIDTH)
    merged += gated(ssm_ref, g1_ref, ATTN_WIDTH, ATTN_WIDTH + SSM_WIDTH)
    merged += gated(mem_ref, g2_ref, ATTN_WIDTH + SSM_WIDTH, ATTN_WIDTH + SSM_WIDTH + MEM_WIDTH)
    mix = jnp.dot(merged.astype(BF16), wo_ref[...], preferred_element_type=F32)
    y = _layer_norm_rows(DEEPNORM_ALPHA * h_ref[...] + mix, lg_ref[...], lb_ref[...])
    o32_ref[...] = y
    _store_slabs(slab_ref, y, tm)


def _merge_out_ln(attn, ssm, mem, proj, h, w_branch, w_out, ln_g, ln_b, tm=256):
    t, d = h.shape
    row = lambda w, col=0: pl.BlockSpec((tm, w), lambda i: (i, col))
    const = lambda shape: pl.BlockSpec(shape, lambda i: (0, 0))
    return pl.pallas_call(
        functools.partial(_merge_kernel, tm=tm),
        grid=(t // tm,),
        in_specs=[row(ATTN_WIDTH), row(SSM_WIDTH), row(MEM_WIDTH),
                  row(d, G_OFF // d), row(d, G_OFF // d + 1), row(d, G_OFF // d + 2),
                  row(d), const(w_branch.shape), const(w_out.shape), const((1, d)), const((1, d))],
        out_specs=[row(d), pl.BlockSpec((tm * SLAB_ROWS, LANES), lambda i: (i, 0))],
        out_shape=[jax.ShapeDtypeStruct((t, d), F32), jax.ShapeDtypeStruct((t * SLAB_ROWS, LANES), U32)],
        compiler_params=_cparams(("parallel",)),
        name="merge_out_ln1",
    )(attn, ssm, mem, proj, proj, proj, h, w_branch, w_out, ln_g.reshape(1, d), ln_b.reshape(1, d))


def _split_bf16(x):
    hi = x.astype(BF16)
    return hi, (x - hi.astype(F32)).astype(BF16)


def _router_kernel(h_ref, wt_ref, b_ref, idx_ref, gate_ref, rank_ref, cnt_ref, carry_ref, *, tm):
    @pl.when(pl.program_id(0) == 0)
    def _():
        carry_ref[...] = jnp.zeros_like(carry_ref)

    nt = (((1,), (1,)), ((), ()))
    h_hi, h_lo = _split_bf16(h_ref[...])
    w_hi, w_lo = _split_bf16(wt_ref[...])
    lg = (lax.dot_general(w_hi, h_hi, nt, preferred_element_type=F32)
          + lax.dot_general(w_hi, h_lo, nt, preferred_element_type=F32)
          + lax.dot_general(w_lo, h_hi, nt, preferred_element_type=F32)) + b_ref[...]

    e_iota = lax.broadcasted_iota(I32, (N_EXPERTS, tm), 0)
    chosen = jnp.zeros((N_EXPERTS, tm), F32)
    vals, sels = [], []
    for k in range(TOP_K):
        m = jnp.max(lg, axis=0, keepdims=True)
        idx = jnp.min(jnp.where(lg == m, e_iota, N_EXPERTS), axis=0, keepdims=True)
        sel = e_iota == idx
        idx_ref[k:k + 1, :] = idx
        vals.append(m)
        sels.append(sel)
        chosen = jnp.where(sel, 1.0, chosen)
        lg = jnp.where(sel, -jnp.inf, lg)

    ex = [jnp.exp(v - vals[0]) for v in vals]
    inv = 1.0 / (ex[0] + ex[1] + ex[2] + ex[3])
    for k in range(TOP_K):
        gate_ref[k:k + 1, :] = ex[k] * inv

    r = lax.broadcasted_iota(I32, (tm, tm), 0)
    c = lax.broadcasted_iota(I32, (tm, tm), 1)
    before = jnp.where(r < c, 1.0, 0.0).astype(BF16)
    earlier = jnp.dot(chosen.astype(BF16), before, preferred_element_type=F32) + carry_ref[...]
    for k in range(TOP_K):
        rank_ref[k:k + 1, :] = jnp.sum(jnp.where(sels[k], earlier, 0.0), axis=0, keepdims=True).astype(I32)
    carry_ref[...] += jnp.sum(chosen, axis=1, keepdims=True)
    cnt_ref[...] = carry_ref[...].astype(I32)


def _router(h, w_router, b_router, tm=512):
    t, d = h.shape
    out = lambda dt: jax.ShapeDtypeStruct((TOP_K, t), dt)
    tok = pl.BlockSpec((TOP_K, tm), lambda i: (0, i))
    return pl.pallas_call(
        functools.partial(_router_kernel, tm=tm),
        grid=(t // tm,),
        in_specs=[pl.BlockSpec((tm, d), lambda i: (i, 0)),
                  pl.BlockSpec((N_EXPERTS, d), lambda i: (0, 0)),
                  pl.BlockSpec((N_EXPERTS, 1), lambda i: (0, 0))],
        out_specs=[tok, tok, tok, pl.BlockSpec((N_EXPERTS, 1), lambda i: (0, 0))],
        out_shape=[out(I32), out(F32), out(I32), jax.ShapeDtypeStruct((N_EXPERTS, 1), I32)],
        scratch_shapes=[pltpu.VMEM((N_EXPERTS, 1), F32)],
        compiler_params=_cparams(("arbitrary",)),
        name="router",
    )(h, w_router.T, b_router.reshape(N_EXPERTS, 1))


def _slab(ref, index):
    return ref.at[pl.ds(pl.multiple_of(index * SLAB_ROWS, SLAB_ROWS), SLAB_ROWS)]


def _dispatch_kernel(dest_ref, src_ref, zero_ref, x_ref, sem, *, tm):
    del zero_ref
    base = pl.program_id(0) * tm

    def copy(t, dst):
        return pltpu.make_async_copy(_slab(src_ref, base + t), _slab(x_ref, dst), sem)

    def start(g, c):
        toks = [g * DMA_GROUP + j for j in range(DMA_GROUP)]
        dsts = [[dest_ref[t * TOP_K + k] for k in range(TOP_K)] for t in toks]
        for t, row in zip(toks, dsts):
            for dst in row:
                copy(t, dst).start()
        return c

    def wait(t, c):
        for k in range(TOP_K):
            copy(t, 0).wait()
        return c

    lax.fori_loop(0, tm // DMA_GROUP, start, 0)
    lax.fori_loop(0, tm, wait, 0, unroll=4)


def _dispatch(h_slabs, dest, n_rows, tm=1024):
    t = dest.shape[0] // TOP_K
    shape = jax.ShapeDtypeStruct((n_rows * SLAB_ROWS, LANES), U32)
    return pl.pallas_call(
        functools.partial(_dispatch_kernel, tm=tm),
        grid=(t // tm,),
        in_specs=[pl.BlockSpec((TOP_K * tm,), lambda i: (i,), memory_space=pltpu.SMEM),
                  pl.BlockSpec(memory_space=pl.ANY),
                  pl.BlockSpec(memory_space=pl.ANY)],
        out_specs=pl.BlockSpec(memory_space=pl.ANY),
        out_shape=shape,
        scratch_shapes=[pltpu.SemaphoreType.DMA(())],
        input_output_aliases={2: 0},
        compiler_params=_cparams(("arbitrary",)),
        name="moe_dispatch",
    )(dest, h_slabs, jnp.zeros(shape.shape, shape.dtype))


UP_CHUNK = 256
UP_HALF = UP_CHUNK // 2


def _pair_split_kernel(w_ref, o_ref):
    r = lax.broadcasted_iota(I32, (UP_CHUNK, UP_CHUNK), 0)
    c = lax.broadcasted_iota(I32, (UP_CHUNK, UP_CHUNK), 1)
    src = jnp.where(c < UP_HALF, 2 * c, 2 * (c - UP_HALF) + 1)
    perm = jnp.where(r == src, 1.0, 0.0).astype(BF16)
    for ch in range(w_ref.shape[-1] // UP_CHUNK):
        cols = slice(ch * UP_CHUNK, (ch + 1) * UP_CHUNK)
        o_ref[0, :, cols] = jnp.dot(w_ref[0, :, cols].astype(BF16), perm, preferred_element_type=F32).astype(BF16)


def _pair_split_up(w_up, layer, tn=1024):
    _, e, d, n = w_up.shape
    return pl.pallas_call(
        _pair_split_kernel,
        grid=(e, n // tn),
        in_specs=[pl.BlockSpec((None, 1, d, tn), lambda i, j: (layer, i, 0, j))],
        out_specs=pl.BlockSpec((1, d, tn), lambda i, j: (i, 0, j)),
        out_shape=jax.ShapeDtypeStruct((e, d, n), BF16),
        compiler_params=_cparams(("parallel", "parallel")),
        name="moe_up_pair_split",
    )(w_up)


def _pair_split_bias(b_up):
    e, n = b_up.shape
    return b_up.reshape(e, n // UP_CHUNK, UP_HALF, 2).transpose(0, 1, 3, 2).reshape(e, 1, n).astype(F32)


def _expert_kernel(be_ref, nu_ref, x_ref, wu_ref, bu_ref, wd_ref, bd_ref, o_ref):
    del be_ref
    i = pl.program_id(0)

    @pl.when(i < nu_ref[0])
    def _():
        los, his = _load_slabs(x_ref, 0, MOE_BLOCK)
        x = jnp.concatenate([p.astype(BF16) for p in los + his], axis=-1)
        up = jnp.dot(x, wu_ref[0], preferred_element_type=F32) + bu_ref[0]
        acts = []
        for ch in range(up.shape[-1] // UP_CHUNK):
            x_glu = jnp.minimum(up[:, ch * UP_CHUNK:ch * UP_CHUNK + UP_HALF], SWIGLU_LIMIT)
            x_lin = jnp.clip(up[:, ch * UP_CHUNK + UP_HALF:(ch + 1) * UP_CHUNK], -SWIGLU_LIMIT, SWIGLU_LIMIT)
            acts.append((x_glu * jax.nn.sigmoid(SWIGLU_ALPHA * x_glu) * (x_lin + 1.0)).astype(BF16))
        act = jnp.concatenate(acts, axis=-1)
        out = jnp.dot(act, wd_ref[0], preferred_element_type=F32) + bd_ref[0]
        _store_slabs(o_ref, out, MOE_BLOCK)

    @pl.when(i >= nu_ref[0])
    def _():
        o_ref[...] = jnp.zeros_like(o_ref)


def _experts(x_pad, block_e, n_used, wu, bu, wd, bd):
    n_blocks = x_pad.shape[0] // (MOE_BLOCK * SLAB_ROWS)
    row = pl.BlockSpec((MOE_BLOCK * SLAB_ROWS, LANES), lambda i, be, nu: (i, 0))
    per_e = lambda a: pl.BlockSpec((1,) + a.shape[1:], lambda i, be, nu: (be[i], 0, 0))
    return pl.pallas_call(
        _expert_kernel,
        grid_spec=pltpu.PrefetchScalarGridSpec(
            num_scalar_prefetch=2,
            grid=(n_blocks,),
            in_specs=[row, per_e(wu), per_e(bu), per_e(wd), per_e(bd)],
            out_specs=row,
        ),
        out_shape=jax.ShapeDtypeStruct(x_pad.shape, U32),
        compiler_params=_cparams(("arbitrary",)),
        name="moe_experts",
    )(block_e, n_used, x_pad, wu, bu, wd, bd)


def _combine_kernel(dest_ref, next_dest_ref, y_ref, gate_ref, h_ref, lg_ref, lb_ref, o32_ref, o16_ref,
                    buf_ref, sems, *, tm):
    i = pl.program_id(0)
    slot = i % 2

    def copy(src, t, k, sl):
        return pltpu.make_async_copy(_slab(y_ref, src), _slab(buf_ref.at[sl], k * tm + t), sems.at[sl])

    def start_tile(d_ref, sl):
        def body(g, c):
            toks = [g * DMA_GROUP + j for j in range(DMA_GROUP)]
            srcs = [[d_ref[t * TOP_K + k] for k in range(TOP_K)] for t in toks]
            for t, row in zip(toks, srcs):
                for k, src in enumerate(row):
                    copy(src, t, k, sl).start()
            return c
        lax.fori_loop(0, tm // DMA_GROUP, body, 0)

    @pl.when(i == 0)
    def _():
        start_tile(dest_ref, 0)

    @pl.when(i + 1 < pl.num_programs(0))
    def _():
        start_tile(next_dest_ref, 1 - slot)

    def wait(t, c):
        for k in range(TOP_K):
            copy(0, t, k, slot).wait()
        return c

    lax.fori_loop(0, tm, wait, 0, unroll=4)

    g = gate_ref[...]
    gk = [jnp.broadcast_to(g[:, k:k + 1], (tm, LANES)) for k in range(TOP_K)]
    lo_acc, hi_acc = None, None
    for k in range(TOP_K):
        los, his = _load_slabs(buf_ref.at[slot], k * tm * SLAB_ROWS, tm)
        los = [p * gk[k] for p in los]
        his = [p * gk[k] for p in his]
        lo_acc = los if lo_acc is None else [a + p for a, p in zip(lo_acc, los)]
        hi_acc = his if hi_acc is None else [a + p for a, p in zip(hi_acc, his)]
    ffn = jnp.concatenate(lo_acc + hi_acc, axis=-1)
    y = _layer_norm_rows(DEEPNORM_ALPHA * h_ref[...] + ffn, lg_ref[...], lb_ref[...])
    o32_ref[...] = y
    o16_ref[...] = y.astype(BF16)


def _combine_ln(out_pad, dest, gate_t, h, ln_g, ln_b, tm=256):
    t, d = h.shape
    n = t // tm
    row = pl.BlockSpec((tm, d), lambda i: (i, 0))
    vec = pl.BlockSpec((1, d), lambda i: (0, 0))
    return pl.pallas_call(
        functools.partial(_combine_kernel, tm=tm),
        grid=(n,),
        in_specs=[pl.BlockSpec((TOP_K * tm,), lambda i: (i,), memory_space=pltpu.SMEM),
                  pl.BlockSpec((TOP_K * tm,), lambda i: (jnp.minimum(i + 1, n - 1),), memory_space=pltpu.SMEM),
                  pl.BlockSpec(memory_space=pl.ANY),
                  pl.BlockSpec((tm, TOP_K), lambda i: (i, 0)),
                  row, vec, vec],
        out_specs=[row, row],
        out_shape=[jax.ShapeDtypeStruct((t, d), F32), jax.ShapeDtypeStruct((t, d), BF16)],
        scratch_shapes=[pltpu.VMEM((2, TOP_K * tm * SLAB_ROWS, LANES), U32), pltpu.SemaphoreType.DMA((2,))],
        compiler_params=_cparams(("arbitrary",)),
        name="moe_combine_ln2",
    )(dest, dest, out_pad, gate_t, h, ln_g.reshape(1, d), ln_b.reshape(1, d))


def _moe_layout(idx, rank, counts, n_blocks):
    counts = counts.reshape(N_EXPERTS)
    padded = (counts + MOE_BLOCK - 1) // MOE_BLOCK * MOE_BLOCK
    pad_end = jnp.cumsum(padded)
    pad_start = pad_end - padded
    experts = jnp.arange(N_EXPERTS, dtype=I32)[:, None, None]
    dest = rank + jnp.sum(jnp.where(idx[None] == experts, pad_start[:, None, None], 0), axis=0)
    block_rows = jnp.arange(n_blocks, dtype=I32) * MOE_BLOCK
    block_e = jnp.minimum(jnp.sum(block_rows[:, None] >= pad_end[None, :], axis=1), N_EXPERTS - 1).astype(I32)
    n_used = (pad_end[-1:] // MOE_BLOCK).astype(I32)
    return dest.astype(I32).T.reshape(-1), block_e, n_used


def _permute_w_in(w):
    pieces = [w[:, QM_END:], w[:, :Q_END], w[:, V_END:U_END], w[:, U_END:QM_END], w[:, Q_END:K_END], w[:, K_END:V_END]]
    return jnp.concatenate(pieces, axis=1).astype(BF16)


def kernel(x, mem, ln_in_g, ln_in_b, w_in, attn_sinks, ssm_lambda_re, ssm_lambda_im, ssm_log_dt, ssm_b_re, ssm_b_im, ssm_c_re, ssm_c_im, ssm_d, w_glu, w_mem_kv, w_branch, w_out, ln1_g, ln1_b, w_router, b_router, w_up, b_up, w_down, b_down, ln2_g, ln2_b):
    b, s, d = x.shape
    t = b * s
    n_blocks = -(-(t * TOP_K) // MOE_BLOCK) + N_EXPERTS
    mem16 = mem.reshape(b * N_MEM, d).astype(BF16)

    h32, h16 = _layer_norm_in(x.reshape(t, d), ln_in_g, ln_in_b)
    for l in range(DEPTH):
        proj = _matmul(h16, _permute_w_in(w_in[l]), tm=1024, tn=768, name="in_proj")
        mem_kv = _matmul(mem16, w_mem_kv[l].astype(BF16), tm=1024, tn=512, name="mem_kv")

        attn = _sliding_window_attention(proj, attn_sinks[l].astype(F32), b, s)
        mem_out = _memory_attention(proj, mem_kv, b, s)
        ssm_w = _ssm_weights(ssm_lambda_re[l], ssm_lambda_im[l], ssm_log_dt[l], ssm_b_re[l], ssm_b_im[l],
                             ssm_c_re[l], ssm_c_im[l], ssm_d[l])
        u_tm = proj[:, U_OFF:U_OFF + SSM_WIDTH].reshape(b, s, SSM_WIDTH).transpose(1, 0, 2).reshape(t, SSM_WIDTH)
        ssm_tm = _ssm_branch(u_tm, ssm_w, w_glu[l].astype(BF16), b, s)
        ssm_out = ssm_tm.reshape(s, b, SSM_WIDTH).transpose(1, 0, 2).reshape(t, SSM_WIDTH)

        h1, h1_slabs = _merge_out_ln(attn, ssm_out, mem_out, proj, h32, w_branch[l].astype(BF16),
                                     w_out[l].astype(BF16), ln1_g[l], ln1_b[l])

        idx, gate, rank, counts = _router(h1, w_router[l], b_router[l])
        dest, block_e, n_used = _moe_layout(idx, rank, counts, n_blocks)
        x_pad = _dispatch(h1_slabs, dest, n_blocks * MOE_BLOCK)
        out_pad = _experts(x_pad, block_e, n_used, _pair_split_up(w_up, l), _pair_split_bias(b_up[l]),
                           w_down[l].astype(BF16), b_down[l][:, None, :].astype(F32))
        h32, h16 = _combine_ln(out_pad, dest, gate.T, h1, ln2_g[l], ln2_b[l])
    return h32.reshape(b, s, d)
```

```python
import functools
import math

import jax
import jax.numpy as jnp
from jax import lax
from jax.experimental import pallas as pl
from jax.experimental.pallas import tpu as pltpu

F32 = jnp.float32
BF16 = jnp.bfloat16
I32 = jnp.int32

D_MODEL = 2048
DEPTH = 2
N_HEADS = 16
N_KV_HEADS = 2
HEAD_DIM = 64
WINDOW = 128
BLOCK = 128
ATTN_WIDTH = N_HEADS * HEAD_DIM
KV_WIDTH = N_KV_HEADS * HEAD_DIM
SSM_WIDTH = D_MODEL // 4
SSM_GROUP = 16
N_SSM_GROUPS = SSM_WIDTH // SSM_GROUP
SSM_STATE = 64
N_MEM = 256
MEM_HEADS = 4
MEM_HEAD_DIM = 128
MEM_WIDTH = MEM_HEADS * MEM_HEAD_DIM
N_BRANCHES = 3
Q_END = ATTN_WIDTH
K_END = Q_END + KV_WIDTH
V_END = K_END + KV_WIDTH
U_END = V_END + SSM_WIDTH
QM_END = U_END + MEM_WIDTH
IN_WIDTH = QM_END + N_BRANCHES * D_MODEL
N_EXPERTS = 32
TOP_K = 4
D_FF = D_MODEL // 2
MOE_BLOCK = 512
SWIGLU_ALPHA = 1.702
SWIGLU_LIMIT = 7.0
LN_EPS = 1e-5
DEEPNORM_ALPHA = (2.0 * DEPTH) ** 0.25

G_OFF = 0
Q_OFF = N_BRANCHES * D_MODEL
U_OFF = Q_OFF + ATTN_WIDTH
QM_OFF = U_OFF + SSM_WIDTH
K_OFF = QM_OFF + MEM_WIDTH
V_OFF = K_OFF + KV_WIDTH

SSM_COMPLEX = N_SSM_GROUPS * SSM_STATE
SSM_HALVES = 2
SSM_SCAN_STRIP = 512

VMEM_LIMIT = 56 * 1024 * 1024


def _cparams(sem):
    return pltpu.CompilerParams(dimension_semantics=sem, vmem_limit_bytes=VMEM_LIMIT)


def _layer_norm_rows(x, g, b):
    mu = jnp.mean(x, axis=-1, keepdims=True)
    xc = x - mu
    var = jnp.mean(xc * xc, axis=-1, keepdims=True)
    return xc * lax.rsqrt(var + LN_EPS) * g + b


LANES = 128
SLAB_ROWS = D_MODEL // 2 // LANES
DMA_GROUP = 4
U32 = jnp.uint32
HIGH_HALF = 0xFFFF0000


def _pack_pair(lo, hi):
    lo = lax.bitcast_convert_type(lo.astype(BF16).astype(F32), U32) >> 16
    hi = lax.bitcast_convert_type(hi.astype(BF16).astype(F32), U32) & U32(HIGH_HALF)
    return lo | hi


def _unpack_pair(w):
    return lax.bitcast_convert_type(w << 16, F32), lax.bitcast_convert_type(w & U32(HIGH_HALF), F32)


def _store_slabs(slab_ref, x, n):
    half = D_MODEL // 2
    for c in range(SLAB_ROWS):
        lo = x[:, c * LANES:(c + 1) * LANES]
        hi = x[:, half + c * LANES:half + (c + 1) * LANES]
        slab_ref[pl.ds(c, n, stride=SLAB_ROWS), :] = _pack_pair(lo, hi)


def _load_slabs(slab_ref, first_row, n):
    los, his = [], []
    for c in range(SLAB_ROWS):
        lo, hi = _unpack_pair(slab_ref[pl.ds(first_row + c, n, stride=SLAB_ROWS), :])
        los.append(lo)
        his.append(hi)
    return los, his


def _ln_kernel(x_ref, g_ref, b_ref, o32_ref, o16_ref):
    y = _layer_norm_rows(x_ref[...], g_ref[...], b_ref[...])
    o32_ref[...] = y
    o16_ref[...] = y.astype(BF16)


def _layer_norm_in(x, g, b, tm=512):
    t, d = x.shape
    row = pl.BlockSpec((tm, d), lambda i: (i, 0))
    vec = pl.BlockSpec((1, d), lambda i: (0, 0))
    return pl.pallas_call(
        _ln_kernel,
        grid=(t // tm,),
        in_specs=[row, vec, vec],
        out_specs=[row, row],
        out_shape=[jax.ShapeDtypeStruct((t, d), F32), jax.ShapeDtypeStruct((t, d), BF16)],
        compiler_params=_cparams(("parallel",)),
        name="ln_in",
    )(x, g.reshape(1, d), b.reshape(1, d))


def _mm_kernel(a_ref, w_ref, o_ref):
    o_ref[...] = jnp.dot(a_ref[...], w_ref[...], preferred_element_type=F32).astype(o_ref.dtype)


def _matmul(a, w, tm, tn, name):
    m, k = a.shape
    n = w.shape[1]
    return pl.pallas_call(
        _mm_kernel,
        grid=(m // tm, n // tn),
        in_specs=[pl.BlockSpec((tm, k), lambda i, j: (i, 0)), pl.BlockSpec((k, tn), lambda i, j: (0, j))],
        out_specs=pl.BlockSpec((tm, tn), lambda i, j: (i, j)),
        out_shape=jax.ShapeDtypeStruct((m, n), BF16),
        compiler_params=_cparams(("parallel", "parallel")),
        name=name,
    )(a, w)


def _swa_kernel(sink_ref, q_ref, kc_ref, vc_ref, kp_ref, vp_ref, o_ref, *, nblk):
    i = pl.program_id(1)
    grp = N_HEADS // N_KV_HEADS
    pairs = grp // 2
    pw = 2 * HEAD_DIM
    nrow = pairs * BLOCK
    nkey = 2 * BLOCK
    log2e = math.log2(math.e)
    scale = HEAD_DIM ** -0.5 * log2e
    row = lax.broadcasted_iota(I32, (nrow, nkey), 0)
    kj = lax.broadcasted_iota(I32, (nrow, nkey), 1)
    rel = (row & (BLOCK - 1)) + BLOCK - kj
    band_ok = (rel >= 0) & (rel < WINDOW)
    row_pair = lax.broadcasted_iota(I32, (nrow, 1), 0) // BLOCK
    low_lanes = lax.broadcasted_iota(I32, (nkey, pw), 1) < HEAD_DIM
    first_lo = jnp.where(i > 0, 0, BLOCK)

    def block_diag(band, g):
        band = band.astype(F32)
        swapped = pltpu.roll(band, HEAD_DIM, axis=1)
        top = jnp.where(low_lanes, band if g == 0 else swapped, 0.0)
        bottom = jnp.where(low_lanes, 0.0, swapped if g == 0 else band)
        return jnp.concatenate([top, bottom], axis=0).astype(BF16)

    for j in range(nblk):
        rows = slice(j * BLOCK, (j + 1) * BLOCK)
        if j == 0:
            k_prev, v_prev = kp_ref[...], vp_ref[...]
            valid = band_ok & (kj >= first_lo)
        else:
            prev = slice((j - 1) * BLOCK, j * BLOCK)
            k_prev, v_prev = kc_ref[prev, :], vc_ref[prev, :]
            valid = band_ok
        k_band = jnp.concatenate([k_prev, kc_ref[rows, :]], axis=0)
        v_band = jnp.concatenate([v_prev, vc_ref[rows, :]], axis=0)
        for g in range(N_KV_HEADS):
            k2 = block_diag(k_band, g)
            v2 = block_diag(v_band, g)
            cols = [slice((g * pairs + p) * pw, (g * pairs + p + 1) * pw) for p in range(pairs)]
            q2 = jnp.concatenate([q_ref[rows, c] for c in cols], axis=0)
            s = lax.dot_general(q2, k2, (((1,), (1,)), ((), ())), preferred_element_type=F32) * scale
            probs = []
            for half in range(2):
                sh = jnp.where(valid, s[:, half * nkey:(half + 1) * nkey], -jnp.inf)
                sink = sink_ref[g * grp + half] * log2e
                for p in range(1, pairs):
                    sink = jnp.where(row_pair == p, sink_ref[g * grp + 2 * p + half] * log2e, sink)
                m = jnp.maximum(jnp.max(sh, axis=-1, keepdims=True), sink)
                e = jnp.exp2(sh - m)
                den = jnp.sum(e, axis=-1, keepdims=True) + jnp.exp2(sink - m)
                probs.append((e * (1.0 / den)).astype(BF16))
            out = jnp.dot(jnp.concatenate(probs, axis=-1), v2, preferred_element_type=F32)
            for p, c in enumerate(cols):
                o_ref[rows, c] = out[p * BLOCK:(p + 1) * BLOCK, :].astype(o_ref.dtype)


def _sliding_window_attention(proj, sinks, b, s, tq=512):
    t = b * s
    nq = s // tq
    nblk = tq // BLOCK
    sb = s // BLOCK
    cur = lambda col: (lambda bi, i: (bi * nq + i, col))
    prev = lambda col: (lambda bi, i: (bi * sb + jnp.maximum(i * nblk - 1, 0), col))
    return pl.pallas_call(
        functools.partial(_swa_kernel, nblk=nblk),
        grid=(b, nq),
        in_specs=[
            pl.BlockSpec(memory_space=pltpu.SMEM),
            pl.BlockSpec((tq, ATTN_WIDTH), cur(Q_OFF // ATTN_WIDTH)),
            pl.BlockSpec((tq, KV_WIDTH), cur(K_OFF // KV_WIDTH)),
            pl.BlockSpec((tq, KV_WIDTH), cur(V_OFF // KV_WIDTH)),
            pl.BlockSpec((BLOCK, KV_WIDTH), prev(K_OFF // KV_WIDTH)),
            pl.BlockSpec((BLOCK, KV_WIDTH), prev(V_OFF // KV_WIDTH)),
        ],
        out_specs=pl.BlockSpec((tq, ATTN_WIDTH), lambda bi, i: (bi * nq + i, 0)),
        out_shape=jax.ShapeDtypeStruct((t, ATTN_WIDTH), BF16),
        compiler_params=_cparams(("parallel", "parallel")),
        name="swa",
    )(sinks, proj, proj, proj, proj, proj)


def _mem_attn_kernel(q_ref, k_ref, v_ref, o_ref):
    s = lax.dot_general(q_ref[...], k_ref[...], (((1,), (1,)), ((), ())),
                        preferred_element_type=F32) * (MEM_HEAD_DIM ** -0.5)
    m = jnp.max(s, axis=-1, keepdims=True)
    e = jnp.exp(s - m)
    p = (e * (1.0 / jnp.sum(e, axis=-1, keepdims=True))).astype(BF16)
    o_ref[...] = jnp.dot(p, v_ref[...], preferred_element_type=F32).astype(o_ref.dtype)


def _memory_attention(proj, mem_kv, b, s):
    t = b * s
    qcol = QM_OFF // MEM_HEAD_DIM
    return pl.pallas_call(
        _mem_attn_kernel,
        grid=(b, MEM_HEADS),
        in_specs=[
            pl.BlockSpec((s, MEM_HEAD_DIM), lambda bi, h: (bi, qcol + h)),
            pl.BlockSpec((N_MEM, MEM_HEAD_DIM), lambda bi, h: (bi, h)),
            pl.BlockSpec((N_MEM, MEM_HEAD_DIM), lambda bi, h: (bi, MEM_HEADS + h)),
        ],
        out_specs=pl.BlockSpec((s, MEM_HEAD_DIM), lambda bi, h: (bi, h)),
        out_shape=jax.ShapeDtypeStruct((t, MEM_WIDTH), BF16),
        compiler_params=_cparams(("parallel", "parallel")),
        name="mem_attn",
    )(proj, mem_kv, mem_kv)


def _gelu_tanh(x):
    return 0.5 * x * (1.0 + jnp.tanh(math.sqrt(2.0 / math.pi) * (x + 0.044715 * (x * x * x))))


def _ssm_kernel(u_ref, wbr_ref, wbi_ref, lr_ref, li_ref, wcr_ref, wci_ref, d_ref, wglu_ref, o_ref,
                xr_ref, xi_ref, sr_ref, si_ref, *, nb, tc):
    @pl.when(pl.program_id(0) == 0)
    def _():
        sr_ref[...] = jnp.zeros_like(sr_ref)
        si_ref[...] = jnp.zeros_like(si_ref)

    u = u_ref[...]
    uw = SSM_WIDTH // SSM_HALVES
    xw = SSM_COMPLEX // SSM_HALVES
    for hf in range(SSM_HALVES):
        uh = u[:, hf * uw:(hf + 1) * uw]
        xr_ref[:, hf * xw:(hf + 1) * xw] = jnp.dot(uh, wbr_ref[hf], preferred_element_type=F32)
        xi_ref[:, hf * xw:(hf + 1) * xw] = jnp.dot(uh, wbi_ref[hf], preferred_element_type=F32)

    for st in range(SSM_COMPLEX // SSM_SCAN_STRIP):
        cols = slice(st * SSM_SCAN_STRIP, (st + 1) * SSM_SCAN_STRIP)
        lr = jnp.broadcast_to(lr_ref[:, cols], (nb, SSM_SCAN_STRIP))
        li = jnp.broadcast_to(li_ref[:, cols], (nb, SSM_SCAN_STRIP))

        def step(t, carry):
            sr, si = carry
            r0 = pl.multiple_of(t * nb, nb)
            nr = lr * sr - li * si + xr_ref[pl.ds(r0, nb), cols]
            ni = lr * si + li * sr + xi_ref[pl.ds(r0, nb), cols]
            xr_ref[pl.ds(r0, nb), cols] = nr
            xi_ref[pl.ds(r0, nb), cols] = ni
            return nr, ni

        sr, si = lax.fori_loop(0, tc, step, (sr_ref[:, cols], si_ref[:, cols]), unroll=8)
        sr_ref[:, cols] = sr
        si_ref[:, cols] = si

    ys = []
    for hf in range(SSM_HALVES):
        xs = slice(hf * xw, (hf + 1) * xw)
        yr = jnp.dot(xr_ref[:, xs].astype(BF16), wcr_ref[hf], preferred_element_type=F32)
        yi = jnp.dot(xi_ref[:, xs].astype(BF16), wci_ref[hf], preferred_element_type=F32)
        ys.append(yr - yi)
    y = jnp.concatenate(ys, axis=-1) + d_ref[...] * u.astype(F32)
    zg = jnp.dot(_gelu_tanh(y).astype(BF16), wglu_ref[...], preferred_element_type=F32)
    o_ref[...] = (zg[:, :SSM_WIDTH] * jax.nn.sigmoid(zg[:, SSM_WIDTH:])).astype(o_ref.dtype)


def _ssm_weights(lambda_re, lambda_im, log_dt, b_re, b_im, c_re, c_im, d_skip):
    g, p, h = N_SSM_GROUPS, SSM_STATE, SSM_GROUP
    gh = g // SSM_HALVES
    lam = lax.complex(lambda_re.astype(F32), lambda_im.astype(F32))
    dt = jnp.exp(log_dt.astype(F32))[:, None]
    lam_bar = jnp.exp(lam * dt)
    b_bar = ((lam_bar - 1.0) / lam)[..., None] * lax.complex(b_re.astype(F32), b_im.astype(F32))
    eye = jnp.eye(gh, dtype=F32)

    def blockdiag_in(m):
        m = m.reshape(SSM_HALVES, gh, p, h)
        return jnp.einsum('xgph,gk->xghkp', m, eye).reshape(SSM_HALVES, gh * h, gh * p).astype(BF16)

    def blockdiag_out(m):
        m = m.reshape(SSM_HALVES, gh, h, p)
        return jnp.einsum('xghp,gk->xgpkh', m, eye).reshape(SSM_HALVES, gh * p, gh * h).astype(BF16)

    return (blockdiag_in(jnp.real(b_bar)), blockdiag_in(jnp.imag(b_bar)),
            jnp.real(lam_bar).reshape(1, g * p), jnp.imag(lam_bar).reshape(1, g * p),
            blockdiag_out(c_re.astype(F32)), blockdiag_out(c_im.astype(F32)),
            d_skip.astype(F32).reshape(1, g * h))


def _ssm_branch(u_tm, ssm_w, w_glu, nb, s, tc=64):
    wbr, wbi, lr, li, wcr, wci, d = ssm_w
    rows = tc * nb
    full = lambda a: pl.BlockSpec(a.shape, lambda i: (0,) * a.ndim)
    return pl.pallas_call(
        functools.partial(_ssm_kernel, nb=nb, tc=tc),
        grid=(s // tc,),
        in_specs=[pl.BlockSpec((rows, SSM_WIDTH), lambda i: (i, 0)),
                  full(wbr), full(wbi), full(lr), full(li), full(wcr), full(wci), full(d), full(w_glu)],
        out_specs=pl.BlockSpec((rows, SSM_WIDTH), lambda i: (i, 0)),
        out_shape=jax.ShapeDtypeStruct((s * nb, SSM_WIDTH), BF16),
        scratch_shapes=[pltpu.VMEM((rows, SSM_COMPLEX), F32), pltpu.VMEM((rows, SSM_COMPLEX), F32),
                        pltpu.VMEM((nb, SSM_COMPLEX), F32), pltpu.VMEM((nb, SSM_COMPLEX), F32)],
        compiler_params=_cparams(("arbitrary",)),
        name="ssm",
    )(u_tm, wbr, wbi, lr, li, wcr, wci, d, w_glu)


def _merge_kernel(attn_ref, ssm_ref, mem_ref, g0_ref, g1_ref, g2_ref, h_ref, wb_ref, wo_ref, lg_ref, lb_ref,
                  o32_ref, slab_ref, *, tm):
    def gated(x_ref, g_ref, lo, hi):
        br = jnp.dot(x_ref[...], wb_ref[lo:hi, :], preferred_element_type=F32)
        return jax.nn.sigmoid(g_ref[...].astype(F32)) * br

    merged = gated(attn_ref, g0_ref, 0, ATTN_WIDTH)
    merged += gated(ssm_ref, g1_ref, ATTN_WIDTH, ATTN_WIDTH + SSM_WIDTH)
    merged += gated(mem_ref, g2_ref, ATTN_WIDTH + SSM_WIDTH, ATTN_WIDTH + SSM_WIDTH + MEM_WIDTH)
    mix = jnp.dot(merged.astype(BF16), wo_ref[...], preferred_element_type=F32)
    y = _layer_norm_rows(DEEPNORM_ALPHA * h_ref[...] + mix, lg_ref[...], lb_ref[...])
    o32_ref[...] = y
    _store_slabs(slab_ref, y, tm)


def _merge_out_ln(attn, ssm, mem, proj, h, w_branch, w_out, ln_g, ln_b, tm=256):
    t, d = h.shape
    row = lambda w, col=0: pl.BlockSpec((tm, w), lambda i: (i, col))
    const = lambda shape: pl.BlockSpec(shape, lambda i: (0, 0))
    return pl.pallas_call(
        functools.partial(_merge_kernel, tm=tm),
        grid=(t // tm,),
        in_specs=[row(ATTN_WIDTH), row(SSM_WIDTH), row(MEM_WIDTH),
                  row(d, G_OFF // d), row(d, G_OFF // d + 1), row(d, G_OFF // d + 2),
                  row(d), const(w_branch.shape), const(w_out.shape), const((1, d)), const((1, d))],
        out_specs=[row(d), pl.BlockSpec((tm * SLAB_ROWS, LANES), lambda i: (i, 0))],
        out_shape=[jax.ShapeDtypeStruct((t, d), F32), jax.ShapeDtypeStruct((t * SLAB_ROWS, LANES), U32)],
        compiler_params=_cparams(("parallel",)),
        name="merge_out_ln1",
    )(attn, ssm, mem, proj, proj, proj, h, w_branch, w_out, ln_g.reshape(1, d), ln_b.reshape(1, d))


def _split_bf16(x):
    hi = x.astype(BF16)
    return hi, (x - hi.astype(F32)).astype(BF16)


def _router_kernel(h_ref, wt_ref, b_ref, idx_ref, gate_ref, rank_ref, cnt_ref, carry_ref, *, tm):
    @pl.when(pl.program_id(0) == 0)
    def _():
        carry_ref[...] = jnp.zeros_like(carry_ref)

    nt = (((1,), (1,)), ((), ()))
    h_hi, h_lo = _split_bf16(h_ref[...])
    w_hi, w_lo = _split_bf16(wt_ref[...])
    lg = (lax.dot_general(w_hi, h_hi, nt, preferred_element_type=F32)
          + lax.dot_general(w_hi, h_lo, nt, preferred_element_type=F32)
          + lax.dot_general(w_lo, h_hi, nt, preferred_element_type=F32)) + b_ref[...]

    e_iota = lax.broadcasted_iota(I32, (N_EXPERTS, tm), 0)
    chosen = jnp.zeros((N_EXPERTS, tm), F32)
    vals, sels = [], []
    for k in range(TOP_K):
        m = jnp.max(lg, axis=0, keepdims=True)
        idx = jnp.min(jnp.where(lg == m, e_iota, N_EXPERTS), axis=0, keepdims=True)
        sel = e_iota == idx
        idx_ref[k:k + 1, :] = idx
        vals.append(m)
        sels.append(sel)
        chosen = jnp.where(sel, 1.0, chosen)
        lg = jnp.where(sel, -jnp.inf, lg)

    ex = [jnp.exp(v - vals[0]) for v in vals]
    inv = 1.0 / (ex[0] + ex[1] + ex[2] + ex[3])
    for k in range(TOP_K):
        gate_ref[k:k + 1, :] = ex[k] * inv

    r = lax.broadcasted_iota(I32, (tm, tm), 0)
    c = lax.broadcasted_iota(I32, (tm, tm), 1)
    before = jnp.where(r < c, 1.0, 0.0).astype(BF16)
    earlier = jnp.dot(chosen.astype(BF16), before, preferred_element_type=F32) + carry_ref[...]
    for k in range(TOP_K):
        rank_ref[k:k + 1, :] = jnp.sum(jnp.where(sels[k], earlier, 0.0), axis=0, keepdims=True).astype(I32)
    carry_ref[...] += jnp.sum(chosen, axis=1, keepdims=True)
    cnt_ref[...] = carry_ref[...].astype(I32)


def _router(h, w_router, b_router, tm=512):
    t, d = h.shape
    out = lambda dt: jax.ShapeDtypeStruct((TOP_K, t), dt)
    tok = pl.BlockSpec((TOP_K, tm), lambda i: (0, i))
    return pl.pallas_call(
        functools.partial(_router_kernel, tm=tm),
        grid=(t // tm,),
        in_specs=[pl.BlockSpec((tm, d), lambda i: (i, 0)),
                  pl.BlockSpec((N_EXPERTS, d), lambda i: (0, 0)),
                  pl.BlockSpec((N_EXPERTS, 1), lambda i: (0, 0))],
        out_specs=[tok, tok, tok, pl.BlockSpec((N_EXPERTS, 1), lambda i: (0, 0))],
        out_shape=[out(I32), out(F32), out(I32), jax.ShapeDtypeStruct((N_EXPERTS, 1), I32)],
        scratch_shapes=[pltpu.VMEM((N_EXPERTS, 1), F32)],
        compiler_params=_cparams(("arbitrary",)),
        name="router",
    )(h, w_router.T, b_router.reshape(N_EXPERTS, 1))


def _slab(ref, index):
    return ref.at[pl.ds(pl.multiple_of(index * SLAB_ROWS, SLAB_ROWS), SLAB_ROWS)]


ZERO_RUNS = tuple(1 << s for s in reversed(range(MOE_BLOCK.bit_length() - 1)))


def _dispatch_kernel(dest_ref, pad_at_ref, pad_len_ref, src_ref, x_ref, zero_ref, sem, zero_sem, *, tm):
    def zero_fill(act):
        def per_expert(e, c):
            at, n = pad_at_ref[e], pad_len_ref[e]
            for run in ZERO_RUNS:
                take = n & run

                @pl.when(take != 0)
                def _(at=at, run=run):
                    rows = run * SLAB_ROWS
                    dst = x_ref.at[pl.ds(pl.multiple_of(at * SLAB_ROWS, SLAB_ROWS), rows)]
                    act(pltpu.make_async_copy(zero_ref.at[pl.ds(0, rows)], dst, zero_sem))
                at = at + take
            return c
        lax.fori_loop(0, N_EXPERTS, per_expert, 0)

    @pl.when(pl.program_id(0) == 0)
    def _():
        zero_ref[...] = jnp.zeros_like(zero_ref)
        zero_fill(lambda cp: cp.start())
        zero_fill(lambda cp: cp.wait())

    def copy(t, dst):
        return pltpu.make_async_copy(_slab(src_ref, t), _slab(x_ref, dst), sem)

    def start(g, c):
        toks = [g * DMA_GROUP + j for j in range(DMA_GROUP)]
        dsts = [[dest_ref[t * TOP_K + k] for k in range(TOP_K)] for t in toks]
        for t, row in zip(toks, dsts):
            for dst in row:
                copy(t, dst).start()
        return c

    def wait(t, c):
        for k in range(TOP_K):
            copy(t, 0).wait()
        return c

    lax.fori_loop(0, tm // DMA_GROUP, start, 0)
    lax.fori_loop(0, tm, wait, 0, unroll=4)


def _dispatch(h_slabs, dest, pad_at, pad_len, n_rows, tm=1024):
    t = dest.shape[0] // TOP_K
    smem = pl.BlockSpec(memory_space=pltpu.SMEM)
    return pl.pallas_call(
        functools.partial(_dispatch_kernel, tm=tm),
        grid=(t // tm,),
        in_specs=[pl.BlockSpec((TOP_K * tm,), lambda i: (i,), memory_space=pltpu.SMEM), smem, smem,
                  pl.BlockSpec((tm * SLAB_ROWS, LANES), lambda i: (i, 0))],
        out_specs=pl.BlockSpec(memory_space=pl.ANY),
        out_shape=jax.ShapeDtypeStruct((n_rows * SLAB_ROWS, LANES), U32),
        scratch_shapes=[pltpu.VMEM((ZERO_RUNS[0] * SLAB_ROWS, LANES), U32),
                        pltpu.SemaphoreType.DMA(()), pltpu.SemaphoreType.DMA(())],
        compiler_params=_cparams(("arbitrary",)),
        name="moe_dispatch",
    )(dest, pad_at, pad_len, h_slabs)


UP_CHUNK = 256
UP_HALF = UP_CHUNK // 2


def _pair_split_kernel(w_ref, o_ref):
    r = lax.broadcasted_iota(I32, (UP_CHUNK, UP_CHUNK), 0)
    c = lax.broadcasted_iota(I32, (UP_CHUNK, UP_CHUNK), 1)
    src = jnp.where(c < UP_HALF, 2 * c, 2 * (c - UP_HALF) + 1)
    perm = jnp.where(r == src, 1.0, 0.0).astype(BF16)
    for ch in range(w_ref.shape[-1] // UP_CHUNK):
        cols = slice(ch * UP_CHUNK, (ch + 1) * UP_CHUNK)
        o_ref[0, :, cols] = jnp.dot(w_ref[0, :, cols].astype(BF16), perm, preferred_element_type=F32).astype(BF16)


def _pair_split_up(w_up, layer, tn=1024):
    _, e, d, n = w_up.shape
    return pl.pallas_call(
        _pair_split_kernel,
        grid=(e, n // tn),
        in_specs=[pl.BlockSpec((None, 1, d, tn), lambda i, j: (layer, i, 0, j))],
        out_specs=pl.BlockSpec((1, d, tn), lambda i, j: (i, 0, j)),
        out_shape=jax.ShapeDtypeStruct((e, d, n), BF16),
        compiler_params=_cparams(("parallel", "parallel")),
        name="moe_up_pair_split",
    )(w_up)


def _pair_split_bias(b_up):
    e, n = b_up.shape
    return b_up.reshape(e, n // UP_CHUNK, UP_HALF, 2).transpose(0, 1, 3, 2).reshape(e, 1, n).astype(F32)


def _expert_kernel(be_ref, nu_ref, x_ref, wu_ref, bu_ref, wd_ref, bd_ref, o_ref):
    del be_ref
    i = pl.program_id(0)

    @pl.when(i < nu_ref[0])
    def _():
        los, his = _load_slabs(x_ref, 0, MOE_BLOCK)
        x = jnp.concatenate([p.astype(BF16) for p in los + his], axis=-1)
        up = jnp.dot(x, wu_ref[0], preferred_element_type=F32) + bu_ref[0]
        acts = []
        for ch in range(up.shape[-1] // UP_CHUNK):
            x_glu = jnp.minimum(up[:, ch * UP_CHUNK:ch * UP_CHUNK + UP_HALF], SWIGLU_LIMIT)
            x_lin = jnp.clip(up[:, ch * UP_CHUNK + UP_HALF:(ch + 1) * UP_CHUNK], -SWIGLU_LIMIT, SWIGLU_LIMIT)
            acts.append((x_glu * jax.nn.sigmoid(SWIGLU_ALPHA * x_glu) * (x_lin + 1.0)).astype(BF16))
        act = jnp.concatenate(acts, axis=-1)
        out = jnp.dot(act, wd_ref[0], preferred_element_type=F32) + bd_ref[0]
        _store_slabs(o_ref, out, MOE_BLOCK)

    @pl.when(i >= nu_ref[0])
    def _():
        o_ref[...] = jnp.zeros_like(o_ref)


def _experts(x_pad, block_e, n_used, wu, bu, wd, bd):
    n_blocks = x_pad.shape[0] // (MOE_BLOCK * SLAB_ROWS)
    row = pl.BlockSpec((MOE_BLOCK * SLAB_ROWS, LANES), lambda i, be, nu: (i, 0))
    used_row = pl.BlockSpec((MOE_BLOCK * SLAB_ROWS, LANES), lambda i, be, nu: (jnp.minimum(i, nu[0] - 1), 0))
    per_e = lambda a: pl.BlockSpec((1,) + a.shape[1:], lambda i, be, nu: (be[i], 0, 0))
    return pl.pallas_call(
        _expert_kernel,
        grid_spec=pltpu.PrefetchScalarGridSpec(
            num_scalar_prefetch=2,
            grid=(n_blocks,),
            in_specs=[used_row, per_e(wu), per_e(bu), per_e(wd), per_e(bd)],
            out_specs=row,
        ),
        out_shape=jax.ShapeDtypeStruct(x_pad.shape, U32),
        compiler_params=_cparams(("arbitrary",)),
        name="moe_experts",
    )(block_e, n_used, x_pad, wu, bu, wd, bd)


def _combine_kernel(dest_ref, next_dest_ref, y_ref, gate_ref, h_ref, lg_ref, lb_ref, o32_ref, o16_ref,
                    buf_ref, sems, *, tm):
    i = pl.program_id(0)
    slot = i % 2

    def copy(src, t, k, sl):
        return pltpu.make_async_copy(_slab(y_ref, src), _slab(buf_ref.at[sl], k * tm + t), sems.at[sl])

    def start_tile(d_ref, sl):
        def body(g, c):
            toks = [g * DMA_GROUP + j for j in range(DMA_GROUP)]
            srcs = [[d_ref[t * TOP_K + k] for k in range(TOP_K)] for t in toks]
            for t, row in zip(toks, srcs):
                for k, src in enumerate(row):
                    copy(src, t, k, sl).start()
            return c
        lax.fori_loop(0, tm // DMA_GROUP, body, 0)

    @pl.when(i == 0)
    def _():
        start_tile(dest_ref, 0)

    @pl.when(i + 1 < pl.num_programs(0))
    def _():
        start_tile(next_dest_ref, 1 - slot)

    def wait(t, c):
        for k in range(TOP_K):
            copy(0, t, k, slot).wait()
        return c

    lax.fori_loop(0, tm, wait, 0, unroll=4)

    g = gate_ref[...]
    gk = [jnp.broadcast_to(g[:, k:k + 1], (tm, LANES)) for k in range(TOP_K)]
    lo_acc, hi_acc = None, None
    for k in range(TOP_K):
        los, his = _load_slabs(buf_ref.at[slot], k * tm * SLAB_ROWS, tm)
        los = [p * gk[k] for p in los]
        his = [p * gk[k] for p in his]
        lo_acc = los if lo_acc is None else [a + p for a, p in zip(lo_acc, los)]
        hi_acc = his if hi_acc is None else [a + p for a, p in zip(hi_acc, his)]
    ffn = jnp.concatenate(lo_acc + hi_acc, axis=-1)
    y = _layer_norm_rows(DEEPNORM_ALPHA * h_ref[...] + ffn, lg_ref[...], lb_ref[...])
    o32_ref[...] = y
    o16_ref[...] = y.astype(BF16)


def _combine_ln(out_pad, dest, gate_t, h, ln_g, ln_b, tm=256):
    t, d = h.shape
    n = t // tm
    row = pl.BlockSpec((tm, d), lambda i: (i, 0))
    vec = pl.BlockSpec((1, d), lambda i: (0, 0))
    return pl.pallas_call(
        functools.partial(_combine_kernel, tm=tm),
        grid=(n,),
        in_specs=[pl.BlockSpec((TOP_K * tm,), lambda i: (i,), memory_space=pltpu.SMEM),
                  pl.BlockSpec((TOP_K * tm,), lambda i: (jnp.minimum(i + 1, n - 1),), memory_space=pltpu.SMEM),
                  pl.BlockSpec(memory_space=pl.ANY),
                  pl.BlockSpec((tm, TOP_K), lambda i: (i, 0)),
                  row, vec, vec],
        out_specs=[row, row],
        out_shape=[jax.ShapeDtypeStruct((t, d), F32), jax.ShapeDtypeStruct((t, d), BF16)],
        scratch_shapes=[pltpu.VMEM((2, TOP_K * tm * SLAB_ROWS, LANES), U32), pltpu.SemaphoreType.DMA((2,))],
        compiler_params=_cparams(("arbitrary",)),
        name="moe_combine_ln2",
    )(dest, dest, out_pad, gate_t, h, ln_g.reshape(1, d), ln_b.reshape(1, d))


def _moe_layout(idx, rank, counts, n_blocks):
    counts = counts.reshape(N_EXPERTS)
    padded = (counts + MOE_BLOCK - 1) // MOE_BLOCK * MOE_BLOCK
    pad_end = jnp.cumsum(padded)
    pad_start = pad_end - padded
    experts = jnp.arange(N_EXPERTS, dtype=I32)[:, None, None]
    dest = rank + jnp.sum(jnp.where(idx[None] == experts, pad_start[:, None, None], 0), axis=0)
    block_rows = jnp.arange(n_blocks, dtype=I32) * MOE_BLOCK
    block_e = jnp.minimum(jnp.sum(block_rows[:, None] >= pad_end[None, :], axis=1), N_EXPERTS - 1).astype(I32)
    n_used = (pad_end[-1:] // MOE_BLOCK).astype(I32)
    pad_at = (pad_start + counts).astype(I32)
    pad_len = (padded - counts).astype(I32)
    return dest.astype(I32).T.reshape(-1), block_e, n_used, pad_at, pad_len


def _permute_w_in(w):
    pieces = [w[:, QM_END:], w[:, :Q_END], w[:, V_END:U_END], w[:, U_END:QM_END], w[:, Q_END:K_END], w[:, K_END:V_END]]
    return jnp.concatenate(pieces, axis=1).astype(BF16)


def kernel(x, mem, ln_in_g, ln_in_b, w_in, attn_sinks, ssm_lambda_re, ssm_lambda_im, ssm_log_dt, ssm_b_re, ssm_b_im, ssm_c_re, ssm_c_im, ssm_d, w_glu, w_mem_kv, w_branch, w_out, ln1_g, ln1_b, w_router, b_router, w_up, b_up, w_down, b_down, ln2_g, ln2_b):
    b, s, d = x.shape
    t = b * s
    n_blocks = -(-(t * TOP_K) // MOE_BLOCK) + N_EXPERTS
    mem16 = mem.reshape(b * N_MEM, d).astype(BF16)

    h32, h16 = _layer_norm_in(x.reshape(t, d), ln_in_g, ln_in_b)
    for l in range(DEPTH):
        proj = _matmul(h16, _permute_w_in(w_in[l]), tm=1024, tn=768, name="in_proj")
        mem_kv = _matmul(mem16, w_mem_kv[l].astype(BF16), tm=1024, tn=512, name="mem_kv")

        attn = _sliding_window_attention(proj, attn_sinks[l].astype(F32), b, s)
        mem_out = _memory_attention(proj, mem_kv, b, s)
        ssm_w = _ssm_weights(ssm_lambda_re[l], ssm_lambda_im[l], ssm_log_dt[l], ssm_b_re[l], ssm_b_im[l],
                             ssm_c_re[l], ssm_c_im[l], ssm_d[l])
        u_tm = proj[:, U_OFF:U_OFF + SSM_WIDTH].reshape(b, s, SSM_WIDTH).transpose(1, 0, 2).reshape(t, SSM_WIDTH)
        ssm_tm = _ssm_branch(u_tm, ssm_w, w_glu[l].astype(BF16), b, s)
        ssm_out = ssm_tm.reshape(s, b, SSM_WIDTH).transpose(1, 0, 2).reshape(t, SSM_WIDTH)

        h1, h1_slabs = _merge_out_ln(attn, ssm_out, mem_out, proj, h32, w_branch[l].astype(BF16),
                                     w_out[l].astype(BF16), ln1_g[l], ln1_b[l])

        idx, gate, rank, counts = _router(h1, w_router[l], b_router[l])
        dest, block_e, n_used, pad_at, pad_len = _moe_layout(idx, rank, counts, n_blocks)
        x_pad = _dispatch(h1_slabs, dest, pad_at, pad_len, n_blocks * MOE_BLOCK)
        out_pad = _experts(x_pad, block_e, n_used, _pair_split_up(w_up, l), _pair_split_bias(b_up[l]),
                           w_down[l].astype(BF16), b_down[l][:, None, :].astype(F32))
        h32, h16 = _combine_ln(out_pad, dest, gate.T, h1, ln2_g[l], ln2_b[l])
    return h32.reshape(b, s, d)
```

```python
import functools
import math

import jax
import jax.numpy as jnp
from jax import lax
from jax.experimental import pallas as pl
from jax.experimental.pallas import tpu as pltpu

F32 = jnp.float32
BF16 = jnp.bfloat16
I32 = jnp.int32

D_MODEL = 2048
DEPTH = 2
N_HEADS = 16
N_KV_HEADS = 2
HEAD_DIM = 64
WINDOW = 128
BLOCK = 128
ATTN_WIDTH = N_HEADS * HEAD_DIM
KV_WIDTH = N_KV_HEADS * HEAD_DIM
SSM_WIDTH = D_MODEL // 4
SSM_GROUP = 16
N_SSM_GROUPS = SSM_WIDTH // SSM_GROUP
SSM_STATE = 64
N_MEM = 256
MEM_HEADS = 4
MEM_HEAD_DIM = 128
MEM_WIDTH = MEM_HEADS * MEM_HEAD_DIM
N_BRANCHES = 3
Q_END = ATTN_WIDTH
K_END = Q_END + KV_WIDTH
V_END = K_END + KV_WIDTH
U_END = V_END + SSM_WIDTH
QM_END = U_END + MEM_WIDTH
IN_WIDTH = QM_END + N_BRANCHES * D_MODEL
N_EXPERTS = 32
TOP_K = 4
D_FF = D_MODEL // 2
MOE_BLOCK = 512
SWIGLU_ALPHA = 1.702
SWIGLU_LIMIT = 7.0
LN_EPS = 1e-5
DEEPNORM_ALPHA = (2.0 * DEPTH) ** 0.25

G_OFF = 0
Q_OFF = N_BRANCHES * D_MODEL
U_OFF = Q_OFF + ATTN_WIDTH
QM_OFF = U_OFF + SSM_WIDTH
K_OFF = QM_OFF + MEM_WIDTH
V_OFF = K_OFF + KV_WIDTH

SSM_COMPLEX = N_SSM_GROUPS * SSM_STATE
SSM_HALVES = 2
SSM_SCAN_STRIP = 512

VMEM_LIMIT = 56 * 1024 * 1024


def _cparams(sem):
    return pltpu.CompilerParams(dimension_semantics=sem, vmem_limit_bytes=VMEM_LIMIT)


def _layer_norm_rows(x, g, b):
    mu = jnp.mean(x, axis=-1, keepdims=True)
    xc = x - mu
    var = jnp.mean(xc * xc, axis=-1, keepdims=True)
    return xc * lax.rsqrt(var + LN_EPS) * g + b


LANES = 128
SLAB_ROWS = D_MODEL // 2 // LANES
DMA_GROUP = 4
DMA_PRIORITIES = 2
U32 = jnp.uint32
HIGH_HALF = 0xFFFF0000


def _pack_pair(lo, hi):
    lo = lax.bitcast_convert_type(lo.astype(BF16).astype(F32), U32) >> 16
    hi = lax.bitcast_convert_type(hi.astype(BF16).astype(F32), U32) & U32(HIGH_HALF)
    return lo | hi


def _unpack_pair(w):
    return lax.bitcast_convert_type(w << 16, F32), lax.bitcast_convert_type(w & U32(HIGH_HALF), F32)


def _store_slabs(slab_ref, x, n, first_slab=0):
    half = D_MODEL // 2
    for c in range(SLAB_ROWS):
        lo = x[:, c * LANES:(c + 1) * LANES]
        hi = x[:, half + c * LANES:half + (c + 1) * LANES]
        slab_ref[pl.ds(first_slab * SLAB_ROWS + c, n, stride=SLAB_ROWS), :] = _pack_pair(lo, hi)


def _load_slabs(slab_ref, first_row, n):
    los, his = [], []
    for c in range(SLAB_ROWS):
        lo, hi = _unpack_pair(slab_ref[pl.ds(first_row + c, n, stride=SLAB_ROWS), :])
        los.append(lo)
        his.append(hi)
    return los, his


def _ln_kernel(x_ref, g_ref, b_ref, o32_ref, o16_ref):
    y = _layer_norm_rows(x_ref[...], g_ref[...], b_ref[...])
    o32_ref[...] = y
    o16_ref[...] = y.astype(BF16)


def _layer_norm_in(x, g, b, tm=512):
    t, d = x.shape
    row = pl.BlockSpec((tm, d), lambda i: (i, 0))
    vec = pl.BlockSpec((1, d), lambda i: (0, 0))
    return pl.pallas_call(
        _ln_kernel,
        grid=(t // tm,),
        in_specs=[row, vec, vec],
        out_specs=[row, row],
        out_shape=[jax.ShapeDtypeStruct((t, d), F32), jax.ShapeDtypeStruct((t, d), BF16)],
        compiler_params=_cparams(("parallel",)),
        name="ln_in",
    )(x, g.reshape(1, d), b.reshape(1, d))


def _mm_kernel(a_ref, w_ref, o_ref):
    o_ref[...] = jnp.dot(a_ref[...], w_ref[...], preferred_element_type=F32).astype(o_ref.dtype)


def _matmul(a, w, tm, tn, name):
    m, k = a.shape
    n = w.shape[1]
    return pl.pallas_call(
        _mm_kernel,
        grid=(m // tm, n // tn),
        in_specs=[pl.BlockSpec((tm, k), lambda i, j: (i, 0)), pl.BlockSpec((k, tn), lambda i, j: (0, j))],
        out_specs=pl.BlockSpec((tm, tn), lambda i, j: (i, j)),
        out_shape=jax.ShapeDtypeStruct((m, n), BF16),
        compiler_params=_cparams(("parallel", "parallel")),
        name=name,
    )(a, w)


def _swa_kernel(sink_ref, q_ref, kc_ref, vc_ref, kp_ref, vp_ref, o_ref, *, nblk):
    i = pl.program_id(1)
    grp = N_HEADS // N_KV_HEADS
    pairs = grp // 2
    pw = 2 * HEAD_DIM
    nrow = pairs * BLOCK
    nkey = 2 * BLOCK
    log2e = math.log2(math.e)
    scale = HEAD_DIM ** -0.5 * log2e
    row = lax.broadcasted_iota(I32, (nrow, nkey), 0)
    kj = lax.broadcasted_iota(I32, (nrow, nkey), 1)
    rel = (row & (BLOCK - 1)) + BLOCK - kj
    band_ok = (rel >= 0) & (rel < WINDOW)
    row_pair = lax.broadcasted_iota(I32, (nrow, 1), 0) // BLOCK
    low_lanes = lax.broadcasted_iota(I32, (nkey, pw), 1) < HEAD_DIM
    first_lo = jnp.where(i > 0, 0, BLOCK)

    def block_diag(band, g):
        band = band.astype(F32)
        swapped = pltpu.roll(band, HEAD_DIM, axis=1)
        top = jnp.where(low_lanes, band if g == 0 else swapped, 0.0)
        bottom = jnp.where(low_lanes, 0.0, swapped if g == 0 else band)
        return jnp.concatenate([top, bottom], axis=0).astype(BF16)

    for j in range(nblk):
        rows = slice(j * BLOCK, (j + 1) * BLOCK)
        if j == 0:
            k_prev, v_prev = kp_ref[...], vp_ref[...]
            valid = band_ok & (kj >= first_lo)
        else:
            prev = slice((j - 1) * BLOCK, j * BLOCK)
            k_prev, v_prev = kc_ref[prev, :], vc_ref[prev, :]
            valid = band_ok
        k_band = jnp.concatenate([k_prev, kc_ref[rows, :]], axis=0)
        v_band = jnp.concatenate([v_prev, vc_ref[rows, :]], axis=0)
        for g in range(N_KV_HEADS):
            k2 = block_diag(k_band, g)
            v2 = block_diag(v_band, g)
            cols = [slice((g * pairs + p) * pw, (g * pairs + p + 1) * pw) for p in range(pairs)]
            q2 = jnp.concatenate([q_ref[rows, c] for c in cols], axis=0)
            s = lax.dot_general(q2, k2, (((1,), (1,)), ((), ())), preferred_element_type=F32) * scale
            probs = []
            for half in range(2):
                sh = jnp.where(valid, s[:, half * nkey:(half + 1) * nkey], -jnp.inf)
                sink = sink_ref[g * grp + half] * log2e
                for p in range(1, pairs):
                    sink = jnp.where(row_pair == p, sink_ref[g * grp + 2 * p + half] * log2e, sink)
                m = jnp.maximum(jnp.max(sh, axis=-1, keepdims=True), sink)
                e = jnp.exp2(sh - m)
                den = jnp.sum(e, axis=-1, keepdims=True) + jnp.exp2(sink - m)
                probs.append((e * (1.0 / den)).astype(BF16))
            out = jnp.dot(jnp.concatenate(probs, axis=-1), v2, preferred_element_type=F32)
            for p, c in enumerate(cols):
                o_ref[rows, c] = out[p * BLOCK:(p + 1) * BLOCK, :].astype(o_ref.dtype)


def _sliding_window_attention(proj, sinks, b, s, tq=512):
    t = b * s
    nq = s // tq
    nblk = tq // BLOCK
    sb = s // BLOCK
    cur = lambda col: (lambda bi, i: (bi * nq + i, col))
    prev = lambda col: (lambda bi, i: (bi * sb + jnp.maximum(i * nblk - 1, 0), col))
    return pl.pallas_call(
        functools.partial(_swa_kernel, nblk=nblk),
        grid=(b, nq),
        in_specs=[
            pl.BlockSpec(memory_space=pltpu.SMEM),
            pl.BlockSpec((tq, ATTN_WIDTH), cur(Q_OFF // ATTN_WIDTH)),
            pl.BlockSpec((tq, KV_WIDTH), cur(K_OFF // KV_WIDTH)),
            pl.BlockSpec((tq, KV_WIDTH), cur(V_OFF // KV_WIDTH)),
            pl.BlockSpec((BLOCK, KV_WIDTH), prev(K_OFF // KV_WIDTH)),
            pl.BlockSpec((BLOCK, KV_WIDTH), prev(V_OFF // KV_WIDTH)),
        ],
        out_specs=pl.BlockSpec((tq, ATTN_WIDTH), lambda bi, i: (bi * nq + i, 0)),
        out_shape=jax.ShapeDtypeStruct((t, ATTN_WIDTH), BF16),
        compiler_params=_cparams(("parallel", "parallel")),
        name="swa",
    )(sinks, proj, proj, proj, proj, proj)


def _mem_attn_kernel(q_ref, k_ref, v_ref, o_ref):
    s = lax.dot_general(q_ref[...], k_ref[...], (((1,), (1,)), ((), ())),
                        preferred_element_type=F32) * (MEM_HEAD_DIM ** -0.5)
    m = jnp.max(s, axis=-1, keepdims=True)
    e = jnp.exp(s - m)
    p = (e * (1.0 / jnp.sum(e, axis=-1, keepdims=True))).astype(BF16)
    o_ref[...] = jnp.dot(p, v_ref[...], preferred_element_type=F32).astype(o_ref.dtype)


def _memory_attention(proj, mem_kv, b, s):
    t = b * s
    qcol = QM_OFF // MEM_HEAD_DIM
    return pl.pallas_call(
        _mem_attn_kernel,
        grid=(b, MEM_HEADS),
        in_specs=[
            pl.BlockSpec((s, MEM_HEAD_DIM), lambda bi, h: (bi, qcol + h)),
            pl.BlockSpec((N_MEM, MEM_HEAD_DIM), lambda bi, h: (bi, h)),
            pl.BlockSpec((N_MEM, MEM_HEAD_DIM), lambda bi, h: (bi, MEM_HEADS + h)),
        ],
        out_specs=pl.BlockSpec((s, MEM_HEAD_DIM), lambda bi, h: (bi, h)),
        out_shape=jax.ShapeDtypeStruct((t, MEM_WIDTH), BF16),
        compiler_params=_cparams(("parallel", "parallel")),
        name="mem_attn",
    )(proj, mem_kv, mem_kv)


def _gelu_tanh(x):
    return 0.5 * x * (1.0 + jnp.tanh(math.sqrt(2.0 / math.pi) * (x + 0.044715 * (x * x * x))))


def _ssm_kernel(u_ref, wbr_ref, wbi_ref, lr_ref, li_ref, wcr_ref, wci_ref, d_ref, wglu_ref, o_ref,
                xr_ref, xi_ref, sr_ref, si_ref, *, nb, tc):
    @pl.when(pl.program_id(0) == 0)
    def _():
        sr_ref[...] = jnp.zeros_like(sr_ref)
        si_ref[...] = jnp.zeros_like(si_ref)

    u = u_ref[...]
    uw = SSM_WIDTH // SSM_HALVES
    xw = SSM_COMPLEX // SSM_HALVES
    for hf in range(SSM_HALVES):
        uh = u[:, hf * uw:(hf + 1) * uw]
        xr_ref[:, hf * xw:(hf + 1) * xw] = jnp.dot(uh, wbr_ref[hf], preferred_element_type=F32)
        xi_ref[:, hf * xw:(hf + 1) * xw] = jnp.dot(uh, wbi_ref[hf], preferred_element_type=F32)

    for st in range(SSM_COMPLEX // SSM_SCAN_STRIP):
        cols = slice(st * SSM_SCAN_STRIP, (st + 1) * SSM_SCAN_STRIP)
        lr = jnp.broadcast_to(lr_ref[:, cols], (nb, SSM_SCAN_STRIP))
        li = jnp.broadcast_to(li_ref[:, cols], (nb, SSM_SCAN_STRIP))

        def step(t, carry):
            sr, si = carry
            r0 = pl.multiple_of(t * nb, nb)
            nr = lr * sr - li * si + xr_ref[pl.ds(r0, nb), cols]
            ni = lr * si + li * sr + xi_ref[pl.ds(r0, nb), cols]
            xr_ref[pl.ds(r0, nb), cols] = nr
            xi_ref[pl.ds(r0, nb), cols] = ni
            return nr, ni

        sr, si = lax.fori_loop(0, tc, step, (sr_ref[:, cols], si_ref[:, cols]), unroll=8)
        sr_ref[:, cols] = sr
        si_ref[:, cols] = si

    ys = []
    for hf in range(SSM_HALVES):
        xs = slice(hf * xw, (hf + 1) * xw)
        yr = jnp.dot(xr_ref[:, xs].astype(BF16), wcr_ref[hf], preferred_element_type=F32)
        yi = jnp.dot(xi_ref[:, xs].astype(BF16), wci_ref[hf], preferred_element_type=F32)
        ys.append(yr - yi)
    y = jnp.concatenate(ys, axis=-1) + d_ref[...] * u.astype(F32)
    zg = jnp.dot(_gelu_tanh(y).astype(BF16), wglu_ref[...], preferred_element_type=F32)
    o_ref[...] = (zg[:, :SSM_WIDTH] * jax.nn.sigmoid(zg[:, SSM_WIDTH:])).astype(o_ref.dtype)


def _ssm_weights(lambda_re, lambda_im, log_dt, b_re, b_im, c_re, c_im, d_skip):
    g, p, h = N_SSM_GROUPS, SSM_STATE, SSM_GROUP
    gh = g // SSM_HALVES
    lam = lax.complex(lambda_re.astype(F32), lambda_im.astype(F32))
    dt = jnp.exp(log_dt.astype(F32))[:, None]
    lam_bar = jnp.exp(lam * dt)
    b_bar = ((lam_bar - 1.0) / lam)[..., None] * lax.complex(b_re.astype(F32), b_im.astype(F32))
    eye = jnp.eye(gh, dtype=F32)

    def blockdiag_in(m):
        m = m.reshape(SSM_HALVES, gh, p, h)
        return jnp.einsum('xgph,gk->xghkp', m, eye).reshape(SSM_HALVES, gh * h, gh * p).astype(BF16)

    def blockdiag_out(m):
        m = m.reshape(SSM_HALVES, gh, h, p)
        return jnp.einsum('xghp,gk->xgpkh', m, eye).reshape(SSM_HALVES, gh * p, gh * h).astype(BF16)

    return (blockdiag_in(jnp.real(b_bar)), blockdiag_in(jnp.imag(b_bar)),
            jnp.real(lam_bar).reshape(1, g * p), jnp.imag(lam_bar).reshape(1, g * p),
            blockdiag_out(c_re.astype(F32)), blockdiag_out(c_im.astype(F32)),
            d_skip.astype(F32).reshape(1, g * h))


def _ssm_branch(u_tm, ssm_w, w_glu, nb, s, tc=64):
    wbr, wbi, lr, li, wcr, wci, d = ssm_w
    rows = tc * nb
    full = lambda a: pl.BlockSpec(a.shape, lambda i: (0,) * a.ndim)
    return pl.pallas_call(
        functools.partial(_ssm_kernel, nb=nb, tc=tc),
        grid=(s // tc,),
        in_specs=[pl.BlockSpec((rows, SSM_WIDTH), lambda i: (i, 0)),
                  full(wbr), full(wbi), full(lr), full(li), full(wcr), full(wci), full(d), full(w_glu)],
        out_specs=pl.BlockSpec((rows, SSM_WIDTH), lambda i: (i, 0)),
        out_shape=jax.ShapeDtypeStruct((s * nb, SSM_WIDTH), BF16),
        scratch_shapes=[pltpu.VMEM((rows, SSM_COMPLEX), F32), pltpu.VMEM((rows, SSM_COMPLEX), F32),
                        pltpu.VMEM((nb, SSM_COMPLEX), F32), pltpu.VMEM((nb, SSM_COMPLEX), F32)],
        compiler_params=_cparams(("arbitrary",)),
        name="ssm",
    )(u_tm, wbr, wbi, lr, li, wcr, wci, d, w_glu)


def _merge_kernel(attn_ref, ssm_ref, mem_ref, g0_ref, g1_ref, g2_ref, h_ref, wb_ref, wo_ref, lg_ref, lb_ref,
                  o32_ref, slab_ref, *, tm, sub):
    for r in range(tm // sub):
        rows = slice(r * sub, (r + 1) * sub)

        def gated(x_ref, g_ref, lo, hi):
            br = jnp.dot(x_ref[rows, :], wb_ref[lo:hi, :], preferred_element_type=F32)
            return jax.nn.sigmoid(g_ref[rows, :].astype(F32)) * br

        merged = gated(attn_ref, g0_ref, 0, ATTN_WIDTH)
        merged += gated(ssm_ref, g1_ref, ATTN_WIDTH, ATTN_WIDTH + SSM_WIDTH)
        merged += gated(mem_ref, g2_ref, ATTN_WIDTH + SSM_WIDTH, ATTN_WIDTH + SSM_WIDTH + MEM_WIDTH)
        mix = jnp.dot(merged.astype(BF16), wo_ref[...], preferred_element_type=F32)
        y = _layer_norm_rows(DEEPNORM_ALPHA * h_ref[rows, :] + mix, lg_ref[...], lb_ref[...])
        o32_ref[rows, :] = y
        _store_slabs(slab_ref, y, sub, first_slab=r * sub)


def _merge_out_ln(attn, ssm, mem, proj, h, w_branch, w_out, ln_g, ln_b, tm=256, sub=256):
    t, d = h.shape
    row = lambda w, col=0: pl.BlockSpec((tm, w), lambda i: (i, col))
    const = lambda shape: pl.BlockSpec(shape, lambda i: (0, 0))
    return pl.pallas_call(
        functools.partial(_merge_kernel, tm=tm, sub=sub),
        grid=(t // tm,),
        in_specs=[row(ATTN_WIDTH), row(SSM_WIDTH), row(MEM_WIDTH),
                  row(d, G_OFF // d), row(d, G_OFF // d + 1), row(d, G_OFF // d + 2),
                  row(d), const(w_branch.shape), const(w_out.shape), const((1, d)), const((1, d))],
        out_specs=[row(d), pl.BlockSpec((tm * SLAB_ROWS, LANES), lambda i: (i, 0))],
        out_shape=[jax.ShapeDtypeStruct((t, d), F32), jax.ShapeDtypeStruct((t * SLAB_ROWS, LANES), U32)],
        compiler_params=_cparams(("parallel",)),
        name="merge_out_ln1",
    )(attn, ssm, mem, proj, proj, proj, h, w_branch, w_out, ln_g.reshape(1, d), ln_b.reshape(1, d))


def _split_bf16(x):
    hi = x.astype(BF16)
    return hi, (x - hi.astype(F32)).astype(BF16)


def _router_kernel(h_ref, wt_ref, b_ref, idx_ref, gate_ref, rank_ref, cnt_ref, carry_ref, *, tm):
    @pl.when(pl.program_id(0) == 0)
    def _():
        carry_ref[...] = jnp.zeros_like(carry_ref)

    nt = (((1,), (1,)), ((), ()))
    h_hi, h_lo = _split_bf16(h_ref[...])
    w_hi, w_lo = _split_bf16(wt_ref[...])
    lg = (lax.dot_general(w_hi, h_hi, nt, preferred_element_type=F32)
          + lax.dot_general(w_hi, h_lo, nt, preferred_element_type=F32)
          + lax.dot_general(w_lo, h_hi, nt, preferred_element_type=F32)) + b_ref[...]

    e_iota = lax.broadcasted_iota(I32, (N_EXPERTS, tm), 0)
    chosen = jnp.zeros((N_EXPERTS, tm), F32)
    vals, sels = [], []
    for k in range(TOP_K):
        m = jnp.max(lg, axis=0, keepdims=True)
        idx = jnp.min(jnp.where(lg == m, e_iota, N_EXPERTS), axis=0, keepdims=True)
        sel = e_iota == idx
        idx_ref[k:k + 1, :] = idx
        vals.append(m)
        sels.append(sel)
        chosen = jnp.where(sel, 1.0, chosen)
        lg = jnp.where(sel, -jnp.inf, lg)

    ex = [jnp.exp(v - vals[0]) for v in vals]
    inv = 1.0 / (ex[0] + ex[1] + ex[2] + ex[3])
    for k in range(TOP_K):
        gate_ref[k:k + 1, :] = ex[k] * inv

    r = lax.broadcasted_iota(I32, (tm, tm), 0)
    c = lax.broadcasted_iota(I32, (tm, tm), 1)
    before = jnp.where(r < c, 1.0, 0.0).astype(BF16)
    earlier = jnp.dot(chosen.astype(BF16), before, preferred_element_type=F32) + carry_ref[...]
    for k in range(TOP_K):
        rank_ref[k:k + 1, :] = jnp.sum(jnp.where(sels[k], earlier, 0.0), axis=0, keepdims=True).astype(I32)
    carry_ref[...] += jnp.sum(chosen, axis=1, keepdims=True)
    cnt_ref[...] = carry_ref[...].astype(I32)


def _router(h, w_router, b_router, tm=512):
    t, d = h.shape
    out = lambda dt: jax.ShapeDtypeStruct((TOP_K, t), dt)
    tok = pl.BlockSpec((TOP_K, tm), lambda i: (0, i))
    return pl.pallas_call(
        functools.partial(_router_kernel, tm=tm),
        grid=(t // tm,),
        in_specs=[pl.BlockSpec((tm, d), lambda i: (i, 0)),
                  pl.BlockSpec((N_EXPERTS, d), lambda i: (0, 0)),
                  pl.BlockSpec((N_EXPERTS, 1), lambda i: (0, 0))],
        out_specs=[tok, tok, tok, pl.BlockSpec((N_EXPERTS, 1), lambda i: (0, 0))],
        out_shape=[out(I32), out(F32), out(I32), jax.ShapeDtypeStruct((N_EXPERTS, 1), I32)],
        scratch_shapes=[pltpu.VMEM((N_EXPERTS, 1), F32)],
        compiler_params=_cparams(("arbitrary",)),
        name="router",
    )(h, w_router.T, b_router.reshape(N_EXPERTS, 1))


def _slab(ref, index):
    return ref.at[pl.ds(pl.multiple_of(index * SLAB_ROWS, SLAB_ROWS), SLAB_ROWS)]


ZERO_RUNS = tuple(1 << s for s in reversed(range(MOE_BLOCK.bit_length() - 1)))


def _dispatch_kernel(dest_ref, pad_at_ref, pad_len_ref, src_ref, x_ref, zero_ref, sem, zero_sem, *, tm):
    def zero_fill(act):
        def per_expert(e, c):
            at, n = pad_at_ref[e], pad_len_ref[e]
            for run in ZERO_RUNS:
                take = n & run

                @pl.when(take != 0)
                def _(at=at, run=run):
                    rows = run * SLAB_ROWS
                    dst = x_ref.at[pl.ds(pl.multiple_of(at * SLAB_ROWS, SLAB_ROWS), rows)]
                    act(pltpu.make_async_copy(zero_ref.at[pl.ds(0, rows)], dst, zero_sem))
                at = at + take
            return c
        lax.fori_loop(0, N_EXPERTS, per_expert, 0)

    @pl.when(pl.program_id(0) == 0)
    def _():
        zero_ref[...] = jnp.zeros_like(zero_ref)
        zero_fill(lambda cp: cp.start())
        zero_fill(lambda cp: cp.wait())

    def copy(t, dst):
        return pltpu.make_async_copy(_slab(src_ref, t), _slab(x_ref, dst), sem)

    def start(g, c):
        toks = [g * DMA_GROUP + j for j in range(DMA_GROUP)]
        dsts = [[dest_ref[t * TOP_K + k] for k in range(TOP_K)] for t in toks]
        for t, row in zip(toks, dsts):
            for k, dst in enumerate(row):
                copy(t, dst).start(priority=k % DMA_PRIORITIES)
        return c

    def wait(t, c):
        for k in range(TOP_K):
            copy(t, 0).wait()
        return c

    lax.fori_loop(0, tm // DMA_GROUP, start, 0)
    lax.fori_loop(0, tm, wait, 0, unroll=4)


def _dispatch(h_slabs, dest, pad_at, pad_len, n_rows, tm=1024):
    t = dest.shape[0] // TOP_K
    smem = pl.BlockSpec(memory_space=pltpu.SMEM)
    return pl.pallas_call(
        functools.partial(_dispatch_kernel, tm=tm),
        grid=(t // tm,),
        in_specs=[pl.BlockSpec((TOP_K * tm,), lambda i: (i,), memory_space=pltpu.SMEM), smem, smem,
                  pl.BlockSpec((tm * SLAB_ROWS, LANES), lambda i: (i, 0))],
        out_specs=pl.BlockSpec(memory_space=pl.ANY),
        out_shape=jax.ShapeDtypeStruct((n_rows * SLAB_ROWS, LANES), U32),
        scratch_shapes=[pltpu.VMEM((ZERO_RUNS[0] * SLAB_ROWS, LANES), U32),
                        pltpu.SemaphoreType.DMA(()), pltpu.SemaphoreType.DMA(())],
        compiler_params=_cparams(("arbitrary",)),
        name="moe_dispatch",
    )(dest, pad_at, pad_len, h_slabs)


UP_CHUNK = 256
UP_HALF = UP_CHUNK // 2


def _pair_split_kernel(w_ref, o_ref):
    r = lax.broadcasted_iota(I32, (UP_CHUNK, UP_CHUNK), 0)
    c = lax.broadcasted_iota(I32, (UP_CHUNK, UP_CHUNK), 1)
    src = jnp.where(c < UP_HALF, 2 * c, 2 * (c - UP_HALF) + 1)
    perm = jnp.where(r == src, 1.0, 0.0).astype(BF16)
    for ch in range(w_ref.shape[-1] // UP_CHUNK):
        cols = slice(ch * UP_CHUNK, (ch + 1) * UP_CHUNK)
        o_ref[0, :, cols] = jnp.dot(w_ref[0, :, cols].astype(BF16), perm, preferred_element_type=F32).astype(BF16)


def _pair_split_up(w_up, layer, tn=1024):
    _, e, d, n = w_up.shape
    return pl.pallas_call(
        _pair_split_kernel,
        grid=(e, n // tn),
        in_specs=[pl.BlockSpec((None, 1, d, tn), lambda i, j: (layer, i, 0, j))],
        out_specs=pl.BlockSpec((1, d, tn), lambda i, j: (i, 0, j)),
        out_shape=jax.ShapeDtypeStruct((e, d, n), BF16),
        compiler_params=_cparams(("parallel", "parallel")),
        name="moe_up_pair_split",
    )(w_up)


def _pair_split_bias(b_up):
    e, n = b_up.shape
    return b_up.reshape(e, n // UP_CHUNK, UP_HALF, 2).transpose(0, 1, 3, 2).reshape(e, 1, n).astype(F32)


def _expert_kernel(be_ref, nu_ref, x_ref, wu_ref, bu_ref, wd_ref, bd_ref, o_ref):
    del be_ref
    i = pl.program_id(0)

    @pl.when(i < nu_ref[0])
    def _():
        los, his = _load_slabs(x_ref, 0, MOE_BLOCK)
        x = jnp.concatenate([p.astype(BF16) for p in los + his], axis=-1)
        up = jnp.dot(x, wu_ref[0], preferred_element_type=F32) + bu_ref[0]
        acts = []
        for ch in range(up.shape[-1] // UP_CHUNK):
            x_glu = jnp.minimum(up[:, ch * UP_CHUNK:ch * UP_CHUNK + UP_HALF], SWIGLU_LIMIT)
            x_lin = jnp.clip(up[:, ch * UP_CHUNK + UP_HALF:(ch + 1) * UP_CHUNK], -SWIGLU_LIMIT, SWIGLU_LIMIT)
            acts.append((x_glu * jax.nn.sigmoid(SWIGLU_ALPHA * x_glu) * (x_lin + 1.0)).astype(BF16))
        act = jnp.concatenate(acts, axis=-1)
        out = jnp.dot(act, wd_ref[0], preferred_element_type=F32) + bd_ref[0]
        _store_slabs(o_ref, out, MOE_BLOCK)

    @pl.when(i >= nu_ref[0])
    def _():
        o_ref[...] = jnp.zeros_like(o_ref)


def _experts(x_pad, block_e, n_used, wu, bu, wd, bd):
    n_blocks = x_pad.shape[0] // (MOE_BLOCK * SLAB_ROWS)
    row = pl.BlockSpec((MOE_BLOCK * SLAB_ROWS, LANES), lambda i, be, nu: (i, 0))
    used_row = pl.BlockSpec((MOE_BLOCK * SLAB_ROWS, LANES), lambda i, be, nu: (jnp.minimum(i, nu[0] - 1), 0))
    per_e = lambda a: pl.BlockSpec((1,) + a.shape[1:], lambda i, be, nu: (be[i], 0, 0))
    return pl.pallas_call(
        _expert_kernel,
        grid_spec=pltpu.PrefetchScalarGridSpec(
            num_scalar_prefetch=2,
            grid=(n_blocks,),
            in_specs=[used_row, per_e(wu), per_e(bu), per_e(wd), per_e(bd)],
            out_specs=row,
        ),
        out_shape=jax.ShapeDtypeStruct(x_pad.shape, U32),
        compiler_params=_cparams(("arbitrary",)),
        name="moe_experts",
    )(block_e, n_used, x_pad, wu, bu, wd, bd)


def _combine_kernel(dest_ref, next_dest_ref, y_ref, gate_ref, h_ref, lg_ref, lb_ref, o32_ref, o16_ref,
                    buf0_ref, buf1_ref, sems, *, tm, n_tiles):
    i = pl.program_id(0)
    bufs = (buf0_ref, buf1_ref)

    def copy(src, t, k, sl):
        return pltpu.make_async_copy(_slab(y_ref, src), _slab(bufs[sl], k * tm + t), sems.at[sl])

    def start_group(d_ref, g, sl):
        toks = [g * DMA_GROUP + j for j in range(DMA_GROUP)]
        srcs = [[d_ref[t * TOP_K + k] for k in range(TOP_K)] for t in toks]
        for t, row in zip(toks, srcs):
            for k, src in enumerate(row):
                copy(src, t, k, sl).start(priority=k % DMA_PRIORITIES)

    def wait_tile(sl):
        def wait(t, c):
            for k in range(TOP_K):
                copy(0, t, k, sl).wait()
            return c
        lax.fori_loop(0, tm, wait, 0, unroll=4)

    def combine(buf_ref):
        g = gate_ref[...]
        gk = [jnp.broadcast_to(g[:, k:k + 1], (tm, LANES)) for k in range(TOP_K)]
        lo_acc, hi_acc = None, None
        for k in range(TOP_K):
            los, his = _load_slabs(buf_ref, k * tm * SLAB_ROWS, tm)
            los = [p * gk[k] for p in los]
            his = [p * gk[k] for p in his]
            lo_acc = los if lo_acc is None else [a + p for a, p in zip(lo_acc, los)]
            hi_acc = his if hi_acc is None else [a + p for a, p in zip(hi_acc, his)]
        ffn = jnp.concatenate(lo_acc + hi_acc, axis=-1)
        y = _layer_norm_rows(DEEPNORM_ALPHA * h_ref[...] + ffn, lg_ref[...], lb_ref[...])
        o32_ref[...] = y
        o16_ref[...] = y.astype(BF16)

    @pl.when(i == 0)
    def _():
        def first(g, c):
            start_group(dest_ref, g, 0)
            return c
        lax.fori_loop(0, tm // DMA_GROUP, first, 0)

    def step(sl):
        wait_tile(sl)
        for g in range(tm // DMA_GROUP):
            start_group(next_dest_ref, g, 1 - sl)
        combine(bufs[sl])

    for sl in range(2):
        pl.when(i % 2 == sl)(functools.partial(step, sl))

    @pl.when(i == n_tiles - 1)
    def _():
        wait_tile(1 - (n_tiles - 1) % 2)


def _combine_ln(out_pad, dest, gate_t, h, ln_g, ln_b, tm=256):
    t, d = h.shape
    n = t // tm
    row = pl.BlockSpec((tm, d), lambda i: (i, 0))
    vec = pl.BlockSpec((1, d), lambda i: (0, 0))
    return pl.pallas_call(
        functools.partial(_combine_kernel, tm=tm, n_tiles=n),
        grid=(n,),
        in_specs=[pl.BlockSpec((TOP_K * tm,), lambda i: (i,), memory_space=pltpu.SMEM),
                  pl.BlockSpec((TOP_K * tm,), lambda i: (jnp.minimum(i + 1, n - 1),), memory_space=pltpu.SMEM),
                  pl.BlockSpec(memory_space=pl.ANY),
                  pl.BlockSpec((tm, TOP_K), lambda i: (i, 0)),
                  row, vec, vec],
        out_specs=[row, row],
        out_shape=[jax.ShapeDtypeStruct((t, d), F32), jax.ShapeDtypeStruct((t, d), BF16)],
        scratch_shapes=[pltpu.VMEM((TOP_K * tm * SLAB_ROWS, LANES), U32),
                        pltpu.VMEM((TOP_K * tm * SLAB_ROWS, LANES), U32), pltpu.SemaphoreType.DMA((2,))],
        compiler_params=_cparams(("arbitrary",)),
        name="moe_combine_ln2",
    )(dest, dest, out_pad, gate_t, h, ln_g.reshape(1, d), ln_b.reshape(1, d))


def _moe_layout(idx, rank, counts, n_blocks):
    counts = counts.reshape(N_EXPERTS)
    padded = (counts + MOE_BLOCK - 1) // MOE_BLOCK * MOE_BLOCK
    pad_end = jnp.cumsum(padded)
    pad_start = pad_end - padded
    experts = jnp.arange(N_EXPERTS, dtype=I32)[:, None, None]
    dest = rank + jnp.sum(jnp.where(idx[None] == experts, pad_start[:, None, None], 0), axis=0)
    block_rows = jnp.arange(n_blocks, dtype=I32) * MOE_BLOCK
    block_e = jnp.minimum(jnp.sum(block_rows[:, None] >= pad_end[None, :], axis=1), N_EXPERTS - 1).astype(I32)
    n_used = (pad_end[-1:] // MOE_BLOCK).astype(I32)
    pad_at = (pad_start + counts).astype(I32)
    pad_len = (padded - counts).astype(I32)
    return dest.astype(I32).T.reshape(-1), block_e, n_used, pad_at, pad_len


def _permute_w_in(w):
    pieces = [w[:, QM_END:], w[:, :Q_END], w[:, V_END:U_END], w[:, U_END:QM_END], w[:, Q_END:K_END], w[:, K_END:V_END]]
    return jnp.concatenate(pieces, axis=1).astype(BF16)


def kernel(x, mem, ln_in_g, ln_in_b, w_in, attn_sinks, ssm_lambda_re, ssm_lambda_im, ssm_log_dt, ssm_b_re, ssm_b_im, ssm_c_re, ssm_c_im, ssm_d, w_glu, w_mem_kv, w_branch, w_out, ln1_g, ln1_b, w_router, b_router, w_up, b_up, w_down, b_down, ln2_g, ln2_b):
    b, s, d = x.shape
    t = b * s
    n_blocks = -(-(t * TOP_K) // MOE_BLOCK) + N_EXPERTS
    mem16 = mem.reshape(b * N_MEM, d).astype(BF16)

    h32, h16 = _layer_norm_in(x.reshape(t, d), ln_in_g, ln_in_b)
    for l in range(DEPTH):
        proj = _matmul(h16, _permute_w_in(w_in[l]), tm=1024, tn=768, name="in_proj")
        mem_kv = _matmul(mem16, w_mem_kv[l].astype(BF16), tm=1024, tn=512, name="mem_kv")

        attn = _sliding_window_attention(proj, attn_sinks[l].astype(F32), b, s)
        mem_out = _memory_attention(proj, mem_kv, b, s)
        ssm_w = _ssm_weights(ssm_lambda_re[l], ssm_lambda_im[l], ssm_log_dt[l], ssm_b_re[l], ssm_b_im[l],
                             ssm_c_re[l], ssm_c_im[l], ssm_d[l])
        u_tm = proj[:, U_OFF:U_OFF + SSM_WIDTH].reshape(b, s, SSM_WIDTH).transpose(1, 0, 2).reshape(t, SSM_WIDTH)
        ssm_tm = _ssm_branch(u_tm, ssm_w, w_glu[l].astype(BF16), b, s)
        ssm_out = ssm_tm.reshape(s, b, SSM_WIDTH).transpose(1, 0, 2).reshape(t, SSM_WIDTH)

        h1, h1_slabs = _merge_out_ln(attn, ssm_out, mem_out, proj, h32, w_branch[l].astype(BF16),
                                     w_out[l].astype(BF16), ln1_g[l], ln1_b[l])

        idx, gate, rank, counts = _router(h1, w_router[l], b_router[l])
        dest, block_e, n_used, pad_at, pad_len = _moe_layout(idx, rank, counts, n_blocks)
        x_pad = _dispatch(h1_slabs, dest, pad_at, pad_len, n_blocks * MOE_BLOCK)
        out_pad = _experts(x_pad, block_e, n_used, _pair_split_up(w_up, l), _pair_split_bias(b_up[l]),
                           w_down[l].astype(BF16), b_down[l][:, None, :].astype(F32))
        h32, h16 = _combine_ln(out_pad, dest, gate.T, h1, ln2_g[l], ln2_b[l])
    return h32.reshape(b, s, d)
```

```python
import functools
import math

import jax
import jax.numpy as jnp
from jax import lax
from jax.experimental import pallas as pl
from jax.experimental.pallas import tpu as pltpu

F32 = jnp.float32
BF16 = jnp.bfloat16
I32 = jnp.int32

D_MODEL = 2048
DEPTH = 2
N_HEADS = 16
N_KV_HEADS = 2
HEAD_DIM = 64
WINDOW = 128
BLOCK = 128
ATTN_WIDTH = N_HEADS * HEAD_DIM
KV_WIDTH = N_KV_HEADS * HEAD_DIM
SSM_WIDTH = D_MODEL // 4
SSM_GROUP = 16
N_SSM_GROUPS = SSM_WIDTH // SSM_GROUP
SSM_STATE = 64
N_MEM = 256
MEM_HEADS = 4
MEM_HEAD_DIM = 128
MEM_WIDTH = MEM_HEADS * MEM_HEAD_DIM
N_BRANCHES = 3
Q_END = ATTN_WIDTH
K_END = Q_END + KV_WIDTH
V_END = K_END + KV_WIDTH
U_END = V_END + SSM_WIDTH
QM_END = U_END + MEM_WIDTH
IN_WIDTH = QM_END + N_BRANCHES * D_MODEL
N_EXPERTS = 32
TOP_K = 4
D_FF = D_MODEL // 2
MOE_BLOCK = 512
SWIGLU_ALPHA = 1.702
SWIGLU_LIMIT = 7.0
LN_EPS = 1e-5
DEEPNORM_ALPHA = (2.0 * DEPTH) ** 0.25

G_OFF = 0
Q_OFF = N_BRANCHES * D_MODEL
U_OFF = Q_OFF + ATTN_WIDTH
QM_OFF = U_OFF + SSM_WIDTH
K_OFF = QM_OFF + MEM_WIDTH
V_OFF = K_OFF + KV_WIDTH

SSM_COMPLEX = N_SSM_GROUPS * SSM_STATE
SSM_HALVES = 2
SSM_SCAN_STRIP = 512

VMEM_LIMIT = 56 * 1024 * 1024


def _cparams(sem):
    return pltpu.CompilerParams(dimension_semantics=sem, vmem_limit_bytes=VMEM_LIMIT)


def _layer_norm_rows(x, g, b):
    mu = jnp.mean(x, axis=-1, keepdims=True)
    xc = x - mu
    var = jnp.mean(xc * xc, axis=-1, keepdims=True)
    return xc * lax.rsqrt(var + LN_EPS) * g + b


LANES = 128
SLAB_ROWS = D_MODEL // 2 // LANES
DMA_GROUP = 4
DMA_PRIORITIES = 2
U32 = jnp.uint32
HIGH_HALF = 0xFFFF0000


def _pack_pair(lo, hi):
    lo = lax.bitcast_convert_type(lo.astype(BF16).astype(F32), U32) >> 16
    hi = lax.bitcast_convert_type(hi.astype(BF16).astype(F32), U32) & U32(HIGH_HALF)
    return lo | hi


def _unpack_pair(w):
    return lax.bitcast_convert_type(w << 16, F32), lax.bitcast_convert_type(w & U32(HIGH_HALF), F32)


def _store_slabs(slab_ref, x, n, first_slab=0):
    half = D_MODEL // 2
    for c in range(SLAB_ROWS):
        lo = x[:, c * LANES:(c + 1) * LANES]
        hi = x[:, half + c * LANES:half + (c + 1) * LANES]
        slab_ref[pl.ds(first_slab * SLAB_ROWS + c, n, stride=SLAB_ROWS), :] = _pack_pair(lo, hi)


def _load_slabs(slab_ref, first_row, n):
    los, his = [], []
    for c in range(SLAB_ROWS):
        lo, hi = _unpack_pair(slab_ref[pl.ds(first_row + c, n, stride=SLAB_ROWS), :])
        los.append(lo)
        his.append(hi)
    return los, his


def _ln_kernel(x_ref, g_ref, b_ref, o32_ref, o16_ref):
    y = _layer_norm_rows(x_ref[...], g_ref[...], b_ref[...])
    o32_ref[...] = y
    o16_ref[...] = y.astype(BF16)


def _layer_norm_in(x, g, b, tm=512):
    t, d = x.shape
    row = pl.BlockSpec((tm, d), lambda i: (i, 0))
    vec = pl.BlockSpec((1, d), lambda i: (0, 0))
    return pl.pallas_call(
        _ln_kernel,
        grid=(t // tm,),
        in_specs=[row, vec, vec],
        out_specs=[row, row],
        out_shape=[jax.ShapeDtypeStruct((t, d), F32), jax.ShapeDtypeStruct((t, d), BF16)],
        compiler_params=_cparams(("parallel",)),
        name="ln_in",
    )(x, g.reshape(1, d), b.reshape(1, d))


def _mm_kernel(a_ref, w_ref, o_ref):
    o_ref[...] = jnp.dot(a_ref[...], w_ref[...], preferred_element_type=F32).astype(o_ref.dtype)


def _matmul(a, w, tm, tn, name):
    m, k = a.shape
    n = w.shape[1]
    return pl.pallas_call(
        _mm_kernel,
        grid=(m // tm, n // tn),
        in_specs=[pl.BlockSpec((tm, k), lambda i, j: (i, 0)), pl.BlockSpec((k, tn), lambda i, j: (0, j))],
        out_specs=pl.BlockSpec((tm, tn), lambda i, j: (i, j)),
        out_shape=jax.ShapeDtypeStruct((m, n), BF16),
        compiler_params=_cparams(("parallel", "parallel")),
        name=name,
    )(a, w)


def _swa_kernel(sink_ref, q_ref, kc_ref, vc_ref, kp_ref, vp_ref, o_ref, *, nblk):
    i = pl.program_id(1)
    grp = N_HEADS // N_KV_HEADS
    pairs = grp // 2
    pw = 2 * HEAD_DIM
    nrow = pairs * BLOCK
    nkey = 2 * BLOCK
    log2e = math.log2(math.e)
    scale = HEAD_DIM ** -0.5 * log2e
    row = lax.broadcasted_iota(I32, (nrow, nkey), 0)
    kj = lax.broadcasted_iota(I32, (nrow, nkey), 1)
    rel = (row & (BLOCK - 1)) + BLOCK - kj
    band_ok = (rel >= 0) & (rel < WINDOW)
    row_pair = lax.broadcasted_iota(I32, (nrow, 1), 0) // BLOCK
    low_lanes = lax.broadcasted_iota(I32, (nkey, pw), 1) < HEAD_DIM
    first_lo = jnp.where(i > 0, 0, BLOCK)

    def block_diag(band, g):
        band = band.astype(F32)
        swapped = pltpu.roll(band, HEAD_DIM, axis=1)
        top = jnp.where(low_lanes, band if g == 0 else swapped, 0.0)
        bottom = jnp.where(low_lanes, 0.0, swapped if g == 0 else band)
        return jnp.concatenate([top, bottom], axis=0).astype(BF16)

    for j in range(nblk):
        rows = slice(j * BLOCK, (j + 1) * BLOCK)
        if j == 0:
            k_prev, v_prev = kp_ref[...], vp_ref[...]
            valid = band_ok & (kj >= first_lo)
        else:
            prev = slice((j - 1) * BLOCK, j * BLOCK)
            k_prev, v_prev = kc_ref[prev, :], vc_ref[prev, :]
            valid = band_ok
        k_band = jnp.concatenate([k_prev, kc_ref[rows, :]], axis=0)
        v_band = jnp.concatenate([v_prev, vc_ref[rows, :]], axis=0)
        for g in range(N_KV_HEADS):
            k2 = block_diag(k_band, g)
            v2 = block_diag(v_band, g)
            cols = [slice((g * pairs + p) * pw, (g * pairs + p + 1) * pw) for p in range(pairs)]
            q2 = jnp.concatenate([q_ref[rows, c] for c in cols], axis=0)
            s = lax.dot_general(q2, k2, (((1,), (1,)), ((), ())), preferred_element_type=F32) * scale
            probs = []
            for half in range(2):
                sh = jnp.where(valid, s[:, half * nkey:(half + 1) * nkey], -jnp.inf)
                sink = sink_ref[g * grp + half] * log2e
                for p in range(1, pairs):
                    sink = jnp.where(row_pair == p, sink_ref[g * grp + 2 * p + half] * log2e, sink)
                m = jnp.maximum(jnp.max(sh, axis=-1, keepdims=True), sink)
                e = jnp.exp2(sh - m)
                den = jnp.sum(e, axis=-1, keepdims=True) + jnp.exp2(sink - m)
                probs.append((e * (1.0 / den)).astype(BF16))
            out = jnp.dot(jnp.concatenate(probs, axis=-1), v2, preferred_element_type=F32)
            for p, c in enumerate(cols):
                o_ref[rows, c] = out[p * BLOCK:(p + 1) * BLOCK, :].astype(o_ref.dtype)


def _sliding_window_attention(proj, sinks, b, s, tq=512):
    t = b * s
    nq = s // tq
    nblk = tq // BLOCK
    sb = s // BLOCK
    cur = lambda col: (lambda bi, i: (bi * nq + i, col))
    prev = lambda col: (lambda bi, i: (bi * sb + jnp.maximum(i * nblk - 1, 0), col))
    return pl.pallas_call(
        functools.partial(_swa_kernel, nblk=nblk),
        grid=(b, nq),
        in_specs=[
            pl.BlockSpec(memory_space=pltpu.SMEM),
            pl.BlockSpec((tq, ATTN_WIDTH), cur(Q_OFF // ATTN_WIDTH)),
            pl.BlockSpec((tq, KV_WIDTH), cur(K_OFF // KV_WIDTH)),
            pl.BlockSpec((tq, KV_WIDTH), cur(V_OFF // KV_WIDTH)),
            pl.BlockSpec((BLOCK, KV_WIDTH), prev(K_OFF // KV_WIDTH)),
            pl.BlockSpec((BLOCK, KV_WIDTH), prev(V_OFF // KV_WIDTH)),
        ],
        out_specs=pl.BlockSpec((tq, ATTN_WIDTH), lambda bi, i: (bi * nq + i, 0)),
        out_shape=jax.ShapeDtypeStruct((t, ATTN_WIDTH), BF16),
        compiler_params=_cparams(("parallel", "parallel")),
        name="swa",
    )(sinks, proj, proj, proj, proj, proj)


def _mem_attn_kernel(q_ref, k_ref, v_ref, o_ref):
    s = lax.dot_general(q_ref[...], k_ref[...], (((1,), (1,)), ((), ())),
                        preferred_element_type=F32) * (MEM_HEAD_DIM ** -0.5)
    m = jnp.max(s, axis=-1, keepdims=True)
    e = jnp.exp(s - m)
    p = (e * (1.0 / jnp.sum(e, axis=-1, keepdims=True))).astype(BF16)
    o_ref[...] = jnp.dot(p, v_ref[...], preferred_element_type=F32).astype(o_ref.dtype)


def _memory_attention(proj, mem_kv, b, s):
    t = b * s
    qcol = QM_OFF // MEM_HEAD_DIM
    return pl.pallas_call(
        _mem_attn_kernel,
        grid=(b, MEM_HEADS),
        in_specs=[
            pl.BlockSpec((s, MEM_HEAD_DIM), lambda bi, h: (bi, qcol + h)),
            pl.BlockSpec((N_MEM, MEM_HEAD_DIM), lambda bi, h: (bi, h)),
            pl.BlockSpec((N_MEM, MEM_HEAD_DIM), lambda bi, h: (bi, MEM_HEADS + h)),
        ],
        out_specs=pl.BlockSpec((s, MEM_HEAD_DIM), lambda bi, h: (bi, h)),
        out_shape=jax.ShapeDtypeStruct((t, MEM_WIDTH), BF16),
        compiler_params=_cparams(("parallel", "parallel")),
        name="mem_attn",
    )(proj, mem_kv, mem_kv)


def _gelu_tanh(x):
    return 0.5 * x * (1.0 + jnp.tanh(math.sqrt(2.0 / math.pi) * (x + 0.044715 * (x * x * x))))


def _ssm_kernel(u_ref, wbr_ref, wbi_ref, lr_ref, li_ref, wcr_ref, wci_ref, d_ref, wglu_ref, o_ref,
                xr_ref, xi_ref, sr_ref, si_ref, *, nb, tc):
    @pl.when(pl.program_id(0) == 0)
    def _():
        sr_ref[...] = jnp.zeros_like(sr_ref)
        si_ref[...] = jnp.zeros_like(si_ref)

    u = u_ref[...]
    uw = SSM_WIDTH // SSM_HALVES
    xw = SSM_COMPLEX // SSM_HALVES
    for hf in range(SSM_HALVES):
        uh = u[:, hf * uw:(hf + 1) * uw]
        xr_ref[:, hf * xw:(hf + 1) * xw] = jnp.dot(uh, wbr_ref[hf], preferred_element_type=F32)
        xi_ref[:, hf * xw:(hf + 1) * xw] = jnp.dot(uh, wbi_ref[hf], preferred_element_type=F32)

    for st in range(SSM_COMPLEX // SSM_SCAN_STRIP):
        cols = slice(st * SSM_SCAN_STRIP, (st + 1) * SSM_SCAN_STRIP)
        lr = jnp.broadcast_to(lr_ref[:, cols], (nb, SSM_SCAN_STRIP))
        li = jnp.broadcast_to(li_ref[:, cols], (nb, SSM_SCAN_STRIP))

        def step(t, carry):
            sr, si = carry
            r0 = pl.multiple_of(t * nb, nb)
            nr = lr * sr - li * si + xr_ref[pl.ds(r0, nb), cols]
            ni = lr * si + li * sr + xi_ref[pl.ds(r0, nb), cols]
            xr_ref[pl.ds(r0, nb), cols] = nr
            xi_ref[pl.ds(r0, nb), cols] = ni
            return nr, ni

        sr, si = lax.fori_loop(0, tc, step, (sr_ref[:, cols], si_ref[:, cols]), unroll=8)
        sr_ref[:, cols] = sr
        si_ref[:, cols] = si

    ys = []
    for hf in range(SSM_HALVES):
        xs = slice(hf * xw, (hf + 1) * xw)
        yr = jnp.dot(xr_ref[:, xs].astype(BF16), wcr_ref[hf], preferred_element_type=F32)
        yi = jnp.dot(xi_ref[:, xs].astype(BF16), wci_ref[hf], preferred_element_type=F32)
        ys.append(yr - yi)
    y = jnp.concatenate(ys, axis=-1) + d_ref[...] * u.astype(F32)
    zg = jnp.dot(_gelu_tanh(y).astype(BF16), wglu_ref[...], preferred_element_type=F32)
    o_ref[...] = (zg[:, :SSM_WIDTH] * jax.nn.sigmoid(zg[:, SSM_WIDTH:])).astype(o_ref.dtype)


def _ssm_weights(lambda_re, lambda_im, log_dt, b_re, b_im, c_re, c_im, d_skip):
    g, p, h = N_SSM_GROUPS, SSM_STATE, SSM_GROUP
    gh = g // SSM_HALVES
    lam = lax.complex(lambda_re.astype(F32), lambda_im.astype(F32))
    dt = jnp.exp(log_dt.astype(F32))[:, None]
    lam_bar = jnp.exp(lam * dt)
    b_bar = ((lam_bar - 1.0) / lam)[..., None] * lax.complex(b_re.astype(F32), b_im.astype(F32))
    eye = jnp.eye(gh, dtype=F32)

    def blockdiag_in(m):
        m = m.reshape(SSM_HALVES, gh, p, h)
        return jnp.einsum('xgph,gk->xghkp', m, eye).reshape(SSM_HALVES, gh * h, gh * p).astype(BF16)

    def blockdiag_out(m):
        m = m.reshape(SSM_HALVES, gh, h, p)
        return jnp.einsum('xghp,gk->xgpkh', m, eye).reshape(SSM_HALVES, gh * p, gh * h).astype(BF16)

    return (blockdiag_in(jnp.real(b_bar)), blockdiag_in(jnp.imag(b_bar)),
            jnp.real(lam_bar).reshape(1, g * p), jnp.imag(lam_bar).reshape(1, g * p),
            blockdiag_out(c_re.astype(F32)), blockdiag_out(c_im.astype(F32)),
            d_skip.astype(F32).reshape(1, g * h))


def _ssm_branch(u_tm, ssm_w, w_glu, nb, s, tc=64):
    wbr, wbi, lr, li, wcr, wci, d = ssm_w
    rows = tc * nb
    full = lambda a: pl.BlockSpec(a.shape, lambda i: (0,) * a.ndim)
    return pl.pallas_call(
        functools.partial(_ssm_kernel, nb=nb, tc=tc),
        grid=(s // tc,),
        in_specs=[pl.BlockSpec((rows, SSM_WIDTH), lambda i: (i, 0)),
                  full(wbr), full(wbi), full(lr), full(li), full(wcr), full(wci), full(d), full(w_glu)],
        out_specs=pl.BlockSpec((rows, SSM_WIDTH), lambda i: (i, 0)),
        out_shape=jax.ShapeDtypeStruct((s * nb, SSM_WIDTH), BF16),
        scratch_shapes=[pltpu.VMEM((rows, SSM_COMPLEX), F32), pltpu.VMEM((rows, SSM_COMPLEX), F32),
                        pltpu.VMEM((nb, SSM_COMPLEX), F32), pltpu.VMEM((nb, SSM_COMPLEX), F32)],
        compiler_params=_cparams(("arbitrary",)),
        name="ssm",
    )(u_tm, wbr, wbi, lr, li, wcr, wci, d, w_glu)


def _merge_kernel(attn_ref, ssm_ref, mem_ref, g0_ref, g1_ref, g2_ref, h_ref, wb_ref, wo_ref, lg_ref, lb_ref,
                  o32_ref, slab_ref, *, tm, sub):
    for r in range(tm // sub):
        rows = slice(r * sub, (r + 1) * sub)

        def gated(x_ref, g_ref, lo, hi):
            br = jnp.dot(x_ref[rows, :], wb_ref[lo:hi, :], preferred_element_type=F32)
            return jax.nn.sigmoid(g_ref[rows, :].astype(F32)) * br

        merged = gated(attn_ref, g0_ref, 0, ATTN_WIDTH)
        merged += gated(ssm_ref, g1_ref, ATTN_WIDTH, ATTN_WIDTH + SSM_WIDTH)
        merged += gated(mem_ref, g2_ref, ATTN_WIDTH + SSM_WIDTH, ATTN_WIDTH + SSM_WIDTH + MEM_WIDTH)
        mix = jnp.dot(merged.astype(BF16), wo_ref[...], preferred_element_type=F32)
        y = _layer_norm_rows(DEEPNORM_ALPHA * h_ref[rows, :] + mix, lg_ref[...], lb_ref[...])
        o32_ref[rows, :] = y
        _store_slabs(slab_ref, y, sub, first_slab=r * sub)


def _merge_out_ln(attn, ssm, mem, proj, h, w_branch, w_out, ln_g, ln_b, tm=256, sub=256):
    t, d = h.shape
    row = lambda w, col=0: pl.BlockSpec((tm, w), lambda i: (i, col))
    const = lambda shape: pl.BlockSpec(shape, lambda i: (0, 0))
    return pl.pallas_call(
        functools.partial(_merge_kernel, tm=tm, sub=sub),
        grid=(t // tm,),
        in_specs=[row(ATTN_WIDTH), row(SSM_WIDTH), row(MEM_WIDTH),
                  row(d, G_OFF // d), row(d, G_OFF // d + 1), row(d, G_OFF // d + 2),
                  row(d), const(w_branch.shape), const(w_out.shape), const((1, d)), const((1, d))],
        out_specs=[row(d), pl.BlockSpec((tm * SLAB_ROWS, LANES), lambda i: (i, 0))],
        out_shape=[jax.ShapeDtypeStruct((t, d), F32), jax.ShapeDtypeStruct((t * SLAB_ROWS, LANES), U32)],
        compiler_params=_cparams(("parallel",)),
        name="merge_out_ln1",
    )(attn, ssm, mem, proj, proj, proj, h, w_branch, w_out, ln_g.reshape(1, d), ln_b.reshape(1, d))


def _split_bf16(x):
    hi = x.astype(BF16)
    return hi, (x - hi.astype(F32)).astype(BF16)


def _router_kernel(h_ref, wt_ref, b_ref, idx_ref, gate_ref, rank_ref, cnt_ref, carry_ref, *, tm):
    @pl.when(pl.program_id(0) == 0)
    def _():
        carry_ref[...] = jnp.zeros_like(carry_ref)

    nt = (((1,), (1,)), ((), ()))
    h_hi, h_lo = _split_bf16(h_ref[...])
    w_hi, w_lo = _split_bf16(wt_ref[...])
    lg = (lax.dot_general(w_hi, h_hi, nt, preferred_element_type=F32)
          + lax.dot_general(w_hi, h_lo, nt, preferred_element_type=F32)
          + lax.dot_general(w_lo, h_hi, nt, preferred_element_type=F32)) + b_ref[...]

    e_iota = lax.broadcasted_iota(I32, (N_EXPERTS, tm), 0)
    chosen = jnp.zeros((N_EXPERTS, tm), F32)
    vals, sels = [], []
    for k in range(TOP_K):
        m = jnp.max(lg, axis=0, keepdims=True)
        idx = jnp.min(jnp.where(lg == m, e_iota, N_EXPERTS), axis=0, keepdims=True)
        sel = e_iota == idx
        idx_ref[k:k + 1, :] = idx
        vals.append(m)
        sels.append(sel)
        chosen = jnp.where(sel, 1.0, chosen)
        lg = jnp.where(sel, -jnp.inf, lg)

    ex = [jnp.exp(v - vals[0]) for v in vals]
    inv = 1.0 / (ex[0] + ex[1] + ex[2] + ex[3])
    for k in range(TOP_K):
        gate_ref[k:k + 1, :] = ex[k] * inv

    r = lax.broadcasted_iota(I32, (tm, tm), 0)
    c = lax.broadcasted_iota(I32, (tm, tm), 1)
    before = jnp.where(r < c, 1.0, 0.0).astype(BF16)
    earlier = jnp.dot(chosen.astype(BF16), before, preferred_element_type=F32) + carry_ref[...]
    for k in range(TOP_K):
        rank_ref[k:k + 1, :] = jnp.sum(jnp.where(sels[k], earlier, 0.0), axis=0, keepdims=True).astype(I32)
    carry_ref[...] += jnp.sum(chosen, axis=1, keepdims=True)
    cnt_ref[...] = carry_ref[...].astype(I32)


def _router(h, w_router, b_router, tm=512):
    t, d = h.shape
    out = lambda dt: jax.ShapeDtypeStruct((TOP_K, t), dt)
    tok = pl.BlockSpec((TOP_K, tm), lambda i: (0, i))
    return pl.pallas_call(
        functools.partial(_router_kernel, tm=tm),
        grid=(t // tm,),
        in_specs=[pl.BlockSpec((tm, d), lambda i: (i, 0)),
                  pl.BlockSpec((N_EXPERTS, d), lambda i: (0, 0)),
                  pl.BlockSpec((N_EXPERTS, 1), lambda i: (0, 0))],
        out_specs=[tok, tok, tok, pl.BlockSpec((N_EXPERTS, 1), lambda i: (0, 0))],
        out_shape=[out(I32), out(F32), out(I32), jax.ShapeDtypeStruct((N_EXPERTS, 1), I32)],
        scratch_shapes=[pltpu.VMEM((N_EXPERTS, 1), F32)],
        compiler_params=_cparams(("arbitrary",)),
        name="router",
    )(h, w_router.T, b_router.reshape(N_EXPERTS, 1))


def _slab(ref, index):
    return ref.at[pl.ds(pl.multiple_of(index * SLAB_ROWS, SLAB_ROWS), SLAB_ROWS)]


ZERO_RUNS = tuple(1 << s for s in reversed(range(MOE_BLOCK.bit_length() - 1)))


def _dispatch_kernel(dest_ref, pad_at_ref, pad_len_ref, src_ref, x_ref, zero_ref, sem, zero_sem, *, tm):
    def zero_fill(act):
        def per_expert(e, c):
            at, n = pad_at_ref[e], pad_len_ref[e]
            for run in ZERO_RUNS:
                take = n & run

                @pl.when(take != 0)
                def _(at=at, run=run):
                    rows = run * SLAB_ROWS
                    dst = x_ref.at[pl.ds(pl.multiple_of(at * SLAB_ROWS, SLAB_ROWS), rows)]
                    act(pltpu.make_async_copy(zero_ref.at[pl.ds(0, rows)], dst, zero_sem))
                at = at + take
            return c
        lax.fori_loop(0, N_EXPERTS, per_expert, 0)

    @pl.when(pl.program_id(0) == 0)
    def _():
        zero_ref[...] = jnp.zeros_like(zero_ref)
        zero_fill(lambda cp: cp.start())
        zero_fill(lambda cp: cp.wait())

    def copy(t, dst):
        return pltpu.make_async_copy(_slab(src_ref, t), _slab(x_ref, dst), sem)

    def start(g, c):
        toks = [g * DMA_GROUP + j for j in range(DMA_GROUP)]
        dsts = [[dest_ref[t * TOP_K + k] for k in range(TOP_K)] for t in toks]
        for t, row in zip(toks, dsts):
            for k, dst in enumerate(row):
                copy(t, dst).start(priority=k % DMA_PRIORITIES)
        return c

    def wait(t, c):
        for k in range(TOP_K):
            copy(t, 0).wait()
        return c

    lax.fori_loop(0, tm // DMA_GROUP, start, 0)
    lax.fori_loop(0, tm, wait, 0, unroll=4)


def _dispatch(h_slabs, dest, pad_at, pad_len, n_rows, tm=1024):
    t = dest.shape[0] // TOP_K
    smem = pl.BlockSpec(memory_space=pltpu.SMEM)
    return pl.pallas_call(
        functools.partial(_dispatch_kernel, tm=tm),
        grid=(t // tm,),
        in_specs=[pl.BlockSpec((TOP_K * tm,), lambda i: (i,), memory_space=pltpu.SMEM), smem, smem,
                  pl.BlockSpec((tm * SLAB_ROWS, LANES), lambda i: (i, 0))],
        out_specs=pl.BlockSpec(memory_space=pl.ANY),
        out_shape=jax.ShapeDtypeStruct((n_rows * SLAB_ROWS, LANES), U32),
        scratch_shapes=[pltpu.VMEM((ZERO_RUNS[0] * SLAB_ROWS, LANES), U32),
                        pltpu.SemaphoreType.DMA(()), pltpu.SemaphoreType.DMA(())],
        compiler_params=_cparams(("arbitrary",)),
        name="moe_dispatch",
    )(dest, pad_at, pad_len, h_slabs)


UP_CHUNK = 256
UP_HALF = UP_CHUNK // 2


PREP_PARTS = 8


def _in_proj_kernel(a_ref, w_ref, wup_ref, wdn_ref, o_ref, oup_ref, odn_ref):
    o_ref[...] = jnp.dot(a_ref[...], w_ref[...], preferred_element_type=F32).astype(o_ref.dtype)
    r = lax.broadcasted_iota(I32, (UP_CHUNK, UP_CHUNK), 0)
    c = lax.broadcasted_iota(I32, (UP_CHUNK, UP_CHUNK), 1)
    src = jnp.where(c < UP_HALF, 2 * c, 2 * (c - UP_HALF) + 1)
    perm = jnp.where(r == src, 1.0, 0.0).astype(BF16)
    oup_ref[0] = jnp.dot(wup_ref[0].astype(BF16), perm, preferred_element_type=F32).astype(BF16)
    odn_ref[0] = wdn_ref[0].astype(BF16)


def _in_proj_and_moe_weights(a, w, w_up, w_down, layer, tm_max=1024, tn=768):
    m, k = a.shape
    n = w.shape[1]
    _, e, d, n_up = w_up.shape
    f = w_down.shape[2]
    nj = n // tn
    pieces = e * PREP_PARTS
    tm = tm_max
    while (m // tm) * nj < pieces:
        tm //= 2
    assert n_up == PREP_PARTS * UP_CHUNK and f % PREP_PARTS == 0 and m % tm == 0
    fp = f // PREP_PARTS

    def piece(i, j):
        u = jnp.minimum(i * nj + j, pieces - 1)
        return u // PREP_PARTS, u % PREP_PARTS

    def up_in(i, j):
        ex, part = piece(i, j)
        return layer, ex, 0, part

    def up_out(i, j):
        ex, part = piece(i, j)
        return ex, 0, part

    def down_in(i, j):
        ex, part = piece(i, j)
        return layer, ex, part, 0

    def down_out(i, j):
        ex, part = piece(i, j)
        return ex, part, 0

    return pl.pallas_call(
        _in_proj_kernel,
        grid=(m // tm, nj),
        in_specs=[pl.BlockSpec((tm, k), lambda i, j: (i, 0)), pl.BlockSpec((k, tn), lambda i, j: (0, j)),
                  pl.BlockSpec((None, 1, d, UP_CHUNK), up_in), pl.BlockSpec((None, 1, fp, d), down_in)],
        out_specs=[pl.BlockSpec((tm, tn), lambda i, j: (i, j)),
                   pl.BlockSpec((1, d, UP_CHUNK), up_out), pl.BlockSpec((1, fp, d), down_out)],
        out_shape=[jax.ShapeDtypeStruct((m, n), BF16), jax.ShapeDtypeStruct((e, d, n_up), BF16),
                   jax.ShapeDtypeStruct((e, f, d), BF16)],
        compiler_params=_cparams(("arbitrary", "arbitrary")),
        name="in_proj",
    )(a, w, w_up, w_down)


def _pair_split_bias(b_up):
    e, n = b_up.shape
    return b_up.reshape(e, n // UP_CHUNK, UP_HALF, 2).transpose(0, 1, 3, 2).reshape(e, 1, n).astype(F32)


def _expert_kernel(be_ref, nu_ref, x_ref, wu_ref, bu_ref, wd_ref, bd_ref, o_ref):
    del be_ref
    i = pl.program_id(0)

    @pl.when(i < nu_ref[0])
    def _():
        los, his = _load_slabs(x_ref, 0, MOE_BLOCK)
        x = jnp.concatenate([p.astype(BF16) for p in los + his], axis=-1)
        up = jnp.dot(x, wu_ref[0], preferred_element_type=F32) + bu_ref[0]
        acts = []
        for ch in range(up.shape[-1] // UP_CHUNK):
            x_glu = jnp.minimum(up[:, ch * UP_CHUNK:ch * UP_CHUNK + UP_HALF], SWIGLU_LIMIT)
            x_lin = jnp.clip(up[:, ch * UP_CHUNK + UP_HALF:(ch + 1) * UP_CHUNK], -SWIGLU_LIMIT, SWIGLU_LIMIT)
            acts.append((x_glu * jax.nn.sigmoid(SWIGLU_ALPHA * x_glu) * (x_lin + 1.0)).astype(BF16))
        act = jnp.concatenate(acts, axis=-1)
        out = jnp.dot(act, wd_ref[0], preferred_element_type=F32) + bd_ref[0]
        _store_slabs(o_ref, out, MOE_BLOCK)

    @pl.when(i >= nu_ref[0])
    def _():
        o_ref[...] = jnp.zeros_like(o_ref)


def _experts(x_pad, block_e, n_used, wu, bu, wd, bd):
    n_blocks = x_pad.shape[0] // (MOE_BLOCK * SLAB_ROWS)
    row = pl.BlockSpec((MOE_BLOCK * SLAB_ROWS, LANES), lambda i, be, nu: (i, 0))
    used_row = pl.BlockSpec((MOE_BLOCK * SLAB_ROWS, LANES), lambda i, be, nu: (jnp.minimum(i, nu[0] - 1), 0))
    per_e = lambda a: pl.BlockSpec((1,) + a.shape[1:], lambda i, be, nu: (be[i], 0, 0))
    return pl.pallas_call(
        _expert_kernel,
        grid_spec=pltpu.PrefetchScalarGridSpec(
            num_scalar_prefetch=2,
            grid=(n_blocks,),
            in_specs=[used_row, per_e(wu), per_e(bu), per_e(wd), per_e(bd)],
            out_specs=row,
        ),
        out_shape=jax.ShapeDtypeStruct(x_pad.shape, U32),
        compiler_params=_cparams(("arbitrary",)),
        name="moe_experts",
    )(block_e, n_used, x_pad, wu, bu, wd, bd)


def _combine_kernel(dest_ref, next_dest_ref, y_ref, gate_ref, h_ref, lg_ref, lb_ref, o32_ref, o16_ref,
                    buf0_ref, buf1_ref, sems, *, tm, n_tiles):
    i = pl.program_id(0)
    bufs = (buf0_ref, buf1_ref)

    def copy(src, t, k, sl):
        return pltpu.make_async_copy(_slab(y_ref, src), _slab(bufs[sl], k * tm + t), sems.at[sl])

    def start_group(d_ref, g, sl):
        toks = [g * DMA_GROUP + j for j in range(DMA_GROUP)]
        srcs = [[d_ref[t * TOP_K + k] for k in range(TOP_K)] for t in toks]
        for t, row in zip(toks, srcs):
            for k, src in enumerate(row):
                copy(src, t, k, sl).start(priority=k % DMA_PRIORITIES)

    def wait_tile(sl):
        def wait(t, c):
            for k in range(TOP_K):
                copy(0, t, k, sl).wait()
            return c
        lax.fori_loop(0, tm, wait, 0, unroll=4)

    def combine(buf_ref):
        g = gate_ref[...]
        gk = [jnp.broadcast_to(g[:, k:k + 1], (tm, LANES)) for k in range(TOP_K)]
        lo_acc, hi_acc = None, None
        for k in range(TOP_K):
            los, his = _load_slabs(buf_ref, k * tm * SLAB_ROWS, tm)
            los = [p * gk[k] for p in los]
            his = [p * gk[k] for p in his]
            lo_acc = los if lo_acc is None else [a + p for a, p in zip(lo_acc, los)]
            hi_acc = his if hi_acc is None else [a + p for a, p in zip(hi_acc, his)]
        ffn = jnp.concatenate(lo_acc + hi_acc, axis=-1)
        y = _layer_norm_rows(DEEPNORM_ALPHA * h_ref[...] + ffn, lg_ref[...], lb_ref[...])
        o32_ref[...] = y
        o16_ref[...] = y.astype(BF16)

    @pl.when(i == 0)
    def _():
        def first(g, c):
            start_group(dest_ref, g, 0)
            return c
        lax.fori_loop(0, tm // DMA_GROUP, first, 0)

    def step(sl):
        wait_tile(sl)
        for g in range(tm // DMA_GROUP):
            start_group(next_dest_ref, g, 1 - sl)
        combine(bufs[sl])

    for sl in range(2):
        pl.when(i % 2 == sl)(functools.partial(step, sl))

    @pl.when(i == n_tiles - 1)
    def _():
        wait_tile(1 - (n_tiles - 1) % 2)


def _combine_ln(out_pad, dest, gate_t, h, ln_g, ln_b, tm=256):
    t, d = h.shape
    n = t // tm
    row = pl.BlockSpec((tm, d), lambda i: (i, 0))
    vec = pl.BlockSpec((1, d), lambda i: (0, 0))
    return pl.pallas_call(
        functools.partial(_combine_kernel, tm=tm, n_tiles=n),
        grid=(n,),
        in_specs=[pl.BlockSpec((TOP_K * tm,), lambda i: (i,), memory_space=pltpu.SMEM),
                  pl.BlockSpec((TOP_K * tm,), lambda i: (jnp.minimum(i + 1, n - 1),), memory_space=pltpu.SMEM),
                  pl.BlockSpec(memory_space=pl.ANY),
                  pl.BlockSpec((tm, TOP_K), lambda i: (i, 0)),
                  row, vec, vec],
        out_specs=[row, row],
        out_shape=[jax.ShapeDtypeStruct((t, d), F32), jax.ShapeDtypeStruct((t, d), BF16)],
        scratch_shapes=[pltpu.VMEM((TOP_K * tm * SLAB_ROWS, LANES), U32),
                        pltpu.VMEM((TOP_K * tm * SLAB_ROWS, LANES), U32), pltpu.SemaphoreType.DMA((2,))],
        compiler_params=_cparams(("arbitrary",)),
        name="moe_combine_ln2",
    )(dest, dest, out_pad, gate_t, h, ln_g.reshape(1, d), ln_b.reshape(1, d))


def _moe_layout(idx, rank, counts, n_blocks):
    counts = counts.reshape(N_EXPERTS)
    padded = (counts + MOE_BLOCK - 1) // MOE_BLOCK * MOE_BLOCK
    pad_end = jnp.cumsum(padded)
    pad_start = pad_end - padded
    experts = jnp.arange(N_EXPERTS, dtype=I32)[:, None, None]
    dest = rank + jnp.sum(jnp.where(idx[None] == experts, pad_start[:, None, None], 0), axis=0)
    block_rows = jnp.arange(n_blocks, dtype=I32) * MOE_BLOCK
    block_e = jnp.minimum(jnp.sum(block_rows[:, None] >= pad_end[None, :], axis=1), N_EXPERTS - 1).astype(I32)
    n_used = (pad_end[-1:] // MOE_BLOCK).astype(I32)
    pad_at = (pad_start + counts).astype(I32)
    pad_len = (padded - counts).astype(I32)
    return dest.astype(I32).T.reshape(-1), block_e, n_used, pad_at, pad_len


def _permute_w_in(w):
    pieces = [w[:, QM_END:], w[:, :Q_END], w[:, V_END:U_END], w[:, U_END:QM_END], w[:, Q_END:K_END], w[:, K_END:V_END]]
    return jnp.concatenate(pieces, axis=1).astype(BF16)


def kernel(x, mem, ln_in_g, ln_in_b, w_in, attn_sinks, ssm_lambda_re, ssm_lambda_im, ssm_log_dt, ssm_b_re, ssm_b_im, ssm_c_re, ssm_c_im, ssm_d, w_glu, w_mem_kv, w_branch, w_out, ln1_g, ln1_b, w_router, b_router, w_up, b_up, w_down, b_down, ln2_g, ln2_b):
    b, s, d = x.shape
    t = b * s
    n_blocks = -(-(t * TOP_K) // MOE_BLOCK) + N_EXPERTS
    mem16 = mem.reshape(b * N_MEM, d).astype(BF16)

    h32, h16 = _layer_norm_in(x.reshape(t, d), ln_in_g, ln_in_b)
    for l in range(DEPTH):
        proj, wu16, wd16 = _in_proj_and_moe_weights(h16, _permute_w_in(w_in[l]), w_up, w_down, l)
        mem_kv = _matmul(mem16, w_mem_kv[l].astype(BF16), tm=1024, tn=512, name="mem_kv")

        attn = _sliding_window_attention(proj, attn_sinks[l].astype(F32), b, s)
        mem_out = _memory_attention(proj, mem_kv, b, s)
        ssm_w = _ssm_weights(ssm_lambda_re[l], ssm_lambda_im[l], ssm_log_dt[l], ssm_b_re[l], ssm_b_im[l],
                             ssm_c_re[l], ssm_c_im[l], ssm_d[l])
        u_tm = proj[:, U_OFF:U_OFF + SSM_WIDTH].reshape(b, s, SSM_WIDTH).transpose(1, 0, 2).reshape(t, SSM_WIDTH)
        ssm_tm = _ssm_branch(u_tm, ssm_w, w_glu[l].astype(BF16), b, s)
        ssm_out = ssm_tm.reshape(s, b, SSM_WIDTH).transpose(1, 0, 2).reshape(t, SSM_WIDTH)

        h1, h1_slabs = _merge_out_ln(attn, ssm_out, mem_out, proj, h32, w_branch[l].astype(BF16),
                                     w_out[l].astype(BF16), ln1_g[l], ln1_b[l])

        idx, gate, rank, counts = _router(h1, w_router[l], b_router[l])
        dest, block_e, n_used, pad_at, pad_len = _moe_layout(idx, rank, counts, n_blocks)
        x_pad = _dispatch(h1_slabs, dest, pad_at, pad_len, n_blocks * MOE_BLOCK)
        out_pad = _experts(x_pad, block_e, n_used, wu16, _pair_split_bias(b_up[l]), wd16,
                           b_down[l][:, None, :].astype(F32))
        h32, h16 = _combine_ln(out_pad, dest, gate.T, h1, ln2_g[l], ln2_b[l])
    return h32.reshape(b, s, d)
```

```python
import functools
import math

import jax
import jax.numpy as jnp
from jax import lax
from jax.experimental import pallas as pl
from jax.experimental.pallas import tpu as pltpu

F32 = jnp.float32
BF16 = jnp.bfloat16
I32 = jnp.int32

D_MODEL = 2048
DEPTH = 2
N_HEADS = 16
N_KV_HEADS = 2
HEAD_DIM = 64
WINDOW = 128
BLOCK = 128
ATTN_WIDTH = N_HEADS * HEAD_DIM
KV_WIDTH = N_KV_HEADS * HEAD_DIM
SSM_WIDTH = D_MODEL // 4
SSM_GROUP = 16
N_SSM_GROUPS = SSM_WIDTH // SSM_GROUP
SSM_STATE = 64
N_MEM = 256
MEM_HEADS = 4
MEM_HEAD_DIM = 128
MEM_WIDTH = MEM_HEADS * MEM_HEAD_DIM
N_BRANCHES = 3
Q_END = ATTN_WIDTH
K_END = Q_END + KV_WIDTH
V_END = K_END + KV_WIDTH
U_END = V_END + SSM_WIDTH
QM_END = U_END + MEM_WIDTH
IN_WIDTH = QM_END + N_BRANCHES * D_MODEL
N_EXPERTS = 32
TOP_K = 4
D_FF = D_MODEL // 2
MOE_BLOCK = 512
SWIGLU_ALPHA = 1.702
SWIGLU_LIMIT = 7.0
LN_EPS = 1e-5
DEEPNORM_ALPHA = (2.0 * DEPTH) ** 0.25

G_OFF = 0
Q_OFF = N_BRANCHES * D_MODEL
U_OFF = Q_OFF + ATTN_WIDTH
QM_OFF = U_OFF + SSM_WIDTH
K_OFF = QM_OFF + MEM_WIDTH
V_OFF = K_OFF + KV_WIDTH

SSM_COMPLEX = N_SSM_GROUPS * SSM_STATE
SSM_HALVES = 2
SSM_SCAN_STRIP = 512

VMEM_LIMIT = 56 * 1024 * 1024


def _cparams(sem):
    return pltpu.CompilerParams(dimension_semantics=sem, vmem_limit_bytes=VMEM_LIMIT)


def _layer_norm_rows(x, g, b):
    mu = jnp.mean(x, axis=-1, keepdims=True)
    xc = x - mu
    var = jnp.mean(xc * xc, axis=-1, keepdims=True)
    return xc * lax.rsqrt(var + LN_EPS) * g + b


LANES = 128
SLAB_ROWS = D_MODEL // 2 // LANES
DMA_GROUP = 4
DMA_PRIORITIES = 2
U32 = jnp.uint32
HIGH_HALF = 0xFFFF0000


def _pack_pair(lo, hi):
    lo = lax.bitcast_convert_type(lo.astype(BF16).astype(F32), U32) >> 16
    hi = lax.bitcast_convert_type(hi.astype(BF16).astype(F32), U32) & U32(HIGH_HALF)
    return lo | hi


def _unpack_pair(w):
    return lax.bitcast_convert_type(w << 16, F32), lax.bitcast_convert_type(w & U32(HIGH_HALF), F32)


def _store_slabs(slab_ref, x, n, first_slab=0):
    half = D_MODEL // 2
    for c in range(SLAB_ROWS):
        lo = x[:, c * LANES:(c + 1) * LANES]
        hi = x[:, half + c * LANES:half + (c + 1) * LANES]
        slab_ref[pl.ds(first_slab * SLAB_ROWS + c, n, stride=SLAB_ROWS), :] = _pack_pair(lo, hi)


def _load_slabs(slab_ref, first_row, n):
    los, his = [], []
    for c in range(SLAB_ROWS):
        lo, hi = _unpack_pair(slab_ref[pl.ds(first_row + c, n, stride=SLAB_ROWS), :])
        los.append(lo)
        his.append(hi)
    return los, his


def _ln_kernel(x_ref, g_ref, b_ref, o32_ref, o16_ref):
    y = _layer_norm_rows(x_ref[...], g_ref[...], b_ref[...])
    o32_ref[...] = y
    o16_ref[...] = y.astype(BF16)


def _layer_norm_in(x, g, b, tm=512):
    t, d = x.shape
    row = pl.BlockSpec((tm, d), lambda i: (i, 0))
    vec = pl.BlockSpec((1, d), lambda i: (0, 0))
    return pl.pallas_call(
        _ln_kernel,
        grid=(t // tm,),
        in_specs=[row, vec, vec],
        out_specs=[row, row],
        out_shape=[jax.ShapeDtypeStruct((t, d), F32), jax.ShapeDtypeStruct((t, d), BF16)],
        compiler_params=_cparams(("parallel",)),
        name="ln_in",
    )(x, g.reshape(1, d), b.reshape(1, d))


def _mm_kernel(a_ref, w_ref, o_ref):
    o_ref[...] = jnp.dot(a_ref[...], w_ref[...], preferred_element_type=F32).astype(o_ref.dtype)


def _matmul(a, w, tm, tn, name):
    m, k = a.shape
    n = w.shape[1]
    return pl.pallas_call(
        _mm_kernel,
        grid=(m // tm, n // tn),
        in_specs=[pl.BlockSpec((tm, k), lambda i, j: (i, 0)), pl.BlockSpec((k, tn), lambda i, j: (0, j))],
        out_specs=pl.BlockSpec((tm, tn), lambda i, j: (i, j)),
        out_shape=jax.ShapeDtypeStruct((m, n), BF16),
        compiler_params=_cparams(("parallel", "parallel")),
        name=name,
    )(a, w)


def _swa_kernel(sink_ref, q_ref, kc_ref, vc_ref, kp_ref, vp_ref, o_ref, *, nblk):
    i = pl.program_id(1)
    grp = N_HEADS // N_KV_HEADS
    pairs = grp // 2
    pw = 2 * HEAD_DIM
    nrow = pairs * BLOCK
    nkey = 2 * BLOCK
    log2e = math.log2(math.e)
    scale = HEAD_DIM ** -0.5 * log2e
    row = lax.broadcasted_iota(I32, (nrow, nkey), 0)
    kj = lax.broadcasted_iota(I32, (nrow, nkey), 1)
    rel = (row & (BLOCK - 1)) + BLOCK - kj
    band_ok = (rel >= 0) & (rel < WINDOW)
    row_pair = lax.broadcasted_iota(I32, (nrow, 1), 0) // BLOCK
    low_lanes = lax.broadcasted_iota(I32, (nkey, pw), 1) < HEAD_DIM
    first_lo = jnp.where(i > 0, 0, BLOCK)

    def block_diag(band, g):
        band = band.astype(F32)
        swapped = pltpu.roll(band, HEAD_DIM, axis=1)
        top = jnp.where(low_lanes, band if g == 0 else swapped, 0.0)
        bottom = jnp.where(low_lanes, 0.0, swapped if g == 0 else band)
        return jnp.concatenate([top, bottom], axis=0).astype(BF16)

    for j in range(nblk):
        rows = slice(j * BLOCK, (j + 1) * BLOCK)
        if j == 0:
            k_prev, v_prev = kp_ref[...], vp_ref[...]
            valid = band_ok & (kj >= first_lo)
        else:
            prev = slice((j - 1) * BLOCK, j * BLOCK)
            k_prev, v_prev = kc_ref[prev, :], vc_ref[prev, :]
            valid = band_ok
        k_band = jnp.concatenate([k_prev, kc_ref[rows, :]], axis=0)
        v_band = jnp.concatenate([v_prev, vc_ref[rows, :]], axis=0)
        for g in range(N_KV_HEADS):
            k2 = block_diag(k_band, g)
            v2 = block_diag(v_band, g)
            cols = [slice((g * pairs + p) * pw, (g * pairs + p + 1) * pw) for p in range(pairs)]
            q2 = jnp.concatenate([q_ref[rows, c] for c in cols], axis=0)
            s = lax.dot_general(q2, k2, (((1,), (1,)), ((), ())), preferred_element_type=F32) * scale
            probs = []
            for half in range(2):
                sh = jnp.where(valid, s[:, half * nkey:(half + 1) * nkey], -jnp.inf)
                sink = sink_ref[g * grp + half] * log2e
                for p in range(1, pairs):
                    sink = jnp.where(row_pair == p, sink_ref[g * grp + 2 * p + half] * log2e, sink)
                m = jnp.maximum(jnp.max(sh, axis=-1, keepdims=True), sink)
                e = jnp.exp2(sh - m)
                den = jnp.sum(e, axis=-1, keepdims=True) + jnp.exp2(sink - m)
                probs.append((e * (1.0 / den)).astype(BF16))
            out = jnp.dot(jnp.concatenate(probs, axis=-1), v2, preferred_element_type=F32)
            for p, c in enumerate(cols):
                o_ref[rows, c] = out[p * BLOCK:(p + 1) * BLOCK, :].astype(o_ref.dtype)


def _sliding_window_attention(proj, sinks, b, s, tq=512):
    t = b * s
    nq = s // tq
    nblk = tq // BLOCK
    sb = s // BLOCK
    cur = lambda col: (lambda bi, i: (bi * nq + i, col))
    prev = lambda col: (lambda bi, i: (bi * sb + jnp.maximum(i * nblk - 1, 0), col))
    return pl.pallas_call(
        functools.partial(_swa_kernel, nblk=nblk),
        grid=(b, nq),
        in_specs=[
            pl.BlockSpec(memory_space=pltpu.SMEM),
            pl.BlockSpec((tq, ATTN_WIDTH), cur(Q_OFF // ATTN_WIDTH)),
            pl.BlockSpec((tq, KV_WIDTH), cur(K_OFF // KV_WIDTH)),
            pl.BlockSpec((tq, KV_WIDTH), cur(V_OFF // KV_WIDTH)),
            pl.BlockSpec((BLOCK, KV_WIDTH), prev(K_OFF // KV_WIDTH)),
            pl.BlockSpec((BLOCK, KV_WIDTH), prev(V_OFF // KV_WIDTH)),
        ],
        out_specs=pl.BlockSpec((tq, ATTN_WIDTH), lambda bi, i: (bi * nq + i, 0)),
        out_shape=jax.ShapeDtypeStruct((t, ATTN_WIDTH), BF16),
        compiler_params=_cparams(("parallel", "parallel")),
        name="swa",
    )(sinks, proj, proj, proj, proj, proj)


def _mem_attn_kernel(q_ref, k_ref, v_ref, o_ref):
    s = lax.dot_general(q_ref[...], k_ref[...], (((1,), (1,)), ((), ())),
                        preferred_element_type=F32) * (MEM_HEAD_DIM ** -0.5)
    m = jnp.max(s, axis=-1, keepdims=True)
    e = jnp.exp(s - m)
    p = (e * (1.0 / jnp.sum(e, axis=-1, keepdims=True))).astype(BF16)
    o_ref[...] = jnp.dot(p, v_ref[...], preferred_element_type=F32).astype(o_ref.dtype)


def _memory_attention(proj, mem_kv, b, s):
    t = b * s
    qcol = QM_OFF // MEM_HEAD_DIM
    return pl.pallas_call(
        _mem_attn_kernel,
        grid=(b, MEM_HEADS),
        in_specs=[
            pl.BlockSpec((s, MEM_HEAD_DIM), lambda bi, h: (bi, qcol + h)),
            pl.BlockSpec((N_MEM, MEM_HEAD_DIM), lambda bi, h: (bi, h)),
            pl.BlockSpec((N_MEM, MEM_HEAD_DIM), lambda bi, h: (bi, MEM_HEADS + h)),
        ],
        out_specs=pl.BlockSpec((s, MEM_HEAD_DIM), lambda bi, h: (bi, h)),
        out_shape=jax.ShapeDtypeStruct((t, MEM_WIDTH), BF16),
        compiler_params=_cparams(("parallel", "parallel")),
        name="mem_attn",
    )(proj, mem_kv, mem_kv)


def _gelu_tanh(x):
    return 0.5 * x * (1.0 + jnp.tanh(math.sqrt(2.0 / math.pi) * (x + 0.044715 * (x * x * x))))


def _ssm_kernel(u_ref, wbr_ref, wbi_ref, lr_ref, li_ref, wcr_ref, wci_ref, d_ref, wglu_ref, o_ref,
                xr_ref, xi_ref, sr_ref, si_ref, *, nb, tc):
    @pl.when(pl.program_id(0) == 0)
    def _():
        sr_ref[...] = jnp.zeros_like(sr_ref)
        si_ref[...] = jnp.zeros_like(si_ref)

    u = u_ref[...]
    uw = SSM_WIDTH // SSM_HALVES
    xw = SSM_COMPLEX // SSM_HALVES
    for hf in range(SSM_HALVES):
        uh = u[:, hf * uw:(hf + 1) * uw]
        xr_ref[:, hf * xw:(hf + 1) * xw] = jnp.dot(uh, wbr_ref[hf], preferred_element_type=F32)
        xi_ref[:, hf * xw:(hf + 1) * xw] = jnp.dot(uh, wbi_ref[hf], preferred_element_type=F32)

    def scan_strip(st):
        cols = slice(st * SSM_SCAN_STRIP, (st + 1) * SSM_SCAN_STRIP)
        lr = jnp.broadcast_to(lr_ref[:, cols], (nb, SSM_SCAN_STRIP))
        li = jnp.broadcast_to(li_ref[:, cols], (nb, SSM_SCAN_STRIP))

        def step(t, carry):
            sr, si = carry
            r0 = pl.multiple_of(t * nb, nb)
            nr = lr * sr - li * si + xr_ref[pl.ds(r0, nb), cols]
            ni = lr * si + li * sr + xi_ref[pl.ds(r0, nb), cols]
            xr_ref[pl.ds(r0, nb), cols] = nr
            xi_ref[pl.ds(r0, nb), cols] = ni
            return nr, ni

        sr, si = lax.fori_loop(0, tc, step, (sr_ref[:, cols], si_ref[:, cols]), unroll=True)
        sr_ref[:, cols] = sr
        si_ref[:, cols] = si

    strips_per_half = xw // SSM_SCAN_STRIP
    ys = []
    for hf in range(SSM_HALVES):
        for st in range(hf * strips_per_half, (hf + 1) * strips_per_half):
            scan_strip(st)
        xs = slice(hf * xw, (hf + 1) * xw)
        yr = jnp.dot(xr_ref[:, xs].astype(BF16), wcr_ref[hf], preferred_element_type=F32)
        yi = jnp.dot(xi_ref[:, xs].astype(BF16), wci_ref[hf], preferred_element_type=F32)
        ys.append(yr - yi)
    y = jnp.concatenate(ys, axis=-1) + d_ref[...] * u.astype(F32)
    zg = jnp.dot(_gelu_tanh(y).astype(BF16), wglu_ref[...], preferred_element_type=F32)
    o_ref[...] = (zg[:, :SSM_WIDTH] * jax.nn.sigmoid(zg[:, SSM_WIDTH:])).astype(o_ref.dtype)


def _ssm_weights(lambda_re, lambda_im, log_dt, b_re, b_im, c_re, c_im, d_skip):
    g, p, h = N_SSM_GROUPS, SSM_STATE, SSM_GROUP
    gh = g // SSM_HALVES
    lam = lax.complex(lambda_re.astype(F32), lambda_im.astype(F32))
    dt = jnp.exp(log_dt.astype(F32))[:, None]
    lam_bar = jnp.exp(lam * dt)
    b_bar = ((lam_bar - 1.0) / lam)[..., None] * lax.complex(b_re.astype(F32), b_im.astype(F32))
    eye = jnp.eye(gh, dtype=F32)

    def blockdiag_in(m):
        m = m.reshape(SSM_HALVES, gh, p, h)
        return jnp.einsum('xgph,gk->xghkp', m, eye).reshape(SSM_HALVES, gh * h, gh * p).astype(BF16)

    def blockdiag_out(m):
        m = m.reshape(SSM_HALVES, gh, h, p)
        return jnp.einsum('xghp,gk->xgpkh', m, eye).reshape(SSM_HALVES, gh * p, gh * h).astype(BF16)

    return (blockdiag_in(jnp.real(b_bar)), blockdiag_in(jnp.imag(b_bar)),
            jnp.real(lam_bar).reshape(1, g * p), jnp.imag(lam_bar).reshape(1, g * p),
            blockdiag_out(c_re.astype(F32)), blockdiag_out(c_im.astype(F32)),
            d_skip.astype(F32).reshape(1, g * h))


def _ssm_branch(u_tm, ssm_w, w_glu, nb, s, tc=64):
    wbr, wbi, lr, li, wcr, wci, d = ssm_w
    rows = tc * nb
    full = lambda a: pl.BlockSpec(a.shape, lambda i: (0,) * a.ndim)
    return pl.pallas_call(
        functools.partial(_ssm_kernel, nb=nb, tc=tc),
        grid=(s // tc,),
        in_specs=[pl.BlockSpec((rows, SSM_WIDTH), lambda i: (i, 0)),
                  full(wbr), full(wbi), full(lr), full(li), full(wcr), full(wci), full(d), full(w_glu)],
        out_specs=pl.BlockSpec((rows, SSM_WIDTH), lambda i: (i, 0)),
        out_shape=jax.ShapeDtypeStruct((s * nb, SSM_WIDTH), BF16),
        scratch_shapes=[pltpu.VMEM((rows, SSM_COMPLEX), F32), pltpu.VMEM((rows, SSM_COMPLEX), F32),
                        pltpu.VMEM((nb, SSM_COMPLEX), F32), pltpu.VMEM((nb, SSM_COMPLEX), F32)],
        compiler_params=_cparams(("arbitrary",)),
        name="ssm",
    )(u_tm, wbr, wbi, lr, li, wcr, wci, d, w_glu)


def _merge_kernel(attn_ref, ssm_ref, mem_ref, g0_ref, g1_ref, g2_ref, h_ref, wb_ref, wo_ref, lg_ref, lb_ref,
                  o32_ref, slab_ref, *, tm):
    def gated(x_ref, g_ref, lo, hi):
        br = jnp.dot(x_ref[...], wb_ref[lo:hi, :], preferred_element_type=F32)
        return jax.nn.sigmoid(g_ref[...].astype(F32)) * br

    merged = gated(attn_ref, g0_ref, 0, ATTN_WIDTH)
    merged += gated(ssm_ref, g1_ref, ATTN_WIDTH, ATTN_WIDTH + SSM_WIDTH)
    merged += gated(mem_ref, g2_ref, ATTN_WIDTH + SSM_WIDTH, ATTN_WIDTH + SSM_WIDTH + MEM_WIDTH)
    mix = jnp.dot(merged.astype(BF16), wo_ref[...], preferred_element_type=F32)
    y = _layer_norm_rows(DEEPNORM_ALPHA * h_ref[...] + mix, lg_ref[...], lb_ref[...])
    o32_ref[...] = y
    _store_slabs(slab_ref, y, tm)


def _merge_out_ln(attn, ssm, mem, proj, h, w_branch, w_out, ln_g, ln_b, tm=256):
    t, d = h.shape
    row = lambda w, col=0: pl.BlockSpec((tm, w), lambda i: (i, col))
    const = lambda shape: pl.BlockSpec(shape, lambda i: (0, 0))
    return pl.pallas_call(
        functools.partial(_merge_kernel, tm=tm),
        grid=(t // tm,),
        in_specs=[row(ATTN_WIDTH), row(SSM_WIDTH), row(MEM_WIDTH),
                  row(d, G_OFF // d), row(d, G_OFF // d + 1), row(d, G_OFF // d + 2),
                  row(d), const(w_branch.shape), const(w_out.shape), const((1, d)), const((1, d))],
        out_specs=[row(d), pl.BlockSpec((tm * SLAB_ROWS, LANES), lambda i: (i, 0))],
        out_shape=[jax.ShapeDtypeStruct((t, d), F32), jax.ShapeDtypeStruct((t * SLAB_ROWS, LANES), U32)],
        compiler_params=_cparams(("parallel",)),
        name="merge_out_ln1",
    )(attn, ssm, mem, proj, proj, proj, h, w_branch, w_out, ln_g.reshape(1, d), ln_b.reshape(1, d))


def _split_bf16(x):
    hi = x.astype(BF16)
    return hi, (x - hi.astype(F32)).astype(BF16)


def _router_kernel(h_ref, wt_ref, b_ref, idx_ref, gate_ref, rank_ref, cnt_ref, carry_ref, *, tm):
    @pl.when(pl.program_id(0) == 0)
    def _():
        carry_ref[...] = jnp.zeros_like(carry_ref)

    nt = (((1,), (1,)), ((), ()))
    h_hi, h_lo = _split_bf16(h_ref[...])
    w_hi, w_lo = _split_bf16(wt_ref[...])
    lg = (lax.dot_general(w_hi, h_hi, nt, preferred_element_type=F32)
          + lax.dot_general(w_hi, h_lo, nt, preferred_element_type=F32)
          + lax.dot_general(w_lo, h_hi, nt, preferred_element_type=F32)) + b_ref[...]

    e_iota = lax.broadcasted_iota(I32, (N_EXPERTS, tm), 0)
    chosen = jnp.zeros((N_EXPERTS, tm), F32)
    vals, sels = [], []
    for k in range(TOP_K):
        m = jnp.max(lg, axis=0, keepdims=True)
        idx = jnp.min(jnp.where(lg == m, e_iota, N_EXPERTS), axis=0, keepdims=True)
        sel = e_iota == idx
        idx_ref[k:k + 1, :] = idx
        vals.append(m)
        sels.append(sel)
        chosen = jnp.where(sel, 1.0, chosen)
        lg = jnp.where(sel, -jnp.inf, lg)

    ex = [jnp.exp(v - vals[0]) for v in vals]
    inv = 1.0 / (ex[0] + ex[1] + ex[2] + ex[3])
    for k in range(TOP_K):
        gate_ref[k:k + 1, :] = ex[k] * inv

    r = lax.broadcasted_iota(I32, (tm, tm), 0)
    c = lax.broadcasted_iota(I32, (tm, tm), 1)
    before = jnp.where(r < c, 1.0, 0.0).astype(BF16)
    earlier = jnp.dot(chosen.astype(BF16), before, preferred_element_type=F32) + carry_ref[...]
    for k in range(TOP_K):
        rank_ref[k:k + 1, :] = jnp.sum(jnp.where(sels[k], earlier, 0.0), axis=0, keepdims=True).astype(I32)
    carry_ref[...] += jnp.sum(chosen, axis=1, keepdims=True)
    cnt_ref[...] = carry_ref[...].astype(I32)


def _router(h, w_router, b_router, tm=512):
    t, d = h.shape
    out = lambda dt: jax.ShapeDtypeStruct((TOP_K, t), dt)
    tok = pl.BlockSpec((TOP_K, tm), lambda i: (0, i))
    return pl.pallas_call(
        functools.partial(_router_kernel, tm=tm),
        grid=(t // tm,),
        in_specs=[pl.BlockSpec((tm, d), lambda i: (i, 0)),
                  pl.BlockSpec((N_EXPERTS, d), lambda i: (0, 0)),
                  pl.BlockSpec((N_EXPERTS, 1), lambda i: (0, 0))],
        out_specs=[tok, tok, tok, pl.BlockSpec((N_EXPERTS, 1), lambda i: (0, 0))],
        out_shape=[out(I32), out(F32), out(I32), jax.ShapeDtypeStruct((N_EXPERTS, 1), I32)],
        scratch_shapes=[pltpu.VMEM((N_EXPERTS, 1), F32)],
        compiler_params=_cparams(("arbitrary",)),
        name="router",
    )(h, w_router.T, b_router.reshape(N_EXPERTS, 1))


def _slab(ref, index):
    return ref.at[pl.ds(pl.multiple_of(index * SLAB_ROWS, SLAB_ROWS), SLAB_ROWS)]


ZERO_RUNS = tuple(1 << s for s in reversed(range(MOE_BLOCK.bit_length() - 1)))


def _dispatch_kernel(dest_ref, pad_at_ref, pad_len_ref, src_ref, x_ref, zero_ref, sem, zero_sem, *, tm):
    def zero_fill(act):
        def per_expert(e, c):
            at, n = pad_at_ref[e], pad_len_ref[e]
            for run in ZERO_RUNS:
                take = n & run

                @pl.when(take != 0)
                def _(at=at, run=run):
                    rows = run * SLAB_ROWS
                    dst = x_ref.at[pl.ds(pl.multiple_of(at * SLAB_ROWS, SLAB_ROWS), rows)]
                    act(pltpu.make_async_copy(zero_ref.at[pl.ds(0, rows)], dst, zero_sem))
                at = at + take
            return c
        lax.fori_loop(0, N_EXPERTS, per_expert, 0)

    @pl.when(pl.program_id(0) == 0)
    def _():
        zero_ref[...] = jnp.zeros_like(zero_ref)
        zero_fill(lambda cp: cp.start())
        zero_fill(lambda cp: cp.wait())

    def copy(t, dst):
        return pltpu.make_async_copy(_slab(src_ref, t), _slab(x_ref, dst), sem)

    def start(g, c):
        toks = [g * DMA_GROUP + j for j in range(DMA_GROUP)]
        dsts = [[dest_ref[t * TOP_K + k] for k in range(TOP_K)] for t in toks]
        for t, row in zip(toks, dsts):
            for k, dst in enumerate(row):
                copy(t, dst).start(priority=k % DMA_PRIORITIES)
        return c

    def wait(t, c):
        for k in range(TOP_K):
            copy(t, 0).wait()
        return c

    lax.fori_loop(0, tm // DMA_GROUP, start, 0)
    lax.fori_loop(0, tm, wait, 0, unroll=4)


def _dispatch(h_slabs, dest, pad_at, pad_len, n_rows, tm=1024):
    t = dest.shape[0] // TOP_K
    smem = pl.BlockSpec(memory_space=pltpu.SMEM)
    return pl.pallas_call(
        functools.partial(_dispatch_kernel, tm=tm),
        grid=(t // tm,),
        in_specs=[pl.BlockSpec((TOP_K * tm,), lambda i: (i,), memory_space=pltpu.SMEM), smem, smem,
                  pl.BlockSpec((tm * SLAB_ROWS, LANES), lambda i: (i, 0))],
        out_specs=pl.BlockSpec(memory_space=pl.ANY),
        out_shape=jax.ShapeDtypeStruct((n_rows * SLAB_ROWS, LANES), U32),
        scratch_shapes=[pltpu.VMEM((ZERO_RUNS[0] * SLAB_ROWS, LANES), U32),
                        pltpu.SemaphoreType.DMA(()), pltpu.SemaphoreType.DMA(())],
        compiler_params=_cparams(("arbitrary",)),
        name="moe_dispatch",
    )(dest, pad_at, pad_len, h_slabs)


UP_CHUNK = 256
UP_HALF = UP_CHUNK // 2


PREP_PARTS = 8


def _in_proj_kernel(a_ref, w_ref, wup_ref, wdn_ref, o_ref, oup_ref, odn_ref):
    o_ref[...] = jnp.dot(a_ref[...], w_ref[...], preferred_element_type=F32).astype(o_ref.dtype)
    r = lax.broadcasted_iota(I32, (UP_CHUNK, UP_CHUNK), 0)
    c = lax.broadcasted_iota(I32, (UP_CHUNK, UP_CHUNK), 1)
    src = jnp.where(c < UP_HALF, 2 * c, 2 * (c - UP_HALF) + 1)
    perm = jnp.where(r == src, 1.0, 0.0).astype(BF16)
    oup_ref[0] = jnp.dot(wup_ref[0].astype(BF16), perm, preferred_element_type=F32).astype(BF16)
    odn_ref[0] = wdn_ref[0].astype(BF16)


def _in_proj_and_moe_weights(a, w, w_up, w_down, layer, tm_max=1024, tn=768):
    m, k = a.shape
    n = w.shape[1]
    _, e, d, n_up = w_up.shape
    f = w_down.shape[2]
    nj = n // tn
    pieces = e * PREP_PARTS
    tm = tm_max
    while (m // tm) * nj < pieces:
        tm //= 2
    assert n_up == PREP_PARTS * UP_CHUNK and f % PREP_PARTS == 0 and m % tm == 0
    fp = f // PREP_PARTS

    def piece(i, j):
        u = jnp.minimum(i * nj + j, pieces - 1)
        return u // PREP_PARTS, u % PREP_PARTS

    def up_in(i, j):
        ex, part = piece(i, j)
        return layer, ex, 0, part

    def up_out(i, j):
        ex, part = piece(i, j)
        return ex, 0, part

    def down_in(i, j):
        ex, part = piece(i, j)
        return layer, ex, part, 0

    def down_out(i, j):
        ex, part = piece(i, j)
        return ex, part, 0

    return pl.pallas_call(
        _in_proj_kernel,
        grid=(m // tm, nj),
        in_specs=[pl.BlockSpec((tm, k), lambda i, j: (i, 0)), pl.BlockSpec((k, tn), lambda i, j: (0, j)),
                  pl.BlockSpec((None, 1, d, UP_CHUNK), up_in), pl.BlockSpec((None, 1, fp, d), down_in)],
        out_specs=[pl.BlockSpec((tm, tn), lambda i, j: (i, j)),
                   pl.BlockSpec((1, d, UP_CHUNK), up_out), pl.BlockSpec((1, fp, d), down_out)],
        out_shape=[jax.ShapeDtypeStruct((m, n), BF16), jax.ShapeDtypeStruct((e, d, n_up), BF16),
                   jax.ShapeDtypeStruct((e, f, d), BF16)],
        compiler_params=_cparams(("arbitrary", "arbitrary")),
        name="in_proj",
    )(a, w, w_up, w_down)


def _pair_split_bias(b_up):
    e, n = b_up.shape
    return b_up.reshape(e, n // UP_CHUNK, UP_HALF, 2).transpose(0, 1, 3, 2).reshape(e, 1, n).astype(F32)


def _expert_kernel(be_ref, nu_ref, x_ref, wu_ref, bu_ref, wd_ref, bd_ref, o_ref):
    del be_ref
    i = pl.program_id(0)

    @pl.when(i < nu_ref[0])
    def _():
        los, his = _load_slabs(x_ref, 0, MOE_BLOCK)
        x = jnp.concatenate([p.astype(BF16) for p in los + his], axis=-1)
        up = jnp.dot(x, wu_ref[0], preferred_element_type=F32) + bu_ref[0]
        acts = []
        for ch in range(up.shape[-1] // UP_CHUNK):
            x_glu = jnp.minimum(up[:, ch * UP_CHUNK:ch * UP_CHUNK + UP_HALF], SWIGLU_LIMIT)
            x_lin = jnp.clip(up[:, ch * UP_CHUNK + UP_HALF:(ch + 1) * UP_CHUNK], -SWIGLU_LIMIT, SWIGLU_LIMIT)
            acts.append((x_glu * jax.nn.sigmoid(SWIGLU_ALPHA * x_glu) * (x_lin + 1.0)).astype(BF16))
        act = jnp.concatenate(acts, axis=-1)
        out = jnp.dot(act, wd_ref[0], preferred_element_type=F32) + bd_ref[0]
        _store_slabs(o_ref, out, MOE_BLOCK)

    @pl.when(i >= nu_ref[0])
    def _():
        o_ref[...] = jnp.zeros_like(o_ref)


def _experts(x_pad, block_e, n_used, wu, bu, wd, bd):
    n_blocks = x_pad.shape[0] // (MOE_BLOCK * SLAB_ROWS)
    row = pl.BlockSpec((MOE_BLOCK * SLAB_ROWS, LANES), lambda i, be, nu: (i, 0))
    used_row = pl.BlockSpec((MOE_BLOCK * SLAB_ROWS, LANES), lambda i, be, nu: (jnp.minimum(i, nu[0] - 1), 0))
    per_e = lambda a: pl.BlockSpec((1,) + a.shape[1:], lambda i, be, nu: (be[i], 0, 0))
    return pl.pallas_call(
        _expert_kernel,
        grid_spec=pltpu.PrefetchScalarGridSpec(
            num_scalar_prefetch=2,
            grid=(n_blocks,),
            in_specs=[used_row, per_e(wu), per_e(bu), per_e(wd), per_e(bd)],
            out_specs=row,
        ),
        out_shape=jax.ShapeDtypeStruct(x_pad.shape, U32),
        compiler_params=_cparams(("arbitrary",)),
        name="moe_experts",
    )(block_e, n_used, x_pad, wu, bu, wd, bd)


def _combine_kernel(dest_ref, next_dest_ref, y_ref, gate_ref, h_ref, lg_ref, lb_ref, o32_ref, o16_ref,
                    buf0_ref, buf1_ref, sems, *, tm, n_tiles):
    i = pl.program_id(0)
    bufs = (buf0_ref, buf1_ref)

    def copy(src, t, k, sl):
        return pltpu.make_async_copy(_slab(y_ref, src), _slab(bufs[sl], k * tm + t), sems.at[sl])

    def start_group(d_ref, g, sl):
        toks = [g * DMA_GROUP + j for j in range(DMA_GROUP)]
        srcs = [[d_ref[t * TOP_K + k] for k in range(TOP_K)] for t in toks]
        for t, row in zip(toks, srcs):
            for k, src in enumerate(row):
                copy(src, t, k, sl).start(priority=k % DMA_PRIORITIES)

    def wait_tile(sl):
        def wait(t, c):
            for k in range(TOP_K):
                copy(0, t, k, sl).wait()
            return c
        lax.fori_loop(0, tm, wait, 0, unroll=4)

    def combine(buf_ref):
        g = gate_ref[...]
        gk = [jnp.broadcast_to(g[:, k:k + 1], (tm, LANES)) for k in range(TOP_K)]
        lo_acc, hi_acc = None, None
        for k in range(TOP_K):
            los, his = _load_slabs(buf_ref, k * tm * SLAB_ROWS, tm)
            los = [p * gk[k] for p in los]
            his = [p * gk[k] for p in his]
            lo_acc = los if lo_acc is None else [a + p for a, p in zip(lo_acc, los)]
            hi_acc = his if hi_acc is None else [a + p for a, p in zip(hi_acc, his)]
        ffn = jnp.concatenate(lo_acc + hi_acc, axis=-1)
        y = _layer_norm_rows(DEEPNORM_ALPHA * h_ref[...] + ffn, lg_ref[...], lb_ref[...])
        o32_ref[...] = y
        o16_ref[...] = y.astype(BF16)

    @pl.when(i == 0)
    def _():
        def first(g, c):
            start_group(dest_ref, g, 0)
            return c
        lax.fori_loop(0, tm // DMA_GROUP, first, 0)

    def step(sl):
        wait_tile(sl)
        for g in range(tm // DMA_GROUP):
            start_group(next_dest_ref, g, 1 - sl)
        combine(bufs[sl])

    for sl in range(2):
        pl.when(i % 2 == sl)(functools.partial(step, sl))

    @pl.when(i == n_tiles - 1)
    def _():
        wait_tile(1 - (n_tiles - 1) % 2)


def _combine_ln(out_pad, dest, gate_t, h, ln_g, ln_b, tm=256):
    t, d = h.shape
    n = t // tm
    row = pl.BlockSpec((tm, d), lambda i: (i, 0))
    vec = pl.BlockSpec((1, d), lambda i: (0, 0))
    return pl.pallas_call(
        functools.partial(_combine_kernel, tm=tm, n_tiles=n),
        grid=(n,),
        in_specs=[pl.BlockSpec((TOP_K * tm,), lambda i: (i,), memory_space=pltpu.SMEM),
                  pl.BlockSpec((TOP_K * tm,), lambda i: (jnp.minimum(i + 1, n - 1),), memory_space=pltpu.SMEM),
                  pl.BlockSpec(memory_space=pl.ANY),
                  pl.BlockSpec((tm, TOP_K), lambda i: (i, 0)),
                  row, vec, vec],
        out_specs=[row, row],
        out_shape=[jax.ShapeDtypeStruct((t, d), F32), jax.ShapeDtypeStruct((t, d), BF16)],
        scratch_shapes=[pltpu.VMEM((TOP_K * tm * SLAB_ROWS, LANES), U32),
                        pltpu.VMEM((TOP_K * tm * SLAB_ROWS, LANES), U32), pltpu.SemaphoreType.DMA((2,))],
        compiler_params=_cparams(("arbitrary",)),
        name="moe_combine_ln2",
    )(dest, dest, out_pad, gate_t, h, ln_g.reshape(1, d), ln_b.reshape(1, d))


def _moe_layout(idx, rank, counts, n_blocks):
    counts = counts.reshape(N_EXPERTS)
    padded = (counts + MOE_BLOCK - 1) // MOE_BLOCK * MOE_BLOCK
    pad_end = jnp.cumsum(padded)
    pad_start = pad_end - padded
    experts = jnp.arange(N_EXPERTS, dtype=I32)[:, None, None]
    dest = rank + jnp.sum(jnp.where(idx[None] == experts, pad_start[:, None, None], 0), axis=0)
    block_rows = jnp.arange(n_blocks, dtype=I32) * MOE_BLOCK
    block_e = jnp.minimum(jnp.sum(block_rows[:, None] >= pad_end[None, :], axis=1), N_EXPERTS - 1).astype(I32)
    n_used = (pad_end[-1:] // MOE_BLOCK).astype(I32)
    pad_at = (pad_start + counts).astype(I32)
    pad_len = (padded - counts).astype(I32)
    return dest.astype(I32).T.reshape(-1), block_e, n_used, pad_at, pad_len


def _permute_w_in(w):
    pieces = [w[:, QM_END:], w[:, :Q_END], w[:, V_END:U_END], w[:, U_END:QM_END], w[:, Q_END:K_END], w[:, K_END:V_END]]
    return jnp.concatenate(pieces, axis=1).astype(BF16)


def kernel(x, mem, ln_in_g, ln_in_b, w_in, attn_sinks, ssm_lambda_re, ssm_lambda_im, ssm_log_dt, ssm_b_re, ssm_b_im, ssm_c_re, ssm_c_im, ssm_d, w_glu, w_mem_kv, w_branch, w_out, ln1_g, ln1_b, w_router, b_router, w_up, b_up, w_down, b_down, ln2_g, ln2_b):
    b, s, d = x.shape
    t = b * s
    n_blocks = -(-(t * TOP_K) // MOE_BLOCK) + N_EXPERTS
    mem16 = mem.reshape(b * N_MEM, d).astype(BF16)

    h32, h16 = _layer_norm_in(x.reshape(t, d), ln_in_g, ln_in_b)
    for l in range(DEPTH):
        proj, wu16, wd16 = _in_proj_and_moe_weights(h16, _permute_w_in(w_in[l]), w_up, w_down, l)
        mem_kv = _matmul(mem16, w_mem_kv[l].astype(BF16), tm=1024, tn=512, name="mem_kv")

        attn = _sliding_window_attention(proj, attn_sinks[l].astype(F32), b, s)
        mem_out = _memory_attention(proj, mem_kv, b, s)
        ssm_w = _ssm_weights(ssm_lambda_re[l], ssm_lambda_im[l], ssm_log_dt[l], ssm_b_re[l], ssm_b_im[l],
                             ssm_c_re[l], ssm_c_im[l], ssm_d[l])
        u_tm = proj[:, U_OFF:U_OFF + SSM_WIDTH].reshape(b, s, SSM_WIDTH).transpose(1, 0, 2).reshape(t, SSM_WIDTH)
        ssm_tm = _ssm_branch(u_tm, ssm_w, w_glu[l].astype(BF16), b, s)
        ssm_out = ssm_tm.reshape(s, b, SSM_WIDTH).transpose(1, 0, 2).reshape(t, SSM_WIDTH)

        h1, h1_slabs = _merge_out_ln(attn, ssm_out, mem_out, proj, h32, w_branch[l].astype(BF16),
                                     w_out[l].astype(BF16), ln1_g[l], ln1_b[l])

        idx, gate, rank, counts = _router(h1, w_router[l], b_router[l])
        dest, block_e, n_used, pad_at, pad_len = _moe_layout(idx, rank, counts, n_blocks)
        x_pad = _dispatch(h1_slabs, dest, pad_at, pad_len, n_blocks * MOE_BLOCK)
        out_pad = _experts(x_pad, block_e, n_used, wu16, _pair_split_bias(b_up[l]), wd16,
                           b_down[l][:, None, :].astype(F32))
        h32, h16 = _combine_ln(out_pad, dest, gate.T, h1, ln2_g[l], ln2_b[l])
    return h32.reshape(b, s, d)
```

```python
import functools
import math

import jax
import jax.numpy as jnp
from jax import lax
from jax.experimental import pallas as pl
from jax.experimental.pallas import tpu as pltpu

F32 = jnp.float32
BF16 = jnp.bfloat16
I32 = jnp.int32

D_MODEL = 2048
DEPTH = 2
N_HEADS = 16
N_KV_HEADS = 2
HEAD_DIM = 64
WINDOW = 128
BLOCK = 128
ATTN_WIDTH = N_HEADS * HEAD_DIM
KV_WIDTH = N_KV_HEADS * HEAD_DIM
SSM_WIDTH = D_MODEL // 4
SSM_GROUP = 16
N_SSM_GROUPS = SSM_WIDTH // SSM_GROUP
SSM_STATE = 64
N_MEM = 256
MEM_HEADS = 4
MEM_HEAD_DIM = 128
MEM_WIDTH = MEM_HEADS * MEM_HEAD_DIM
N_BRANCHES = 3
Q_END = ATTN_WIDTH
K_END = Q_END + KV_WIDTH
V_END = K_END + KV_WIDTH
U_END = V_END + SSM_WIDTH
QM_END = U_END + MEM_WIDTH
IN_WIDTH = QM_END + N_BRANCHES * D_MODEL
N_EXPERTS = 32
TOP_K = 4
D_FF = D_MODEL // 2
MOE_BLOCK = 512
SWIGLU_ALPHA = 1.702
SWIGLU_LIMIT = 7.0
LN_EPS = 1e-5
DEEPNORM_ALPHA = (2.0 * DEPTH) ** 0.25

G_OFF = 0
Q_OFF = N_BRANCHES * D_MODEL
U_OFF = Q_OFF + ATTN_WIDTH
QM_OFF = U_OFF + SSM_WIDTH
K_OFF = QM_OFF + MEM_WIDTH
V_OFF = K_OFF + KV_WIDTH

SSM_COMPLEX = N_SSM_GROUPS * SSM_STATE
SSM_HALVES = 2
SSM_SCAN_STRIP = 512

VMEM_LIMIT = 60 * 1024 * 1024


def _cparams(sem):
    return pltpu.CompilerParams(dimension_semantics=sem, vmem_limit_bytes=VMEM_LIMIT)


def _layer_norm_rows(x, g, b):
    mu = jnp.mean(x, axis=-1, keepdims=True)
    xc = x - mu
    var = jnp.mean(xc * xc, axis=-1, keepdims=True)
    return xc * lax.rsqrt(var + LN_EPS) * g + b


LANES = 128
SLAB_ROWS = D_MODEL // 2 // LANES
DMA_GROUP = 4
DMA_PRIORITIES = 2
U32 = jnp.uint32
HIGH_HALF = 0xFFFF0000


def _pack_pair(lo, hi):
    lo = lax.bitcast_convert_type(lo.astype(BF16).astype(F32), U32) >> 16
    hi = lax.bitcast_convert_type(hi.astype(BF16).astype(F32), U32) & U32(HIGH_HALF)
    return lo | hi


def _unpack_pair(w):
    return lax.bitcast_convert_type(w << 16, F32), lax.bitcast_convert_type(w & U32(HIGH_HALF), F32)


def _store_slabs(slab_ref, x, n, first_slab=0):
    half = D_MODEL // 2
    for c in range(SLAB_ROWS):
        lo = x[:, c * LANES:(c + 1) * LANES]
        hi = x[:, half + c * LANES:half + (c + 1) * LANES]
        slab_ref[pl.ds(first_slab * SLAB_ROWS + c, n, stride=SLAB_ROWS), :] = _pack_pair(lo, hi)


def _load_slabs(slab_ref, first_row, n):
    los, his = [], []
    for c in range(SLAB_ROWS):
        lo, hi = _unpack_pair(slab_ref[pl.ds(first_row + c, n, stride=SLAB_ROWS), :])
        los.append(lo)
        his.append(hi)
    return los, his


def _ln_kernel(x_ref, g_ref, b_ref, o32_ref, o16_ref):
    y = _layer_norm_rows(x_ref[...], g_ref[...], b_ref[...])
    o32_ref[...] = y
    o16_ref[...] = y.astype(BF16)


def _layer_norm_in(x, g, b, tm=512):
    t, d = x.shape
    row = pl.BlockSpec((tm, d), lambda i: (i, 0))
    vec = pl.BlockSpec((1, d), lambda i: (0, 0))
    return pl.pallas_call(
        _ln_kernel,
        grid=(t // tm,),
        in_specs=[row, vec, vec],
        out_specs=[row, row],
        out_shape=[jax.ShapeDtypeStruct((t, d), F32), jax.ShapeDtypeStruct((t, d), BF16)],
        compiler_params=_cparams(("parallel",)),
        name="ln_in",
    )(x, g.reshape(1, d), b.reshape(1, d))


def _mm_kernel(a_ref, w_ref, o_ref):
    o_ref[...] = jnp.dot(a_ref[...], w_ref[...], preferred_element_type=F32).astype(o_ref.dtype)


def _matmul(a, w, tm, tn, name):
    m, k = a.shape
    n = w.shape[1]
    return pl.pallas_call(
        _mm_kernel,
        grid=(m // tm, n // tn),
        in_specs=[pl.BlockSpec((tm, k), lambda i, j: (i, 0)), pl.BlockSpec((k, tn), lambda i, j: (0, j))],
        out_specs=pl.BlockSpec((tm, tn), lambda i, j: (i, j)),
        out_shape=jax.ShapeDtypeStruct((m, n), BF16),
        compiler_params=_cparams(("parallel", "parallel")),
        name=name,
    )(a, w)


def _swa_kernel(sink_ref, q_ref, kc_ref, vc_ref, kp_ref, vp_ref, o_ref, *, nblk):
    i = pl.program_id(1)
    grp = N_HEADS // N_KV_HEADS
    pairs = grp // 2
    pw = 2 * HEAD_DIM
    nrow = pairs * BLOCK
    nkey = 2 * BLOCK
    log2e = math.log2(math.e)
    scale = HEAD_DIM ** -0.5 * log2e
    row = lax.broadcasted_iota(I32, (nrow, nkey), 0)
    kj = lax.broadcasted_iota(I32, (nrow, nkey), 1)
    rel = (row & (BLOCK - 1)) + BLOCK - kj
    band_ok = (rel >= 0) & (rel < WINDOW)
    row_pair = lax.broadcasted_iota(I32, (nrow, 1), 0) // BLOCK
    low_lanes = lax.broadcasted_iota(I32, (nkey, pw), 1) < HEAD_DIM
    first_lo = jnp.where(i > 0, 0, BLOCK)

    def block_diag(band, g):
        band = band.astype(F32)
        swapped = pltpu.roll(band, HEAD_DIM, axis=1)
        top = jnp.where(low_lanes, band if g == 0 else swapped, 0.0)
        bottom = jnp.where(low_lanes, 0.0, swapped if g == 0 else band)
        return jnp.concatenate([top, bottom], axis=0).astype(BF16)

    for j in range(nblk):
        rows = slice(j * BLOCK, (j + 1) * BLOCK)
        if j == 0:
            k_prev, v_prev = kp_ref[...], vp_ref[...]
            valid = band_ok & (kj >= first_lo)
        else:
            prev = slice((j - 1) * BLOCK, j * BLOCK)
            k_prev, v_prev = kc_ref[prev, :], vc_ref[prev, :]
            valid = band_ok
        k_band = jnp.concatenate([k_prev, kc_ref[rows, :]], axis=0)
        v_band = jnp.concatenate([v_prev, vc_ref[rows, :]], axis=0)
        for g in range(N_KV_HEADS):
            k2 = block_diag(k_band, g)
            v2 = block_diag(v_band, g)
            cols = [slice((g * pairs + p) * pw, (g * pairs + p + 1) * pw) for p in range(pairs)]
            q2 = jnp.concatenate([q_ref[rows, c] for c in cols], axis=0)
            s = lax.dot_general(q2, k2, (((1,), (1,)), ((), ())), preferred_element_type=F32) * scale
            probs = []
            for half in range(2):
                sh = jnp.where(valid, s[:, half * nkey:(half + 1) * nkey], -jnp.inf)
                sink = sink_ref[g * grp + half] * log2e
                for p in range(1, pairs):
                    sink = jnp.where(row_pair == p, sink_ref[g * grp + 2 * p + half] * log2e, sink)
                m = jnp.maximum(jnp.max(sh, axis=-1, keepdims=True), sink)
                e = jnp.exp2(sh - m)
                den = jnp.sum(e, axis=-1, keepdims=True) + jnp.exp2(sink - m)
                probs.append((e * (1.0 / den)).astype(BF16))
            out = jnp.dot(jnp.concatenate(probs, axis=-1), v2, preferred_element_type=F32)
            for p, c in enumerate(cols):
                o_ref[rows, c] = out[p * BLOCK:(p + 1) * BLOCK, :].astype(o_ref.dtype)


def _sliding_window_attention(proj, sinks, b, s, tq=512):
    t = b * s
    nq = s // tq
    nblk = tq // BLOCK
    sb = s // BLOCK
    cur = lambda col: (lambda bi, i: (bi * nq + i, col))
    prev = lambda col: (lambda bi, i: (bi * sb + jnp.maximum(i * nblk - 1, 0), col))
    return pl.pallas_call(
        functools.partial(_swa_kernel, nblk=nblk),
        grid=(b, nq),
        in_specs=[
            pl.BlockSpec(memory_space=pltpu.SMEM),
            pl.BlockSpec((tq, ATTN_WIDTH), cur(Q_OFF // ATTN_WIDTH)),
            pl.BlockSpec((tq, KV_WIDTH), cur(K_OFF // KV_WIDTH)),
            pl.BlockSpec((tq, KV_WIDTH), cur(V_OFF // KV_WIDTH)),
            pl.BlockSpec((BLOCK, KV_WIDTH), prev(K_OFF // KV_WIDTH)),
            pl.BlockSpec((BLOCK, KV_WIDTH), prev(V_OFF // KV_WIDTH)),
        ],
        out_specs=pl.BlockSpec((tq, ATTN_WIDTH), lambda bi, i: (bi * nq + i, 0)),
        out_shape=jax.ShapeDtypeStruct((t, ATTN_WIDTH), BF16),
        compiler_params=_cparams(("parallel", "parallel")),
        name="swa",
    )(sinks, proj, proj, proj, proj, proj)


def _mem_attn_kernel(q_ref, k_ref, v_ref, o_ref):
    s = lax.dot_general(q_ref[...], k_ref[...], (((1,), (1,)), ((), ())),
                        preferred_element_type=F32) * (MEM_HEAD_DIM ** -0.5)
    m = jnp.max(s, axis=-1, keepdims=True)
    e = jnp.exp(s - m)
    p = (e * (1.0 / jnp.sum(e, axis=-1, keepdims=True))).astype(BF16)
    o_ref[...] = jnp.dot(p, v_ref[...], preferred_element_type=F32).astype(o_ref.dtype)


def _memory_attention(proj, mem_kv, b, s):
    t = b * s
    qcol = QM_OFF // MEM_HEAD_DIM
    return pl.pallas_call(
        _mem_attn_kernel,
        grid=(b, MEM_HEADS),
        in_specs=[
            pl.BlockSpec((s, MEM_HEAD_DIM), lambda bi, h: (bi, qcol + h)),
            pl.BlockSpec((N_MEM, MEM_HEAD_DIM), lambda bi, h: (bi, h)),
            pl.BlockSpec((N_MEM, MEM_HEAD_DIM), lambda bi, h: (bi, MEM_HEADS + h)),
        ],
        out_specs=pl.BlockSpec((s, MEM_HEAD_DIM), lambda bi, h: (bi, h)),
        out_shape=jax.ShapeDtypeStruct((t, MEM_WIDTH), BF16),
        compiler_params=_cparams(("parallel", "parallel")),
        name="mem_attn",
    )(proj, mem_kv, mem_kv)


def _gelu_tanh(x):
    return 0.5 * x * (1.0 + jnp.tanh(math.sqrt(2.0 / math.pi) * (x + 0.044715 * (x * x * x))))


def _ssm_kernel(u_ref, wbr_ref, wbi_ref, lr_ref, li_ref, wcr_ref, wci_ref, d_ref, wglu_ref, o_ref,
                xr_ref, xi_ref, sr_ref, si_ref, *, nb, tc):
    @pl.when(pl.program_id(0) == 0)
    def _():
        sr_ref[...] = jnp.zeros_like(sr_ref)
        si_ref[...] = jnp.zeros_like(si_ref)

    u = u_ref[...]
    uw = SSM_WIDTH // SSM_HALVES
    xw = SSM_COMPLEX // SSM_HALVES
    for hf in range(SSM_HALVES):
        uh = u[:, hf * uw:(hf + 1) * uw]
        xr_ref[:, hf * xw:(hf + 1) * xw] = jnp.dot(uh, wbr_ref[hf], preferred_element_type=F32)
        xi_ref[:, hf * xw:(hf + 1) * xw] = jnp.dot(uh, wbi_ref[hf], preferred_element_type=F32)

    def scan_strip(st):
        cols = slice(st * SSM_SCAN_STRIP, (st + 1) * SSM_SCAN_STRIP)
        lr = jnp.broadcast_to(lr_ref[:, cols], (nb, SSM_SCAN_STRIP))
        li = jnp.broadcast_to(li_ref[:, cols], (nb, SSM_SCAN_STRIP))

        def step(t, carry):
            sr, si = carry
            r0 = pl.multiple_of(t * nb, nb)
            nr = lr * sr - li * si + xr_ref[pl.ds(r0, nb), cols]
            ni = lr * si + li * sr + xi_ref[pl.ds(r0, nb), cols]
            xr_ref[pl.ds(r0, nb), cols] = nr
            xi_ref[pl.ds(r0, nb), cols] = ni
            return nr, ni

        sr, si = lax.fori_loop(0, tc, step, (sr_ref[:, cols], si_ref[:, cols]), unroll=True)
        sr_ref[:, cols] = sr
        si_ref[:, cols] = si

    strips_per_half = xw // SSM_SCAN_STRIP
    ys = []
    for hf in range(SSM_HALVES):
        for st in range(hf * strips_per_half, (hf + 1) * strips_per_half):
            scan_strip(st)
        xs = slice(hf * xw, (hf + 1) * xw)
        yr = jnp.dot(xr_ref[:, xs].astype(BF16), wcr_ref[hf], preferred_element_type=F32)
        yi = jnp.dot(xi_ref[:, xs].astype(BF16), wci_ref[hf], preferred_element_type=F32)
        ys.append(yr - yi)
    y = jnp.concatenate(ys, axis=-1) + d_ref[...] * u.astype(F32)
    zg = jnp.dot(_gelu_tanh(y).astype(BF16), wglu_ref[...], preferred_element_type=F32)
    o_ref[...] = (zg[:, :SSM_WIDTH] * jax.nn.sigmoid(zg[:, SSM_WIDTH:])).astype(o_ref.dtype)


def _ssm_weights(lambda_re, lambda_im, log_dt, b_re, b_im, c_re, c_im, d_skip):
    g, p, h = N_SSM_GROUPS, SSM_STATE, SSM_GROUP
    gh = g // SSM_HALVES
    lam = lax.complex(lambda_re.astype(F32), lambda_im.astype(F32))
    dt = jnp.exp(log_dt.astype(F32))[:, None]
    lam_bar = jnp.exp(lam * dt)
    b_bar = ((lam_bar - 1.0) / lam)[..., None] * lax.complex(b_re.astype(F32), b_im.astype(F32))
    eye = jnp.eye(gh, dtype=F32)

    def blockdiag_in(m):
        m = m.reshape(SSM_HALVES, gh, p, h)
        return jnp.einsum('xgph,gk->xghkp', m, eye).reshape(SSM_HALVES, gh * h, gh * p).astype(BF16)

    def blockdiag_out(m):
        m = m.reshape(SSM_HALVES, gh, h, p)
        return jnp.einsum('xghp,gk->xgpkh', m, eye).reshape(SSM_HALVES, gh * p, gh * h).astype(BF16)

    return (blockdiag_in(jnp.real(b_bar)), blockdiag_in(jnp.imag(b_bar)),
            jnp.real(lam_bar).reshape(1, g * p), jnp.imag(lam_bar).reshape(1, g * p),
            blockdiag_out(c_re.astype(F32)), blockdiag_out(c_im.astype(F32)),
            d_skip.astype(F32).reshape(1, g * h))


def _ssm_branch(u_tm, ssm_w, w_glu, nb, s, tc=64):
    wbr, wbi, lr, li, wcr, wci, d = ssm_w
    rows = tc * nb
    full = lambda a: pl.BlockSpec(a.shape, lambda i: (0,) * a.ndim)
    return pl.pallas_call(
        functools.partial(_ssm_kernel, nb=nb, tc=tc),
        grid=(s // tc,),
        in_specs=[pl.BlockSpec((rows, SSM_WIDTH), lambda i: (i, 0)),
                  full(wbr), full(wbi), full(lr), full(li), full(wcr), full(wci), full(d), full(w_glu)],
        out_specs=pl.BlockSpec((rows, SSM_WIDTH), lambda i: (i, 0)),
        out_shape=jax.ShapeDtypeStruct((s * nb, SSM_WIDTH), BF16),
        scratch_shapes=[pltpu.VMEM((rows, SSM_COMPLEX), F32), pltpu.VMEM((rows, SSM_COMPLEX), F32),
                        pltpu.VMEM((nb, SSM_COMPLEX), F32), pltpu.VMEM((nb, SSM_COMPLEX), F32)],
        compiler_params=_cparams(("arbitrary",)),
        name="ssm",
    )(u_tm, wbr, wbi, lr, li, wcr, wci, d, w_glu)


def _merge_kernel(attn_ref, ssm_ref, mem_ref, g0_ref, g1_ref, g2_ref, h_ref, wb_ref, wo_ref, lg_ref, lb_ref,
                  o32_ref, slab_ref, *, tm):
    def gated(x_ref, g_ref, lo, hi):
        br = jnp.dot(x_ref[...], wb_ref[lo:hi, :], preferred_element_type=F32)
        return jax.nn.sigmoid(g_ref[...].astype(F32)) * br

    merged = gated(attn_ref, g0_ref, 0, ATTN_WIDTH)
    merged += gated(ssm_ref, g1_ref, ATTN_WIDTH, ATTN_WIDTH + SSM_WIDTH)
    merged += gated(mem_ref, g2_ref, ATTN_WIDTH + SSM_WIDTH, ATTN_WIDTH + SSM_WIDTH + MEM_WIDTH)
    mix = jnp.dot(merged.astype(BF16), wo_ref[...], preferred_element_type=F32)
    y = _layer_norm_rows(DEEPNORM_ALPHA * h_ref[...] + mix, lg_ref[...], lb_ref[...])
    o32_ref[...] = y
    _store_slabs(slab_ref, y, tm)


def _merge_out_ln(attn, ssm, mem, proj, h, w_branch, w_out, ln_g, ln_b, tm=256):
    t, d = h.shape
    row = lambda w, col=0: pl.BlockSpec((tm, w), lambda i: (i, col))
    const = lambda shape: pl.BlockSpec(shape, lambda i: (0, 0))
    return pl.pallas_call(
        functools.partial(_merge_kernel, tm=tm),
        grid=(t // tm,),
        in_specs=[row(ATTN_WIDTH), row(SSM_WIDTH), row(MEM_WIDTH),
                  row(d, G_OFF // d), row(d, G_OFF // d + 1), row(d, G_OFF // d + 2),
                  row(d), const(w_branch.shape), const(w_out.shape), const((1, d)), const((1, d))],
        out_specs=[row(d), pl.BlockSpec((tm * SLAB_ROWS, LANES), lambda i: (i, 0))],
        out_shape=[jax.ShapeDtypeStruct((t, d), F32), jax.ShapeDtypeStruct((t * SLAB_ROWS, LANES), U32)],
        compiler_params=_cparams(("parallel",)),
        name="merge_out_ln1",
    )(attn, ssm, mem, proj, proj, proj, h, w_branch, w_out, ln_g.reshape(1, d), ln_b.reshape(1, d))


def _split_bf16(x):
    hi = x.astype(BF16)
    return hi, (x - hi.astype(F32)).astype(BF16)


def _router_kernel(h_ref, wt_ref, b_ref, idx_ref, gate_ref, rank_ref, cnt_ref, carry_ref, *, tm):
    @pl.when(pl.program_id(0) == 0)
    def _():
        carry_ref[...] = jnp.zeros_like(carry_ref)

    nt = (((1,), (1,)), ((), ()))
    h_hi, h_lo = _split_bf16(h_ref[...])
    w_hi, w_lo = _split_bf16(wt_ref[...])
    lg = (lax.dot_general(w_hi, h_hi, nt, preferred_element_type=F32)
          + lax.dot_general(w_hi, h_lo, nt, preferred_element_type=F32)
          + lax.dot_general(w_lo, h_hi, nt, preferred_element_type=F32)) + b_ref[...]

    e_iota = lax.broadcasted_iota(I32, (N_EXPERTS, tm), 0)
    chosen = jnp.zeros((N_EXPERTS, tm), F32)
    vals, sels = [], []
    for k in range(TOP_K):
        m = jnp.max(lg, axis=0, keepdims=True)
        idx = jnp.min(jnp.where(lg == m, e_iota, N_EXPERTS), axis=0, keepdims=True)
        sel = e_iota == idx
        idx_ref[k:k + 1, :] = idx
        vals.append(m)
        sels.append(sel)
        chosen = jnp.where(sel, 1.0, chosen)
        lg = jnp.where(sel, -jnp.inf, lg)

    ex = [jnp.exp(v - vals[0]) for v in vals]
    inv = 1.0 / (ex[0] + ex[1] + ex[2] + ex[3])
    for k in range(TOP_K):
        gate_ref[k:k + 1, :] = ex[k] * inv

    r = lax.broadcasted_iota(I32, (tm, tm), 0)
    c = lax.broadcasted_iota(I32, (tm, tm), 1)
    before = jnp.where(r < c, 1.0, 0.0).astype(BF16)
    earlier = jnp.dot(chosen.astype(BF16), before, preferred_element_type=F32) + carry_ref[...]
    for k in range(TOP_K):
        rank_ref[k:k + 1, :] = jnp.sum(jnp.where(sels[k], earlier, 0.0), axis=0, keepdims=True).astype(I32)
    carry_ref[...] += jnp.sum(chosen, axis=1, keepdims=True)
    cnt_ref[...] = carry_ref[...].astype(I32)


def _router(h, w_router, b_router, tm=1024):
    t, d = h.shape
    out = lambda dt: jax.ShapeDtypeStruct((TOP_K, t), dt)
    tok = pl.BlockSpec((TOP_K, tm), lambda i: (0, i))
    return pl.pallas_call(
        functools.partial(_router_kernel, tm=tm),
        grid=(t // tm,),
        in_specs=[pl.BlockSpec((tm, d), lambda i: (i, 0)),
                  pl.BlockSpec((N_EXPERTS, d), lambda i: (0, 0)),
                  pl.BlockSpec((N_EXPERTS, 1), lambda i: (0, 0))],
        out_specs=[tok, tok, tok, pl.BlockSpec((N_EXPERTS, 1), lambda i: (0, 0))],
        out_shape=[out(I32), out(F32), out(I32), jax.ShapeDtypeStruct((N_EXPERTS, 1), I32)],
        scratch_shapes=[pltpu.VMEM((N_EXPERTS, 1), F32)],
        compiler_params=_cparams(("arbitrary",)),
        name="router",
    )(h, w_router.T, b_router.reshape(N_EXPERTS, 1))


def _slab(ref, index):
    return ref.at[pl.ds(pl.multiple_of(index * SLAB_ROWS, SLAB_ROWS), SLAB_ROWS)]


ZERO_RUNS = tuple(1 << s for s in reversed(range(MOE_BLOCK.bit_length() - 1)))


def _dispatch_kernel(dest_ref, pad_at_ref, pad_len_ref, src_ref, x_ref, zero_ref, sem, zero_sem, *, tm):
    def zero_fill(act):
        def per_expert(e, c):
            at, n = pad_at_ref[e], pad_len_ref[e]
            for run in ZERO_RUNS:
                take = n & run

                @pl.when(take != 0)
                def _(at=at, run=run):
                    rows = run * SLAB_ROWS
                    dst = x_ref.at[pl.ds(pl.multiple_of(at * SLAB_ROWS, SLAB_ROWS), rows)]
                    act(pltpu.make_async_copy(zero_ref.at[pl.ds(0, rows)], dst, zero_sem))
                at = at + take
            return c
        lax.fori_loop(0, N_EXPERTS, per_expert, 0)

    @pl.when(pl.program_id(0) == 0)
    def _():
        zero_ref[...] = jnp.zeros_like(zero_ref)
        zero_fill(lambda cp: cp.start())
        zero_fill(lambda cp: cp.wait())

    def copy(t, dst):
        return pltpu.make_async_copy(_slab(src_ref, t), _slab(x_ref, dst), sem)

    def start(g, c):
        toks = [g * DMA_GROUP + j for j in range(DMA_GROUP)]
        dsts = [[dest_ref[t * TOP_K + k] for k in range(TOP_K)] for t in toks]
        for t, row in zip(toks, dsts):
            for k, dst in enumerate(row):
                copy(t, dst).start(priority=k % DMA_PRIORITIES)
        return c

    def wait(t, c):
        for k in range(TOP_K):
            copy(t, 0).wait()
        return c

    lax.fori_loop(0, tm // DMA_GROUP, start, 0)
    lax.fori_loop(0, tm, wait, 0, unroll=4)


def _dispatch(h_slabs, dest, pad_at, pad_len, n_rows, tm=1024):
    t = dest.shape[0] // TOP_K
    smem = pl.BlockSpec(memory_space=pltpu.SMEM)
    return pl.pallas_call(
        functools.partial(_dispatch_kernel, tm=tm),
        grid=(t // tm,),
        in_specs=[pl.BlockSpec((TOP_K * tm,), lambda i: (i,), memory_space=pltpu.SMEM), smem, smem,
                  pl.BlockSpec((tm * SLAB_ROWS, LANES), lambda i: (i, 0))],
        out_specs=pl.BlockSpec(memory_space=pl.ANY),
        out_shape=jax.ShapeDtypeStruct((n_rows * SLAB_ROWS, LANES), U32),
        scratch_shapes=[pltpu.VMEM((ZERO_RUNS[0] * SLAB_ROWS, LANES), U32),
                        pltpu.SemaphoreType.DMA(()), pltpu.SemaphoreType.DMA(())],
        compiler_params=_cparams(("arbitrary",)),
        name="moe_dispatch",
    )(dest, pad_at, pad_len, h_slabs)


UP_CHUNK = 256
UP_HALF = UP_CHUNK // 2


PREP_PARTS = 4


def _in_proj_kernel(a_ref, w_ref, wup_ref, wdn_ref, o_ref, oup_ref, odn_ref):
    o_ref[...] = jnp.dot(a_ref[...], w_ref[...], preferred_element_type=F32).astype(o_ref.dtype)
    r = lax.broadcasted_iota(I32, (UP_CHUNK, UP_CHUNK), 0)
    c = lax.broadcasted_iota(I32, (UP_CHUNK, UP_CHUNK), 1)
    src = jnp.where(c < UP_HALF, 2 * c, 2 * (c - UP_HALF) + 1)
    perm = jnp.where(r == src, 1.0, 0.0).astype(BF16)
    for ch in range(wup_ref.shape[-1] // UP_CHUNK):
        cols = slice(ch * UP_CHUNK, (ch + 1) * UP_CHUNK)
        oup_ref[0, :, cols] = jnp.dot(wup_ref[0, :, cols].astype(BF16), perm,
                                      preferred_element_type=F32).astype(BF16)
    odn_ref[0] = wdn_ref[0].astype(BF16)


def _in_proj_and_moe_weights(a, w, w_up, w_down, layer, tm_max=2048, tn=768):
    m, k = a.shape
    n = w.shape[1]
    _, e, d, n_up = w_up.shape
    f = w_down.shape[2]
    nj = n // tn
    pieces = e * PREP_PARTS
    tm = tm_max
    while (m // tm) * nj < pieces:
        tm //= 2
    assert n_up % (PREP_PARTS * UP_CHUNK) == 0 and f % PREP_PARTS == 0 and m % tm == 0
    up_w = n_up // PREP_PARTS
    fp = f // PREP_PARTS

    def piece(i, j):
        u = jnp.minimum(i * nj + j, pieces - 1)
        return u // PREP_PARTS, u % PREP_PARTS

    def up_in(i, j):
        ex, part = piece(i, j)
        return layer, ex, 0, part

    def up_out(i, j):
        ex, part = piece(i, j)
        return ex, 0, part

    def down_in(i, j):
        ex, part = piece(i, j)
        return layer, ex, part, 0

    def down_out(i, j):
        ex, part = piece(i, j)
        return ex, part, 0

    return pl.pallas_call(
        _in_proj_kernel,
        grid=(m // tm, nj),
        in_specs=[pl.BlockSpec((tm, k), lambda i, j: (i, 0)), pl.BlockSpec((k, tn), lambda i, j: (0, j)),
                  pl.BlockSpec((None, 1, d, up_w), up_in), pl.BlockSpec((None, 1, fp, d), down_in)],
        out_specs=[pl.BlockSpec((tm, tn), lambda i, j: (i, j)),
                   pl.BlockSpec((1, d, up_w), up_out), pl.BlockSpec((1, fp, d), down_out)],
        out_shape=[jax.ShapeDtypeStruct((m, n), BF16), jax.ShapeDtypeStruct((e, d, n_up), BF16),
                   jax.ShapeDtypeStruct((e, f, d), BF16)],
        compiler_params=_cparams(("arbitrary", "arbitrary")),
        name="in_proj",
    )(a, w, w_up, w_down)


def _pair_split_bias(b_up):
    e, n = b_up.shape
    return b_up.reshape(e, n // UP_CHUNK, UP_HALF, 2).transpose(0, 1, 3, 2).reshape(e, 1, n).astype(F32)


def _expert_kernel(be_ref, nu_ref, x_ref, wu_ref, bu_ref, wd_ref, bd_ref, o_ref):
    del be_ref
    i = pl.program_id(0)

    @pl.when(i < nu_ref[0])
    def _():
        los, his = _load_slabs(x_ref, 0, MOE_BLOCK)
        x = jnp.concatenate([p.astype(BF16) for p in los + his], axis=-1)
        up = jnp.dot(x, wu_ref[0], preferred_element_type=F32) + bu_ref[0]
        acts = []
        for ch in range(up.shape[-1] // UP_CHUNK):
            x_glu = jnp.minimum(up[:, ch * UP_CHUNK:ch * UP_CHUNK + UP_HALF], SWIGLU_LIMIT)
            x_lin = jnp.clip(up[:, ch * UP_CHUNK + UP_HALF:(ch + 1) * UP_CHUNK], -SWIGLU_LIMIT, SWIGLU_LIMIT)
            acts.append((x_glu * jax.nn.sigmoid(SWIGLU_ALPHA * x_glu) * (x_lin + 1.0)).astype(BF16))
        act = jnp.concatenate(acts, axis=-1)
        out = jnp.dot(act, wd_ref[0], preferred_element_type=F32) + bd_ref[0]
        _store_slabs(o_ref, out, MOE_BLOCK)

    @pl.when(i >= nu_ref[0])
    def _():
        o_ref[...] = jnp.zeros_like(o_ref)


def _experts(x_pad, block_e, n_used, wu, bu, wd, bd):
    n_blocks = x_pad.shape[0] // (MOE_BLOCK * SLAB_ROWS)
    row = pl.BlockSpec((MOE_BLOCK * SLAB_ROWS, LANES), lambda i, be, nu: (i, 0))
    used_row = pl.BlockSpec((MOE_BLOCK * SLAB_ROWS, LANES), lambda i, be, nu: (jnp.minimum(i, nu[0] - 1), 0))
    per_e = lambda a: pl.BlockSpec((1,) + a.shape[1:], lambda i, be, nu: (be[i], 0, 0))
    return pl.pallas_call(
        _expert_kernel,
        grid_spec=pltpu.PrefetchScalarGridSpec(
            num_scalar_prefetch=2,
            grid=(n_blocks,),
            in_specs=[used_row, per_e(wu), per_e(bu), per_e(wd), per_e(bd)],
            out_specs=row,
        ),
        out_shape=jax.ShapeDtypeStruct(x_pad.shape, U32),
        compiler_params=_cparams(("arbitrary",)),
        name="moe_experts",
    )(block_e, n_used, x_pad, wu, bu, wd, bd)


def _combine_kernel(dest_ref, next_dest_ref, y_ref, gate_ref, h_ref, lg_ref, lb_ref, o32_ref, o16_ref,
                    buf0_ref, buf1_ref, sems, *, tm, n_tiles):
    i = pl.program_id(0)
    bufs = (buf0_ref, buf1_ref)

    def copy(src, t, k, sl):
        return pltpu.make_async_copy(_slab(y_ref, src), _slab(bufs[sl], k * tm + t), sems.at[sl])

    def start_group(d_ref, g, sl):
        toks = [g * DMA_GROUP + j for j in range(DMA_GROUP)]
        srcs = [[d_ref[t * TOP_K + k] for k in range(TOP_K)] for t in toks]
        for t, row in zip(toks, srcs):
            for k, src in enumerate(row):
                copy(src, t, k, sl).start(priority=k % DMA_PRIORITIES)

    def wait_tile(sl):
        def wait(t, c):
            for k in range(TOP_K):
                copy(0, t, k, sl).wait()
            return c
        lax.fori_loop(0, tm, wait, 0, unroll=4)

    def combine(buf_ref):
        g = gate_ref[...]
        gk = [jnp.broadcast_to(g[:, k:k + 1], (tm, LANES)) for k in range(TOP_K)]
        lo_acc, hi_acc = None, None
        for k in range(TOP_K):
            los, his = _load_slabs(buf_ref, k * tm * SLAB_ROWS, tm)
            los = [p * gk[k] for p in los]
            his = [p * gk[k] for p in his]
            lo_acc = los if lo_acc is None else [a + p for a, p in zip(lo_acc, los)]
            hi_acc = his if hi_acc is None else [a + p for a, p in zip(hi_acc, his)]
        ffn = jnp.concatenate(lo_acc + hi_acc, axis=-1)
        y = _layer_norm_rows(DEEPNORM_ALPHA * h_ref[...] + ffn, lg_ref[...], lb_ref[...])
        o32_ref[...] = y
        o16_ref[...] = y.astype(BF16)

    @pl.when(i == 0)
    def _():
        def first(g, c):
            start_group(dest_ref, g, 0)
            return c
        lax.fori_loop(0, tm // DMA_GROUP, first, 0)

    def step(sl):
        wait_tile(sl)
        for g in range(tm // DMA_GROUP):
            start_group(next_dest_ref, g, 1 - sl)
        combine(bufs[sl])

    for sl in range(2):
        pl.when(i % 2 == sl)(functools.partial(step, sl))

    @pl.when(i == n_tiles - 1)
    def _():
        wait_tile(1 - (n_tiles - 1) % 2)


def _combine_ln(out_pad, dest, gate_t, h, ln_g, ln_b, tm=256):
    t, d = h.shape
    n = t // tm
    row = pl.BlockSpec((tm, d), lambda i: (i, 0))
    vec = pl.BlockSpec((1, d), lambda i: (0, 0))
    return pl.pallas_call(
        functools.partial(_combine_kernel, tm=tm, n_tiles=n),
        grid=(n,),
        in_specs=[pl.BlockSpec((TOP_K * tm,), lambda i: (i,), memory_space=pltpu.SMEM),
                  pl.BlockSpec((TOP_K * tm,), lambda i: (jnp.minimum(i + 1, n - 1),), memory_space=pltpu.SMEM),
                  pl.BlockSpec(memory_space=pl.ANY),
                  pl.BlockSpec((tm, TOP_K), lambda i: (i, 0)),
                  row, vec, vec],
        out_specs=[row, row],
        out_shape=[jax.ShapeDtypeStruct((t, d), F32), jax.ShapeDtypeStruct((t, d), BF16)],
        scratch_shapes=[pltpu.VMEM((TOP_K * tm * SLAB_ROWS, LANES), U32),
                        pltpu.VMEM((TOP_K * tm * SLAB_ROWS, LANES), U32), pltpu.SemaphoreType.DMA((2,))],
        compiler_params=_cparams(("arbitrary",)),
        name="moe_combine_ln2",
    )(dest, dest, out_pad, gate_t, h, ln_g.reshape(1, d), ln_b.reshape(1, d))


def _moe_layout(idx, rank, counts, n_blocks):
    counts = counts.reshape(N_EXPERTS)
    padded = (counts + MOE_BLOCK - 1) // MOE_BLOCK * MOE_BLOCK
    pad_end = jnp.cumsum(padded)
    pad_start = pad_end - padded
    experts = jnp.arange(N_EXPERTS, dtype=I32)[:, None, None]
    dest = rank + jnp.sum(jnp.where(idx[None] == experts, pad_start[:, None, None], 0), axis=0)
    block_rows = jnp.arange(n_blocks, dtype=I32) * MOE_BLOCK
    block_e = jnp.minimum(jnp.sum(block_rows[:, None] >= pad_end[None, :], axis=1), N_EXPERTS - 1).astype(I32)
    n_used = (pad_end[-1:] // MOE_BLOCK).astype(I32)
    pad_at = (pad_start + counts).astype(I32)
    pad_len = (padded - counts).astype(I32)
    return dest.astype(I32).T.reshape(-1), block_e, n_used, pad_at, pad_len


def _permute_w_in(w):
    pieces = [w[:, QM_END:], w[:, :Q_END], w[:, V_END:U_END], w[:, U_END:QM_END], w[:, Q_END:K_END], w[:, K_END:V_END]]
    return jnp.concatenate(pieces, axis=1).astype(BF16)


def kernel(x, mem, ln_in_g, ln_in_b, w_in, attn_sinks, ssm_lambda_re, ssm_lambda_im, ssm_log_dt, ssm_b_re, ssm_b_im, ssm_c_re, ssm_c_im, ssm_d, w_glu, w_mem_kv, w_branch, w_out, ln1_g, ln1_b, w_router, b_router, w_up, b_up, w_down, b_down, ln2_g, ln2_b):
    b, s, d = x.shape
    t = b * s
    n_blocks = -(-(t * TOP_K) // MOE_BLOCK) + N_EXPERTS
    mem16 = mem.reshape(b * N_MEM, d).astype(BF16)

    h32, h16 = _layer_norm_in(x.reshape(t, d), ln_in_g, ln_in_b)
    for l in range(DEPTH):
        proj, wu16, wd16 = _in_proj_and_moe_weights(h16, _permute_w_in(w_in[l]), w_up, w_down, l)
        mem_kv = _matmul(mem16, w_mem_kv[l].astype(BF16), tm=1024, tn=512, name="mem_kv")

        attn = _sliding_window_attention(proj, attn_sinks[l].astype(F32), b, s)
        mem_out = _memory_attention(proj, mem_kv, b, s)
        ssm_w = _ssm_weights(ssm_lambda_re[l], ssm_lambda_im[l], ssm_log_dt[l], ssm_b_re[l], ssm_b_im[l],
                             ssm_c_re[l], ssm_c_im[l], ssm_d[l])
        u_tm = proj[:, U_OFF:U_OFF + SSM_WIDTH].reshape(b, s, SSM_WIDTH).transpose(1, 0, 2).reshape(t, SSM_WIDTH)
        ssm_tm = _ssm_branch(u_tm, ssm_w, w_glu[l].astype(BF16), b, s)
        ssm_out = ssm_tm.reshape(s, b, SSM_WIDTH).transpose(1, 0, 2).reshape(t, SSM_WIDTH)

        h1, h1_slabs = _merge_out_ln(attn, ssm_out, mem_out, proj, h32, w_branch[l].astype(BF16),
                                     w_out[l].astype(BF16), ln1_g[l], ln1_b[l])

        idx, gate, rank, counts = _router(h1, w_router[l], b_router[l])
        dest, block_e, n_used, pad_at, pad_len = _moe_layout(idx, rank, counts, n_blocks)
        x_pad = _dispatch(h1_slabs, dest, pad_at, pad_len, n_blocks * MOE_BLOCK)
        out_pad = _experts(x_pad, block_e, n_used, wu16, _pair_split_bias(b_up[l]), wd16,
                           b_down[l][:, None, :].astype(F32))
        h32, h16 = _combine_ln(out_pad, dest, gate.T, h1, ln2_g[l], ln2_b[l])
    return h32.reshape(b, s, d)
```

```python
import functools
import math

import jax
import jax.numpy as jnp
from jax import lax
from jax.experimental import pallas as pl
from jax.experimental.pallas import tpu as pltpu

F32 = jnp.float32
BF16 = jnp.bfloat16
I32 = jnp.int32

D_MODEL = 2048
DEPTH = 2
N_HEADS = 16
N_KV_HEADS = 2
HEAD_DIM = 64
WINDOW = 128
BLOCK = 128
ATTN_WIDTH = N_HEADS * HEAD_DIM
KV_WIDTH = N_KV_HEADS * HEAD_DIM
SSM_WIDTH = D_MODEL // 4
SSM_GROUP = 16
N_SSM_GROUPS = SSM_WIDTH // SSM_GROUP
SSM_STATE = 64
N_MEM = 256
MEM_HEADS = 4
MEM_HEAD_DIM = 128
MEM_WIDTH = MEM_HEADS * MEM_HEAD_DIM
N_BRANCHES = 3
Q_END = ATTN_WIDTH
K_END = Q_END + KV_WIDTH
V_END = K_END + KV_WIDTH
U_END = V_END + SSM_WIDTH
QM_END = U_END + MEM_WIDTH
IN_WIDTH = QM_END + N_BRANCHES * D_MODEL
N_EXPERTS = 32
TOP_K = 4
D_FF = D_MODEL // 2
MOE_BLOCK = 512
SWIGLU_ALPHA = 1.702
SWIGLU_LIMIT = 7.0
LN_EPS = 1e-5
DEEPNORM_ALPHA = (2.0 * DEPTH) ** 0.25

G_OFF = 0
Q_OFF = N_BRANCHES * D_MODEL
U_OFF = Q_OFF + ATTN_WIDTH
QM_OFF = U_OFF + SSM_WIDTH
K_OFF = QM_OFF + MEM_WIDTH
V_OFF = K_OFF + KV_WIDTH

SSM_COMPLEX = N_SSM_GROUPS * SSM_STATE
SSM_HALVES = 2
SSM_SCAN_STRIP = 512

VMEM_LIMIT = 60 * 1024 * 1024


def _cparams(sem):
    return pltpu.CompilerParams(dimension_semantics=sem, vmem_limit_bytes=VMEM_LIMIT)


def _layer_norm_rows(x, g, b):
    mu = jnp.mean(x, axis=-1, keepdims=True)
    xc = x - mu
    var = jnp.mean(xc * xc, axis=-1, keepdims=True)
    return xc * lax.rsqrt(var + LN_EPS) * g + b


LANES = 128
SLAB_ROWS = D_MODEL // 2 // LANES
DMA_GROUP = 4
DMA_PRIORITIES = 2
U32 = jnp.uint32
HIGH_HALF = 0xFFFF0000


def _pack_pair(lo, hi):
    lo = lax.bitcast_convert_type(lo.astype(BF16).astype(F32), U32) >> 16
    hi = lax.bitcast_convert_type(hi.astype(BF16).astype(F32), U32) & U32(HIGH_HALF)
    return lo | hi


def _unpack_pair(w):
    return lax.bitcast_convert_type(w << 16, F32), lax.bitcast_convert_type(w & U32(HIGH_HALF), F32)


def _store_slabs(slab_ref, x, n, first_slab=0):
    half = D_MODEL // 2
    for c in range(SLAB_ROWS):
        lo = x[:, c * LANES:(c + 1) * LANES]
        hi = x[:, half + c * LANES:half + (c + 1) * LANES]
        slab_ref[pl.ds(first_slab * SLAB_ROWS + c, n, stride=SLAB_ROWS), :] = _pack_pair(lo, hi)


def _load_slabs(slab_ref, first_row, n):
    los, his = [], []
    for c in range(SLAB_ROWS):
        lo, hi = _unpack_pair(slab_ref[pl.ds(first_row + c, n, stride=SLAB_ROWS), :])
        los.append(lo)
        his.append(hi)
    return los, his


def _ln_kernel(x_ref, g_ref, b_ref, o32_ref, o16_ref):
    y = _layer_norm_rows(x_ref[...], g_ref[...], b_ref[...])
    o32_ref[...] = y
    o16_ref[...] = y.astype(BF16)


def _layer_norm_in(x, g, b, tm=512):
    t, d = x.shape
    row = pl.BlockSpec((tm, d), lambda i: (i, 0))
    vec = pl.BlockSpec((1, d), lambda i: (0, 0))
    return pl.pallas_call(
        _ln_kernel,
        grid=(t // tm,),
        in_specs=[row, vec, vec],
        out_specs=[row, row],
        out_shape=[jax.ShapeDtypeStruct((t, d), F32), jax.ShapeDtypeStruct((t, d), BF16)],
        compiler_params=_cparams(("parallel",)),
        name="ln_in",
    )(x, g.reshape(1, d), b.reshape(1, d))


def _mm_kernel(a_ref, w_ref, o_ref):
    o_ref[...] = jnp.dot(a_ref[...], w_ref[...], preferred_element_type=F32).astype(o_ref.dtype)


def _matmul(a, w, tm, tn, name):
    m, k = a.shape
    n = w.shape[1]
    return pl.pallas_call(
        _mm_kernel,
        grid=(m // tm, n // tn),
        in_specs=[pl.BlockSpec((tm, k), lambda i, j: (i, 0)), pl.BlockSpec((k, tn), lambda i, j: (0, j))],
        out_specs=pl.BlockSpec((tm, tn), lambda i, j: (i, j)),
        out_shape=jax.ShapeDtypeStruct((m, n), BF16),
        compiler_params=_cparams(("parallel", "parallel")),
        name=name,
    )(a, w)


def _swa_kernel(sink_ref, q_ref, kc_ref, vc_ref, kp_ref, vp_ref, o_ref, *, nblk):
    i = pl.program_id(1)
    grp = N_HEADS // N_KV_HEADS
    pairs = grp // 2
    pw = 2 * HEAD_DIM
    nrow = pairs * BLOCK
    nkey = 2 * BLOCK
    log2e = math.log2(math.e)
    scale = HEAD_DIM ** -0.5 * log2e
    row = lax.broadcasted_iota(I32, (nrow, nkey), 0)
    kj = lax.broadcasted_iota(I32, (nrow, nkey), 1)
    rel = (row & (BLOCK - 1)) + BLOCK - kj
    band_ok = (rel >= 0) & (rel < WINDOW)
    row_pair = lax.broadcasted_iota(I32, (nrow, 1), 0) // BLOCK
    key0 = lax.broadcasted_iota(I32, (nrow, BLOCK), 1) == 0
    low_lanes = lax.broadcasted_iota(I32, (nkey, pw), 1) < HEAD_DIM
    band_key0 = lax.broadcasted_iota(I32, (nkey, pw), 0) == 0
    ones_cols = jnp.concatenate([jnp.where(low_lanes, 1.0, 0.0), jnp.where(low_lanes, 0.0, 1.0)], axis=0)
    first_lo = jnp.where(i > 0, 0, BLOCK)

    def block_diag(band, g):
        swapped = pltpu.roll(band, HEAD_DIM, axis=1)
        top = jnp.where(low_lanes, band if g == 0 else swapped, 0.0)
        bottom = jnp.where(low_lanes, 0.0, swapped if g == 0 else band)
        return jnp.concatenate([top, bottom], axis=0)

    for j in range(nblk):
        rows = slice(j * BLOCK, (j + 1) * BLOCK)
        if j == 0:
            k_prev, v_prev = kp_ref[...], vp_ref[...]
            valid = band_ok & (kj >= first_lo)
        else:
            prev = slice((j - 1) * BLOCK, j * BLOCK)
            k_prev, v_prev = kc_ref[prev, :], vc_ref[prev, :]
            valid = band_ok
        k_band = jnp.concatenate([k_prev, kc_ref[rows, :]], axis=0).astype(F32)
        v_band = jnp.concatenate([v_prev, vc_ref[rows, :]], axis=0).astype(F32)
        v_band = jnp.where(band_key0, 0.0, v_band)
        for g in range(N_KV_HEADS):
            k2 = block_diag(k_band, g).astype(BF16)
            v2 = jnp.concatenate([block_diag(v_band, g), ones_cols], axis=1).astype(BF16)
            cols = [slice((g * pairs + p) * pw, (g * pairs + p + 1) * pw) for p in range(pairs)]
            q2 = jnp.concatenate([q_ref[rows, c] for c in cols], axis=0)
            s = lax.dot_general(q2, k2, (((1,), (1,)), ((), ())), preferred_element_type=F32) * scale
            weights = []
            for half in range(2):
                sink = sink_ref[g * grp + half] * log2e
                for p in range(1, pairs):
                    sink = jnp.where(row_pair == p, sink_ref[g * grp + 2 * p + half] * log2e, sink)
                s_old = s[:, half * nkey:half * nkey + BLOCK]
                s_new = s[:, half * nkey + BLOCK:(half + 1) * nkey]
                s_old = jnp.where(valid[:, :BLOCK], s_old, jnp.where(key0, sink, -jnp.inf))
                s_new = jnp.where(valid[:, BLOCK:], s_new, -jnp.inf)
                m = jnp.max(jnp.maximum(s_old, s_new), axis=-1, keepdims=True)
                weights += [jnp.exp2(s_old - m).astype(BF16), jnp.exp2(s_new - m).astype(BF16)]
            out = jnp.dot(jnp.concatenate(weights, axis=-1), v2, preferred_element_type=F32)
            out = out[:, :pw] * (1.0 / out[:, pw:])
            for p, c in enumerate(cols):
                o_ref[rows, c] = out[p * BLOCK:(p + 1) * BLOCK, :].astype(o_ref.dtype)


def _sliding_window_attention(proj, sinks, b, s, tq=512):
    t = b * s
    nq = s // tq
    nblk = tq // BLOCK
    sb = s // BLOCK
    cur = lambda col: (lambda bi, i: (bi * nq + i, col))
    prev = lambda col: (lambda bi, i: (bi * sb + jnp.maximum(i * nblk - 1, 0), col))
    return pl.pallas_call(
        functools.partial(_swa_kernel, nblk=nblk),
        grid=(b, nq),
        in_specs=[
            pl.BlockSpec(memory_space=pltpu.SMEM),
            pl.BlockSpec((tq, ATTN_WIDTH), cur(Q_OFF // ATTN_WIDTH)),
            pl.BlockSpec((tq, KV_WIDTH), cur(K_OFF // KV_WIDTH)),
            pl.BlockSpec((tq, KV_WIDTH), cur(V_OFF // KV_WIDTH)),
            pl.BlockSpec((BLOCK, KV_WIDTH), prev(K_OFF // KV_WIDTH)),
            pl.BlockSpec((BLOCK, KV_WIDTH), prev(V_OFF // KV_WIDTH)),
        ],
        out_specs=pl.BlockSpec((tq, ATTN_WIDTH), lambda bi, i: (bi * nq + i, 0)),
        out_shape=jax.ShapeDtypeStruct((t, ATTN_WIDTH), BF16),
        compiler_params=_cparams(("parallel", "parallel")),
        name="swa",
    )(sinks, proj, proj, proj, proj, proj)


def _mem_attn_kernel(q_ref, k_ref, v_ref, o_ref):
    s = lax.dot_general(q_ref[...], k_ref[...], (((1,), (1,)), ((), ())),
                        preferred_element_type=F32) * (MEM_HEAD_DIM ** -0.5)
    m = jnp.max(s, axis=-1, keepdims=True)
    e = jnp.exp(s - m)
    p = (e * (1.0 / jnp.sum(e, axis=-1, keepdims=True))).astype(BF16)
    o_ref[...] = jnp.dot(p, v_ref[...], preferred_element_type=F32).astype(o_ref.dtype)


def _memory_attention(proj, mem_kv, b, s):
    t = b * s
    qcol = QM_OFF // MEM_HEAD_DIM
    return pl.pallas_call(
        _mem_attn_kernel,
        grid=(b, MEM_HEADS),
        in_specs=[
            pl.BlockSpec((s, MEM_HEAD_DIM), lambda bi, h: (bi, qcol + h)),
            pl.BlockSpec((N_MEM, MEM_HEAD_DIM), lambda bi, h: (bi, h)),
            pl.BlockSpec((N_MEM, MEM_HEAD_DIM), lambda bi, h: (bi, MEM_HEADS + h)),
        ],
        out_specs=pl.BlockSpec((s, MEM_HEAD_DIM), lambda bi, h: (bi, h)),
        out_shape=jax.ShapeDtypeStruct((t, MEM_WIDTH), BF16),
        compiler_params=_cparams(("parallel", "parallel")),
        name="mem_attn",
    )(proj, mem_kv, mem_kv)


def _gelu_tanh(x):
    return 0.5 * x * (1.0 + jnp.tanh(math.sqrt(2.0 / math.pi) * (x + 0.044715 * (x * x * x))))


def _ssm_kernel(u_ref, wbr_ref, wbi_ref, lr_ref, li_ref, wcr_ref, wci_ref, d_ref, wglu_ref, o_ref,
                xr_ref, xi_ref, sr_ref, si_ref, *, nb, tc):
    @pl.when(pl.program_id(0) == 0)
    def _():
        sr_ref[...] = jnp.zeros_like(sr_ref)
        si_ref[...] = jnp.zeros_like(si_ref)

    u = u_ref[...]
    uw = SSM_WIDTH // SSM_HALVES
    xw = SSM_COMPLEX // SSM_HALVES
    for hf in range(SSM_HALVES):
        uh = u[:, hf * uw:(hf + 1) * uw]
        xr_ref[:, hf * xw:(hf + 1) * xw] = jnp.dot(uh, wbr_ref[hf], preferred_element_type=F32)
        xi_ref[:, hf * xw:(hf + 1) * xw] = jnp.dot(uh, wbi_ref[hf], preferred_element_type=F32)

    def scan_strip(st):
        cols = slice(st * SSM_SCAN_STRIP, (st + 1) * SSM_SCAN_STRIP)
        lr = jnp.broadcast_to(lr_ref[:, cols], (nb, SSM_SCAN_STRIP))
        li = jnp.broadcast_to(li_ref[:, cols], (nb, SSM_SCAN_STRIP))

        def step(t, carry):
            sr, si = carry
            r0 = pl.multiple_of(t * nb, nb)
            nr = lr * sr - li * si + xr_ref[pl.ds(r0, nb), cols]
            ni = lr * si + li * sr + xi_ref[pl.ds(r0, nb), cols]
            xr_ref[pl.ds(r0, nb), cols] = nr
            xi_ref[pl.ds(r0, nb), cols] = ni
            return nr, ni

        sr, si = lax.fori_loop(0, tc, step, (sr_ref[:, cols], si_ref[:, cols]), unroll=True)
        sr_ref[:, cols] = sr
        si_ref[:, cols] = si

    strips_per_half = xw // SSM_SCAN_STRIP
    ys = []
    for hf in range(SSM_HALVES):
        for st in range(hf * strips_per_half, (hf + 1) * strips_per_half):
            scan_strip(st)
        xs = slice(hf * xw, (hf + 1) * xw)
        yr = jnp.dot(xr_ref[:, xs].astype(BF16), wcr_ref[hf], preferred_element_type=F32)
        yi = jnp.dot(xi_ref[:, xs].astype(BF16), wci_ref[hf], preferred_element_type=F32)
        ys.append(yr - yi)
    y = jnp.concatenate(ys, axis=-1) + d_ref[...] * u.astype(F32)
    zg = jnp.dot(_gelu_tanh(y).astype(BF16), wglu_ref[...], preferred_element_type=F32)
    o_ref[...] = (zg[:, :SSM_WIDTH] * jax.nn.sigmoid(zg[:, SSM_WIDTH:])).astype(o_ref.dtype)


def _ssm_weights(lambda_re, lambda_im, log_dt, b_re, b_im, c_re, c_im, d_skip):
    g, p, h = N_SSM_GROUPS, SSM_STATE, SSM_GROUP
    gh = g // SSM_HALVES
    lam = lax.complex(lambda_re.astype(F32), lambda_im.astype(F32))
    dt = jnp.exp(log_dt.astype(F32))[:, None]
    lam_bar = jnp.exp(lam * dt)
    b_bar = ((lam_bar - 1.0) / lam)[..., None] * lax.complex(b_re.astype(F32), b_im.astype(F32))
    eye = jnp.eye(gh, dtype=F32)

    def blockdiag_in(m):
        m = m.reshape(SSM_HALVES, gh, p, h)
        return jnp.einsum('xgph,gk->xghkp', m, eye).reshape(SSM_HALVES, gh * h, gh * p).astype(BF16)

    def blockdiag_out(m):
        m = m.reshape(SSM_HALVES, gh, h, p)
        return jnp.einsum('xghp,gk->xgpkh', m, eye).reshape(SSM_HALVES, gh * p, gh * h).astype(BF16)

    return (blockdiag_in(jnp.real(b_bar)), blockdiag_in(jnp.imag(b_bar)),
            jnp.real(lam_bar).reshape(1, g * p), jnp.imag(lam_bar).reshape(1, g * p),
            blockdiag_out(c_re.astype(F32)), blockdiag_out(c_im.astype(F32)),
            d_skip.astype(F32).reshape(1, g * h))


def _ssm_branch(u_tm, ssm_w, w_glu, nb, s, tc=64):
    wbr, wbi, lr, li, wcr, wci, d = ssm_w
    rows = tc * nb
    full = lambda a: pl.BlockSpec(a.shape, lambda i: (0,) * a.ndim)
    return pl.pallas_call(
        functools.partial(_ssm_kernel, nb=nb, tc=tc),
        grid=(s // tc,),
        in_specs=[pl.BlockSpec((rows, SSM_WIDTH), lambda i: (i, 0)),
                  full(wbr), full(wbi), full(lr), full(li), full(wcr), full(wci), full(d), full(w_glu)],
        out_specs=pl.BlockSpec((rows, SSM_WIDTH), lambda i: (i, 0)),
        out_shape=jax.ShapeDtypeStruct((s * nb, SSM_WIDTH), BF16),
        scratch_shapes=[pltpu.VMEM((rows, SSM_COMPLEX), F32), pltpu.VMEM((rows, SSM_COMPLEX), F32),
                        pltpu.VMEM((nb, SSM_COMPLEX), F32), pltpu.VMEM((nb, SSM_COMPLEX), F32)],
        compiler_params=_cparams(("arbitrary",)),
        name="ssm",
    )(u_tm, wbr, wbi, lr, li, wcr, wci, d, w_glu)


def _merge_kernel(attn_ref, ssm_ref, mem_ref, g0_ref, g1_ref, g2_ref, h_ref, wb_ref, wo_ref, lg_ref, lb_ref,
                  o32_ref, slab_ref, *, tm):
    def gated(x_ref, g_ref, lo, hi):
        br = jnp.dot(x_ref[...], wb_ref[lo:hi, :], preferred_element_type=F32)
        return jax.nn.sigmoid(g_ref[...].astype(F32)) * br

    merged = gated(attn_ref, g0_ref, 0, ATTN_WIDTH)
    merged += gated(ssm_ref, g1_ref, ATTN_WIDTH, ATTN_WIDTH + SSM_WIDTH)
    merged += gated(mem_ref, g2_ref, ATTN_WIDTH + SSM_WIDTH, ATTN_WIDTH + SSM_WIDTH + MEM_WIDTH)
    mix = jnp.dot(merged.astype(BF16), wo_ref[...], preferred_element_type=F32)
    y = _layer_norm_rows(DEEPNORM_ALPHA * h_ref[...] + mix, lg_ref[...], lb_ref[...])
    o32_ref[...] = y
    _store_slabs(slab_ref, y, tm)


def _merge_out_ln(attn, ssm, mem, proj, h, w_branch, w_out, ln_g, ln_b, tm=256):
    t, d = h.shape
    row = lambda w, col=0: pl.BlockSpec((tm, w), lambda i: (i, col))
    const = lambda shape: pl.BlockSpec(shape, lambda i: (0, 0))
    return pl.pallas_call(
        functools.partial(_merge_kernel, tm=tm),
        grid=(t // tm,),
        in_specs=[row(ATTN_WIDTH), row(SSM_WIDTH), row(MEM_WIDTH),
                  row(d, G_OFF // d), row(d, G_OFF // d + 1), row(d, G_OFF // d + 2),
                  row(d), const(w_branch.shape), const(w_out.shape), const((1, d)), const((1, d))],
        out_specs=[row(d), pl.BlockSpec((tm * SLAB_ROWS, LANES), lambda i: (i, 0))],
        out_shape=[jax.ShapeDtypeStruct((t, d), F32), jax.ShapeDtypeStruct((t * SLAB_ROWS, LANES), U32)],
        compiler_params=_cparams(("parallel",)),
        name="merge_out_ln1",
    )(attn, ssm, mem, proj, proj, proj, h, w_branch, w_out, ln_g.reshape(1, d), ln_b.reshape(1, d))


def _split_bf16(x):
    hi = x.astype(BF16)
    return hi, (x - hi.astype(F32)).astype(BF16)


def _router_kernel(h_ref, wt_ref, b_ref, idx_ref, gate_ref, rank_ref, cnt_ref, carry_ref, *, tm):
    @pl.when(pl.program_id(0) == 0)
    def _():
        carry_ref[...] = jnp.zeros_like(carry_ref)

    nt = (((1,), (1,)), ((), ()))
    h_hi, h_lo = _split_bf16(h_ref[...])
    w_hi, w_lo = _split_bf16(wt_ref[...])
    lg = (lax.dot_general(w_hi, h_hi, nt, preferred_element_type=F32)
          + lax.dot_general(w_hi, h_lo, nt, preferred_element_type=F32)
          + lax.dot_general(w_lo, h_hi, nt, preferred_element_type=F32)) + b_ref[...]

    e_iota = lax.broadcasted_iota(I32, (N_EXPERTS, tm), 0)
    chosen = jnp.zeros((N_EXPERTS, tm), F32)
    vals, sels = [], []
    for k in range(TOP_K):
        m = jnp.max(lg, axis=0, keepdims=True)
        idx = jnp.min(jnp.where(lg == m, e_iota, N_EXPERTS), axis=0, keepdims=True)
        sel = e_iota == idx
        idx_ref[k:k + 1, :] = idx
        vals.append(m)
        sels.append(sel)
        chosen = jnp.where(sel, 1.0, chosen)
        lg = jnp.where(sel, -jnp.inf, lg)

    ex = [jnp.exp(v - vals[0]) for v in vals]
    inv = 1.0 / (ex[0] + ex[1] + ex[2] + ex[3])
    for k in range(TOP_K):
        gate_ref[k:k + 1, :] = ex[k] * inv

    r = lax.broadcasted_iota(I32, (tm, tm), 0)
    c = lax.broadcasted_iota(I32, (tm, tm), 1)
    before = jnp.where(r < c, 1.0, 0.0).astype(BF16)
    earlier = jnp.dot(chosen.astype(BF16), before, preferred_element_type=F32) + carry_ref[...]
    for k in range(TOP_K):
        rank_ref[k:k + 1, :] = jnp.sum(jnp.where(sels[k], earlier, 0.0), axis=0, keepdims=True).astype(I32)
    carry_ref[...] += jnp.sum(chosen, axis=1, keepdims=True)
    cnt_ref[...] = carry_ref[...].astype(I32)


def _router(h, w_router, b_router, tm=1024):
    t, d = h.shape
    out = lambda dt: jax.ShapeDtypeStruct((TOP_K, t), dt)
    tok = pl.BlockSpec((TOP_K, tm), lambda i: (0, i))
    return pl.pallas_call(
        functools.partial(_router_kernel, tm=tm),
        grid=(t // tm,),
        in_specs=[pl.BlockSpec((tm, d), lambda i: (i, 0)),
                  pl.BlockSpec((N_EXPERTS, d), lambda i: (0, 0)),
                  pl.BlockSpec((N_EXPERTS, 1), lambda i: (0, 0))],
        out_specs=[tok, tok, tok, pl.BlockSpec((N_EXPERTS, 1), lambda i: (0, 0))],
        out_shape=[out(I32), out(F32), out(I32), jax.ShapeDtypeStruct((N_EXPERTS, 1), I32)],
        scratch_shapes=[pltpu.VMEM((N_EXPERTS, 1), F32)],
        compiler_params=_cparams(("arbitrary",)),
        name="router",
    )(h, w_router.T, b_router.reshape(N_EXPERTS, 1))


def _slab(ref, index):
    return ref.at[pl.ds(pl.multiple_of(index * SLAB_ROWS, SLAB_ROWS), SLAB_ROWS)]


ZERO_RUNS = tuple(1 << s for s in reversed(range(MOE_BLOCK.bit_length() - 1)))


def _dispatch_kernel(dest_ref, pad_at_ref, pad_len_ref, src_ref, x_ref, zero_ref, sem, zero_sem, *, tm):
    def zero_fill(act):
        def per_expert(e, c):
            at, n = pad_at_ref[e], pad_len_ref[e]
            for run in ZERO_RUNS:
                take = n & run

                @pl.when(take != 0)
                def _(at=at, run=run):
                    rows = run * SLAB_ROWS
                    dst = x_ref.at[pl.ds(pl.multiple_of(at * SLAB_ROWS, SLAB_ROWS), rows)]
                    act(pltpu.make_async_copy(zero_ref.at[pl.ds(0, rows)], dst, zero_sem))
                at = at + take
            return c
        lax.fori_loop(0, N_EXPERTS, per_expert, 0)

    @pl.when(pl.program_id(0) == 0)
    def _():
        zero_ref[...] = jnp.zeros_like(zero_ref)
        zero_fill(lambda cp: cp.start())
        zero_fill(lambda cp: cp.wait())

    def copy(t, dst):
        return pltpu.make_async_copy(_slab(src_ref, t), _slab(x_ref, dst), sem)

    def start(g, c):
        toks = [g * DMA_GROUP + j for j in range(DMA_GROUP)]
        dsts = [[dest_ref[t * TOP_K + k] for k in range(TOP_K)] for t in toks]
        for t, row in zip(toks, dsts):
            for k, dst in enumerate(row):
                copy(t, dst).start(priority=k % DMA_PRIORITIES)
        return c

    def wait(t, c):
        for k in range(TOP_K):
            copy(t, 0).wait()
        return c

    lax.fori_loop(0, tm // DMA_GROUP, start, 0)
    lax.fori_loop(0, tm, wait, 0, unroll=4)


def _dispatch(h_slabs, dest, pad_at, pad_len, n_rows, tm=1024):
    t = dest.shape[0] // TOP_K
    smem = pl.BlockSpec(memory_space=pltpu.SMEM)
    return pl.pallas_call(
        functools.partial(_dispatch_kernel, tm=tm),
        grid=(t // tm,),
        in_specs=[pl.BlockSpec((TOP_K * tm,), lambda i: (i,), memory_space=pltpu.SMEM), smem, smem,
                  pl.BlockSpec((tm * SLAB_ROWS, LANES), lambda i: (i, 0))],
        out_specs=pl.BlockSpec(memory_space=pl.ANY),
        out_shape=jax.ShapeDtypeStruct((n_rows * SLAB_ROWS, LANES), U32),
        scratch_shapes=[pltpu.VMEM((ZERO_RUNS[0] * SLAB_ROWS, LANES), U32),
                        pltpu.SemaphoreType.DMA(()), pltpu.SemaphoreType.DMA(())],
        compiler_params=_cparams(("arbitrary",)),
        name="moe_dispatch",
    )(dest, pad_at, pad_len, h_slabs)


UP_CHUNK = 256
UP_HALF = UP_CHUNK // 2


PREP_PARTS = 4


def _in_proj_kernel(a_ref, w_ref, wup_ref, wdn_ref, o_ref, oup_ref, odn_ref):
    o_ref[...] = jnp.dot(a_ref[...], w_ref[...], preferred_element_type=F32).astype(o_ref.dtype)
    r = lax.broadcasted_iota(I32, (UP_CHUNK, UP_CHUNK), 0)
    c = lax.broadcasted_iota(I32, (UP_CHUNK, UP_CHUNK), 1)
    src = jnp.where(c < UP_HALF, 2 * c, 2 * (c - UP_HALF) + 1)
    perm = jnp.where(r == src, 1.0, 0.0).astype(BF16)
    for ch in range(wup_ref.shape[-1] // UP_CHUNK):
        cols = slice(ch * UP_CHUNK, (ch + 1) * UP_CHUNK)
        oup_ref[0, :, cols] = jnp.dot(wup_ref[0, :, cols].astype(BF16), perm,
                                      preferred_element_type=F32).astype(BF16)
    odn_ref[0] = wdn_ref[0].astype(BF16)


def _in_proj_and_moe_weights(a, w, w_up, w_down, layer, tm_max=2048, tn=768):
    m, k = a.shape
    n = w.shape[1]
    _, e, d, n_up = w_up.shape
    f = w_down.shape[2]
    nj = n // tn
    pieces = e * PREP_PARTS
    tm = tm_max
    while (m // tm) * nj < pieces:
        tm //= 2
    assert n_up % (PREP_PARTS * UP_CHUNK) == 0 and f % PREP_PARTS == 0 and m % tm == 0
    up_w = n_up // PREP_PARTS
    fp = f // PREP_PARTS

    def piece(i, j):
        u = jnp.minimum(i * nj + j, pieces - 1)
        return u // PREP_PARTS, u % PREP_PARTS

    def up_in(i, j):
        ex, part = piece(i, j)
        return layer, ex, 0, part

    def up_out(i, j):
        ex, part = piece(i, j)
        return ex, 0, part

    def down_in(i, j):
        ex, part = piece(i, j)
        return layer, ex, part, 0

    def down_out(i, j):
        ex, part = piece(i, j)
        return ex, part, 0

    return pl.pallas_call(
        _in_proj_kernel,
        grid=(m // tm, nj),
        in_specs=[pl.BlockSpec((tm, k), lambda i, j: (i, 0)), pl.BlockSpec((k, tn), lambda i, j: (0, j)),
                  pl.BlockSpec((None, 1, d, up_w), up_in), pl.BlockSpec((None, 1, fp, d), down_in)],
        out_specs=[pl.BlockSpec((tm, tn), lambda i, j: (i, j)),
                   pl.BlockSpec((1, d, up_w), up_out), pl.BlockSpec((1, fp, d), down_out)],
        out_shape=[jax.ShapeDtypeStruct((m, n), BF16), jax.ShapeDtypeStruct((e, d, n_up), BF16),
                   jax.ShapeDtypeStruct((e, f, d), BF16)],
        compiler_params=_cparams(("arbitrary", "arbitrary")),
        name="in_proj",
    )(a, w, w_up, w_down)


def _pair_split_bias(b_up):
    e, n = b_up.shape
    return b_up.reshape(e, n // UP_CHUNK, UP_HALF, 2).transpose(0, 1, 3, 2).reshape(e, 1, n).astype(F32)


def _expert_kernel(be_ref, nu_ref, x_ref, wu_ref, bu_ref, wd_ref, bd_ref, o_ref):
    del be_ref
    i = pl.program_id(0)

    @pl.when(i < nu_ref[0])
    def _():
        los, his = _load_slabs(x_ref, 0, MOE_BLOCK)
        x = jnp.concatenate([p.astype(BF16) for p in los + his], axis=-1)
        up = jnp.dot(x, wu_ref[0], preferred_element_type=F32) + bu_ref[0]
        acts = []
        for ch in range(up.shape[-1] // UP_CHUNK):
            x_glu = jnp.minimum(up[:, ch * UP_CHUNK:ch * UP_CHUNK + UP_HALF], SWIGLU_LIMIT)
            x_lin = jnp.clip(up[:, ch * UP_CHUNK + UP_HALF:(ch + 1) * UP_CHUNK], -SWIGLU_LIMIT, SWIGLU_LIMIT)
            acts.append((x_glu * jax.nn.sigmoid(SWIGLU_ALPHA * x_glu) * (x_lin + 1.0)).astype(BF16))
        act = jnp.concatenate(acts, axis=-1)
        out = jnp.dot(act, wd_ref[0], preferred_element_type=F32) + bd_ref[0]
        _store_slabs(o_ref, out, MOE_BLOCK)

    @pl.when(i >= nu_ref[0])
    def _():
        o_ref[...] = jnp.zeros_like(o_ref)


def _experts(x_pad, block_e, n_used, wu, bu, wd, bd):
    n_blocks = x_pad.shape[0] // (MOE_BLOCK * SLAB_ROWS)
    row = pl.BlockSpec((MOE_BLOCK * SLAB_ROWS, LANES), lambda i, be, nu: (i, 0))
    used_row = pl.BlockSpec((MOE_BLOCK * SLAB_ROWS, LANES), lambda i, be, nu: (jnp.minimum(i, nu[0] - 1), 0))
    per_e = lambda a: pl.BlockSpec((1,) + a.shape[1:], lambda i, be, nu: (be[i], 0, 0))
    return pl.pallas_call(
        _expert_kernel,
        grid_spec=pltpu.PrefetchScalarGridSpec(
            num_scalar_prefetch=2,
            grid=(n_blocks,),
            in_specs=[used_row, per_e(wu), per_e(bu), per_e(wd), per_e(bd)],
            out_specs=row,
        ),
        out_shape=jax.ShapeDtypeStruct(x_pad.shape, U32),
        compiler_params=_cparams(("arbitrary",)),
        name="moe_experts",
    )(block_e, n_used, x_pad, wu, bu, wd, bd)


def _combine_kernel(dest_ref, next_dest_ref, y_ref, gate_ref, h_ref, lg_ref, lb_ref, o32_ref, o16_ref,
                    buf0_ref, buf1_ref, sems, *, tm, n_tiles):
    i = pl.program_id(0)
    bufs = (buf0_ref, buf1_ref)

    def copy(src, t, k, sl):
        return pltpu.make_async_copy(_slab(y_ref, src), _slab(bufs[sl], k * tm + t), sems.at[sl])

    def start_group(d_ref, g, sl):
        toks = [g * DMA_GROUP + j for j in range(DMA_GROUP)]
        srcs = [[d_ref[t * TOP_K + k] for k in range(TOP_K)] for t in toks]
        for t, row in zip(toks, srcs):
            for k, src in enumerate(row):
                copy(src, t, k, sl).start(priority=k % DMA_PRIORITIES)

    def wait_tile(sl):
        def wait(t, c):
            for k in range(TOP_K):
                copy(0, t, k, sl).wait()
            return c
        lax.fori_loop(0, tm, wait, 0, unroll=4)

    def combine(buf_ref):
        g = gate_ref[...]
        gk = [jnp.broadcast_to(g[:, k:k + 1], (tm, LANES)) for k in range(TOP_K)]
        lo_acc, hi_acc = None, None
        for k in range(TOP_K):
            los, his = _load_slabs(buf_ref, k * tm * SLAB_ROWS, tm)
            los = [p * gk[k] for p in los]
            his = [p * gk[k] for p in his]
            lo_acc = los if lo_acc is None else [a + p for a, p in zip(lo_acc, los)]
            hi_acc = his if hi_acc is None else [a + p for a, p in zip(hi_acc, his)]
        ffn = jnp.concatenate(lo_acc + hi_acc, axis=-1)
        y = _layer_norm_rows(DEEPNORM_ALPHA * h_ref[...] + ffn, lg_ref[...], lb_ref[...])
        o32_ref[...] = y
        o16_ref[...] = y.astype(BF16)

    @pl.when(i == 0)
    def _():
        def first(g, c):
            start_group(dest_ref, g, 0)
            return c
        lax.fori_loop(0, tm // DMA_GROUP, first, 0)

    def step(sl):
        wait_tile(sl)
        for g in range(tm // DMA_GROUP):
            start_group(next_dest_ref, g, 1 - sl)
        combine(bufs[sl])

    for sl in range(2):
        pl.when(i % 2 == sl)(functools.partial(step, sl))

    @pl.when(i == n_tiles - 1)
    def _():
        wait_tile(1 - (n_tiles - 1) % 2)


def _combine_ln(out_pad, dest, gate_t, h, ln_g, ln_b, tm=256):
    t, d = h.shape
    n = t // tm
    row = pl.BlockSpec((tm, d), lambda i: (i, 0))
    vec = pl.BlockSpec((1, d), lambda i: (0, 0))
    return pl.pallas_call(
        functools.partial(_combine_kernel, tm=tm, n_tiles=n),
        grid=(n,),
        in_specs=[pl.BlockSpec((TOP_K * tm,), lambda i: (i,), memory_space=pltpu.SMEM),
                  pl.BlockSpec((TOP_K * tm,), lambda i: (jnp.minimum(i + 1, n - 1),), memory_space=pltpu.SMEM),
                  pl.BlockSpec(memory_space=pl.ANY),
                  pl.BlockSpec((tm, TOP_K), lambda i: (i, 0)),
                  row, vec, vec],
        out_specs=[row, row],
        out_shape=[jax.ShapeDtypeStruct((t, d), F32), jax.ShapeDtypeStruct((t, d), BF16)],
        scratch_shapes=[pltpu.VMEM((TOP_K * tm * SLAB_ROWS, LANES), U32),
                        pltpu.VMEM((TOP_K * tm * SLAB_ROWS, LANES), U32), pltpu.SemaphoreType.DMA((2,))],
        compiler_params=_cparams(("arbitrary",)),
        name="moe_combine_ln2",
    )(dest, dest, out_pad, gate_t, h, ln_g.reshape(1, d), ln_b.reshape(1, d))


def _moe_layout(idx, rank, counts, n_blocks):
    counts = counts.reshape(N_EXPERTS)
    padded = (counts + MOE_BLOCK - 1) // MOE_BLOCK * MOE_BLOCK
    pad_end = jnp.cumsum(padded)
    pad_start = pad_end - padded
    experts = jnp.arange(N_EXPERTS, dtype=I32)[:, None, None]
    dest = rank + jnp.sum(jnp.where(idx[None] == experts, pad_start[:, None, None], 0), axis=0)
    block_rows = jnp.arange(n_blocks, dtype=I32) * MOE_BLOCK
    block_e = jnp.minimum(jnp.sum(block_rows[:, None] >= pad_end[None, :], axis=1), N_EXPERTS - 1).astype(I32)
    n_used = (pad_end[-1:] // MOE_BLOCK).astype(I32)
    pad_at = (pad_start + counts).astype(I32)
    pad_len = (padded - counts).astype(I32)
    return dest.astype(I32).T.reshape(-1), block_e, n_used, pad_at, pad_len


def _permute_w_in(w):
    pieces = [w[:, QM_END:], w[:, :Q_END], w[:, V_END:U_END], w[:, U_END:QM_END], w[:, Q_END:K_END], w[:, K_END:V_END]]
    return jnp.concatenate(pieces, axis=1).astype(BF16)


def kernel(x, mem, ln_in_g, ln_in_b, w_in, attn_sinks, ssm_lambda_re, ssm_lambda_im, ssm_log_dt, ssm_b_re, ssm_b_im, ssm_c_re, ssm_c_im, ssm_d, w_glu, w_mem_kv, w_branch, w_out, ln1_g, ln1_b, w_router, b_router, w_up, b_up, w_down, b_down, ln2_g, ln2_b):
    b, s, d = x.shape
    t = b * s
    n_blocks = -(-(t * TOP_K) // MOE_BLOCK) + N_EXPERTS
    mem16 = mem.reshape(b * N_MEM, d).astype(BF16)

    h32, h16 = _layer_norm_in(x.reshape(t, d), ln_in_g, ln_in_b)
    for l in range(DEPTH):
        proj, wu16, wd16 = _in_proj_and_moe_weights(h16, _permute_w_in(w_in[l]), w_up, w_down, l)
        mem_kv = _matmul(mem16, w_mem_kv[l].astype(BF16), tm=1024, tn=512, name="mem_kv")

        attn = _sliding_window_attention(proj, attn_sinks[l].astype(F32), b, s)
        mem_out = _memory_attention(proj, mem_kv, b, s)
        ssm_w = _ssm_weights(ssm_lambda_re[l], ssm_lambda_im[l], ssm_log_dt[l], ssm_b_re[l], ssm_b_im[l],
                             ssm_c_re[l], ssm_c_im[l], ssm_d[l])
        u_tm = proj[:, U_OFF:U_OFF + SSM_WIDTH].reshape(b, s, SSM_WIDTH).transpose(1, 0, 2).reshape(t, SSM_WIDTH)
        ssm_tm = _ssm_branch(u_tm, ssm_w, w_glu[l].astype(BF16), b, s)
        ssm_out = ssm_tm.reshape(s, b, SSM_WIDTH).transpose(1, 0, 2).reshape(t, SSM_WIDTH)

        h1, h1_slabs = _merge_out_ln(attn, ssm_out, mem_out, proj, h32, w_branch[l].astype(BF16),
                                     w_out[l].astype(BF16), ln1_g[l], ln1_b[l])

        idx, gate, rank, counts = _router(h1, w_router[l], b_router[l])
        dest, block_e, n_used, pad_at, pad_len = _moe_layout(idx, rank, counts, n_blocks)
        x_pad = _dispatch(h1_slabs, dest, pad_at, pad_len, n_blocks * MOE_BLOCK)
        out_pad = _experts(x_pad, block_e, n_used, wu16, _pair_split_bias(b_up[l]), wd16,
                           b_down[l][:, None, :].astype(F32))
        h32, h16 = _combine_ln(out_pad, dest, gate.T, h1, ln2_g[l], ln2_b[l])
    return h32.reshape(b, s, d)
```

```python
import functools
import math

import jax
import jax.numpy as jnp
from jax import lax
from jax.experimental import pallas as pl
from jax.experimental.pallas import tpu as pltpu

F32 = jnp.float32
BF16 = jnp.bfloat16
I32 = jnp.int32

D_MODEL = 2048
DEPTH = 2
N_HEADS = 16
N_KV_HEADS = 2
HEAD_DIM = 64
WINDOW = 128
BLOCK = 128
ATTN_WIDTH = N_HEADS * HEAD_DIM
KV_WIDTH = N_KV_HEADS * HEAD_DIM
SSM_WIDTH = D_MODEL // 4
SSM_GROUP = 16
N_SSM_GROUPS = SSM_WIDTH // SSM_GROUP
SSM_STATE = 64
N_MEM = 256
MEM_HEADS = 4
MEM_HEAD_DIM = 128
MEM_WIDTH = MEM_HEADS * MEM_HEAD_DIM
N_BRANCHES = 3
Q_END = ATTN_WIDTH
K_END = Q_END + KV_WIDTH
V_END = K_END + KV_WIDTH
U_END = V_END + SSM_WIDTH
QM_END = U_END + MEM_WIDTH
IN_WIDTH = QM_END + N_BRANCHES * D_MODEL
N_EXPERTS = 32
TOP_K = 4
D_FF = D_MODEL // 2
MOE_BLOCK = 512
SWIGLU_ALPHA = 1.702
SWIGLU_LIMIT = 7.0
LN_EPS = 1e-5
DEEPNORM_ALPHA = (2.0 * DEPTH) ** 0.25

G_OFF = 0
Q_OFF = N_BRANCHES * D_MODEL
U_OFF = Q_OFF + ATTN_WIDTH
QM_OFF = U_OFF + SSM_WIDTH
K_OFF = QM_OFF + MEM_WIDTH
V_OFF = K_OFF + KV_WIDTH

SSM_COMPLEX = N_SSM_GROUPS * SSM_STATE
SSM_HALVES = 2
SSM_SCAN_STRIP = 512

VMEM_LIMIT = 60 * 1024 * 1024


def _cparams(sem):
    return pltpu.CompilerParams(dimension_semantics=sem, vmem_limit_bytes=VMEM_LIMIT)


def _layer_norm_rows(x, g, b):
    mu = jnp.mean(x, axis=-1, keepdims=True)
    xc = x - mu
    var = jnp.mean(xc * xc, axis=-1, keepdims=True)
    return xc * lax.rsqrt(var + LN_EPS) * g + b


LANES = 128
SLAB_ROWS = D_MODEL // 2 // LANES
DMA_GROUP = 4
DMA_PRIORITIES = 2
U32 = jnp.uint32
HIGH_HALF = 0xFFFF0000


def _pack_pair(lo, hi):
    lo = lax.bitcast_convert_type(lo.astype(BF16).astype(F32), U32) >> 16
    hi = lax.bitcast_convert_type(hi.astype(BF16).astype(F32), U32) & U32(HIGH_HALF)
    return lo | hi


def _unpack_pair(w):
    return lax.bitcast_convert_type(w << 16, F32), lax.bitcast_convert_type(w & U32(HIGH_HALF), F32)


def _store_slabs(slab_ref, x, n, first_slab=0):
    half = D_MODEL // 2
    for c in range(SLAB_ROWS):
        lo = x[:, c * LANES:(c + 1) * LANES]
        hi = x[:, half + c * LANES:half + (c + 1) * LANES]
        slab_ref[pl.ds(first_slab * SLAB_ROWS + c, n, stride=SLAB_ROWS), :] = _pack_pair(lo, hi)


def _load_slabs(slab_ref, first_row, n):
    los, his = [], []
    for c in range(SLAB_ROWS):
        lo, hi = _unpack_pair(slab_ref[pl.ds(first_row + c, n, stride=SLAB_ROWS), :])
        los.append(lo)
        his.append(hi)
    return los, his


def _ln_kernel(x_ref, g_ref, b_ref, o32_ref, o16_ref):
    y = _layer_norm_rows(x_ref[...], g_ref[...], b_ref[...])
    o32_ref[...] = y
    o16_ref[...] = y.astype(BF16)


def _layer_norm_in(x, g, b, tm=512):
    t, d = x.shape
    row = pl.BlockSpec((tm, d), lambda i: (i, 0))
    vec = pl.BlockSpec((1, d), lambda i: (0, 0))
    return pl.pallas_call(
        _ln_kernel,
        grid=(t // tm,),
        in_specs=[row, vec, vec],
        out_specs=[row, row],
        out_shape=[jax.ShapeDtypeStruct((t, d), F32), jax.ShapeDtypeStruct((t, d), BF16)],
        compiler_params=_cparams(("parallel",)),
        name="ln_in",
    )(x, g.reshape(1, d), b.reshape(1, d))


def _mm_kernel(a_ref, w_ref, o_ref):
    o_ref[...] = jnp.dot(a_ref[...], w_ref[...], preferred_element_type=F32).astype(o_ref.dtype)


def _matmul(a, w, tm, tn, name):
    m, k = a.shape
    n = w.shape[1]
    return pl.pallas_call(
        _mm_kernel,
        grid=(m // tm, n // tn),
        in_specs=[pl.BlockSpec((tm, k), lambda i, j: (i, 0)), pl.BlockSpec((k, tn), lambda i, j: (0, j))],
        out_specs=pl.BlockSpec((tm, tn), lambda i, j: (i, j)),
        out_shape=jax.ShapeDtypeStruct((m, n), BF16),
        compiler_params=_cparams(("parallel", "parallel")),
        name=name,
    )(a, w)


def _swa_kernel(sink_ref, q_ref, kc_ref, vc_ref, kp_ref, vp_ref, o_ref, *, nblk):
    i = pl.program_id(1)
    grp = N_HEADS // N_KV_HEADS
    pairs = grp // 2
    pw = 2 * HEAD_DIM
    nrow = pairs * BLOCK
    nkey = 2 * BLOCK
    log2e = math.log2(math.e)
    scale = HEAD_DIM ** -0.5 * log2e
    row = lax.broadcasted_iota(I32, (nrow, nkey), 0)
    kj = lax.broadcasted_iota(I32, (nrow, nkey), 1)
    rel = (row & (BLOCK - 1)) + BLOCK - kj
    band_ok = (rel >= 0) & (rel < WINDOW)
    row_pair = lax.broadcasted_iota(I32, (nrow, 1), 0) // BLOCK
    key0 = lax.broadcasted_iota(I32, (nrow, BLOCK), 1) == 0
    low_lanes = lax.broadcasted_iota(I32, (nkey, pw), 1) < HEAD_DIM
    band_key0 = lax.broadcasted_iota(I32, (nkey, pw), 0) == 0
    ones_cols = jnp.concatenate([jnp.where(low_lanes, 1.0, 0.0), jnp.where(low_lanes, 0.0, 1.0)], axis=0)
    first_lo = jnp.where(i > 0, 0, BLOCK)

    def block_diag(band, g):
        swapped = pltpu.roll(band, HEAD_DIM, axis=1)
        top = jnp.where(low_lanes, band if g == 0 else swapped, 0.0)
        bottom = jnp.where(low_lanes, 0.0, swapped if g == 0 else band)
        return jnp.concatenate([top, bottom], axis=0)

    for j in range(nblk):
        rows = slice(j * BLOCK, (j + 1) * BLOCK)
        if j == 0:
            k_prev, v_prev = kp_ref[...], vp_ref[...]
            valid = band_ok & (kj >= first_lo)
        else:
            prev = slice((j - 1) * BLOCK, j * BLOCK)
            k_prev, v_prev = kc_ref[prev, :], vc_ref[prev, :]
            valid = band_ok
        k_band = jnp.concatenate([k_prev, kc_ref[rows, :]], axis=0).astype(F32)
        v_band = jnp.concatenate([v_prev, vc_ref[rows, :]], axis=0).astype(F32)
        v_band = jnp.where(band_key0, 0.0, v_band)
        for g in range(N_KV_HEADS):
            k2 = block_diag(k_band, g).astype(BF16)
            v2 = jnp.concatenate([block_diag(v_band, g), ones_cols], axis=1).astype(BF16)
            cols = [slice((g * pairs + p) * pw, (g * pairs + p + 1) * pw) for p in range(pairs)]
            q2 = jnp.concatenate([q_ref[rows, c] for c in cols], axis=0)
            s = lax.dot_general(q2, k2, (((1,), (1,)), ((), ())), preferred_element_type=F32) * scale
            weights = []
            for half in range(2):
                sink = sink_ref[g * grp + half] * log2e
                for p in range(1, pairs):
                    sink = jnp.where(row_pair == p, sink_ref[g * grp + 2 * p + half] * log2e, sink)
                s_old = s[:, half * nkey:half * nkey + BLOCK]
                s_new = s[:, half * nkey + BLOCK:(half + 1) * nkey]
                s_old = jnp.where(valid[:, :BLOCK], s_old, jnp.where(key0, sink, -jnp.inf))
                s_new = jnp.where(valid[:, BLOCK:], s_new, -jnp.inf)
                m = jnp.max(jnp.maximum(s_old, s_new), axis=-1, keepdims=True)
                weights += [jnp.exp2(s_old - m).astype(BF16), jnp.exp2(s_new - m).astype(BF16)]
            out = jnp.dot(jnp.concatenate(weights, axis=-1), v2, preferred_element_type=F32)
            out = out[:, :pw] * (1.0 / out[:, pw:])
            for p, c in enumerate(cols):
                o_ref[rows, c] = out[p * BLOCK:(p + 1) * BLOCK, :].astype(o_ref.dtype)


def _sliding_window_attention(proj, sinks, b, s, tq=512):
    t = b * s
    nq = s // tq
    nblk = tq // BLOCK
    sb = s // BLOCK
    cur = lambda col: (lambda bi, i: (bi * nq + i, col))
    prev = lambda col: (lambda bi, i: (bi * sb + jnp.maximum(i * nblk - 1, 0), col))
    return pl.pallas_call(
        functools.partial(_swa_kernel, nblk=nblk),
        grid=(b, nq),
        in_specs=[
            pl.BlockSpec(memory_space=pltpu.SMEM),
            pl.BlockSpec((tq, ATTN_WIDTH), cur(Q_OFF // ATTN_WIDTH)),
            pl.BlockSpec((tq, KV_WIDTH), cur(K_OFF // KV_WIDTH)),
            pl.BlockSpec((tq, KV_WIDTH), cur(V_OFF // KV_WIDTH)),
            pl.BlockSpec((BLOCK, KV_WIDTH), prev(K_OFF // KV_WIDTH)),
            pl.BlockSpec((BLOCK, KV_WIDTH), prev(V_OFF // KV_WIDTH)),
        ],
        out_specs=pl.BlockSpec((tq, ATTN_WIDTH), lambda bi, i: (bi * nq + i, 0)),
        out_shape=jax.ShapeDtypeStruct((t, ATTN_WIDTH), BF16),
        compiler_params=_cparams(("parallel", "parallel")),
        name="swa",
    )(sinks, proj, proj, proj, proj, proj)


def _mem_attn_kernel(q_ref, k_ref, v_ref, o_ref):
    s = lax.dot_general(q_ref[...], k_ref[...], (((1,), (1,)), ((), ())),
                        preferred_element_type=F32) * (MEM_HEAD_DIM ** -0.5 * math.log2(math.e))
    e = jnp.exp2(s - jnp.max(s, axis=-1, keepdims=True)).astype(BF16)
    v = v_ref[...]
    out = jnp.dot(e, jnp.concatenate([v, jnp.ones_like(v)], axis=1), preferred_element_type=F32)
    o_ref[...] = (out[:, :MEM_HEAD_DIM] * (1.0 / out[:, MEM_HEAD_DIM:])).astype(o_ref.dtype)


def _memory_attention(proj, mem_kv, b, s):
    t = b * s
    qcol = QM_OFF // MEM_HEAD_DIM
    return pl.pallas_call(
        _mem_attn_kernel,
        grid=(b, MEM_HEADS),
        in_specs=[
            pl.BlockSpec((s, MEM_HEAD_DIM), lambda bi, h: (bi, qcol + h)),
            pl.BlockSpec((N_MEM, MEM_HEAD_DIM), lambda bi, h: (bi, h)),
            pl.BlockSpec((N_MEM, MEM_HEAD_DIM), lambda bi, h: (bi, MEM_HEADS + h)),
        ],
        out_specs=pl.BlockSpec((s, MEM_HEAD_DIM), lambda bi, h: (bi, h)),
        out_shape=jax.ShapeDtypeStruct((t, MEM_WIDTH), BF16),
        compiler_params=_cparams(("parallel", "parallel")),
        name="mem_attn",
    )(proj, mem_kv, mem_kv)


def _gelu_tanh(x):
    return 0.5 * x * (1.0 + jnp.tanh(math.sqrt(2.0 / math.pi) * (x + 0.044715 * (x * x * x))))


def _ssm_kernel(u_ref, wbr_ref, wbi_ref, lr_ref, li_ref, wcr_ref, wci_ref, d_ref, wglu_ref, o_ref,
                xr_ref, xi_ref, sr_ref, si_ref, *, nb, tc):
    @pl.when(pl.program_id(0) == 0)
    def _():
        sr_ref[...] = jnp.zeros_like(sr_ref)
        si_ref[...] = jnp.zeros_like(si_ref)

    u = u_ref[...]
    uw = SSM_WIDTH // SSM_HALVES
    xw = SSM_COMPLEX // SSM_HALVES
    for hf in range(SSM_HALVES):
        uh = u[:, hf * uw:(hf + 1) * uw]
        xr_ref[:, hf * xw:(hf + 1) * xw] = jnp.dot(uh, wbr_ref[hf], preferred_element_type=F32)
        xi_ref[:, hf * xw:(hf + 1) * xw] = jnp.dot(uh, wbi_ref[hf], preferred_element_type=F32)

    def scan_strip(st):
        cols = slice(st * SSM_SCAN_STRIP, (st + 1) * SSM_SCAN_STRIP)
        lr = jnp.broadcast_to(lr_ref[:, cols], (nb, SSM_SCAN_STRIP))
        li = jnp.broadcast_to(li_ref[:, cols], (nb, SSM_SCAN_STRIP))

        def step(t, carry):
            sr, si = carry
            r0 = pl.multiple_of(t * nb, nb)
            nr = lr * sr - li * si + xr_ref[pl.ds(r0, nb), cols]
            ni = lr * si + li * sr + xi_ref[pl.ds(r0, nb), cols]
            xr_ref[pl.ds(r0, nb), cols] = nr
            xi_ref[pl.ds(r0, nb), cols] = ni
            return nr, ni

        sr, si = lax.fori_loop(0, tc, step, (sr_ref[:, cols], si_ref[:, cols]), unroll=True)
        sr_ref[:, cols] = sr
        si_ref[:, cols] = si

    strips_per_half = xw // SSM_SCAN_STRIP
    ys = []
    for hf in range(SSM_HALVES):
        for st in range(hf * strips_per_half, (hf + 1) * strips_per_half):
            scan_strip(st)
        xs = slice(hf * xw, (hf + 1) * xw)
        yr = jnp.dot(xr_ref[:, xs].astype(BF16), wcr_ref[hf], preferred_element_type=F32)
        yi = jnp.dot(xi_ref[:, xs].astype(BF16), wci_ref[hf], preferred_element_type=F32)
        ys.append(yr - yi)
    y = jnp.concatenate(ys, axis=-1) + d_ref[...] * u.astype(F32)
    zg = jnp.dot(_gelu_tanh(y).astype(BF16), wglu_ref[...], preferred_element_type=F32)
    o_ref[...] = (zg[:, :SSM_WIDTH] * jax.nn.sigmoid(zg[:, SSM_WIDTH:])).astype(o_ref.dtype)


def _ssm_weights(lambda_re, lambda_im, log_dt, b_re, b_im, c_re, c_im, d_skip):
    g, p, h = N_SSM_GROUPS, SSM_STATE, SSM_GROUP
    gh = g // SSM_HALVES
    lam = lax.complex(lambda_re.astype(F32), lambda_im.astype(F32))
    dt = jnp.exp(log_dt.astype(F32))[:, None]
    lam_bar = jnp.exp(lam * dt)
    b_bar = ((lam_bar - 1.0) / lam)[..., None] * lax.complex(b_re.astype(F32), b_im.astype(F32))
    eye = jnp.eye(gh, dtype=F32)

    def blockdiag_in(m):
        m = m.reshape(SSM_HALVES, gh, p, h)
        return jnp.einsum('xgph,gk->xghkp', m, eye).reshape(SSM_HALVES, gh * h, gh * p).astype(BF16)

    def blockdiag_out(m):
        m = m.reshape(SSM_HALVES, gh, h, p)
        return jnp.einsum('xghp,gk->xgpkh', m, eye).reshape(SSM_HALVES, gh * p, gh * h).astype(BF16)

    return (blockdiag_in(jnp.real(b_bar)), blockdiag_in(jnp.imag(b_bar)),
            jnp.real(lam_bar).reshape(1, g * p), jnp.imag(lam_bar).reshape(1, g * p),
            blockdiag_out(c_re.astype(F32)), blockdiag_out(c_im.astype(F32)),
            d_skip.astype(F32).reshape(1, g * h))


def _ssm_branch(u_tm, ssm_w, w_glu, nb, s, tc=64):
    wbr, wbi, lr, li, wcr, wci, d = ssm_w
    rows = tc * nb
    full = lambda a: pl.BlockSpec(a.shape, lambda i: (0,) * a.ndim)
    return pl.pallas_call(
        functools.partial(_ssm_kernel, nb=nb, tc=tc),
        grid=(s // tc,),
        in_specs=[pl.BlockSpec((rows, SSM_WIDTH), lambda i: (i, 0)),
                  full(wbr), full(wbi), full(lr), full(li), full(wcr), full(wci), full(d), full(w_glu)],
        out_specs=pl.BlockSpec((rows, SSM_WIDTH), lambda i: (i, 0)),
        out_shape=jax.ShapeDtypeStruct((s * nb, SSM_WIDTH), BF16),
        scratch_shapes=[pltpu.VMEM((rows, SSM_COMPLEX), F32), pltpu.VMEM((rows, SSM_COMPLEX), F32),
                        pltpu.VMEM((nb, SSM_COMPLEX), F32), pltpu.VMEM((nb, SSM_COMPLEX), F32)],
        compiler_params=_cparams(("arbitrary",)),
        name="ssm",
    )(u_tm, wbr, wbi, lr, li, wcr, wci, d, w_glu)


def _merge_kernel(attn_ref, ssm_ref, mem_ref, g0_ref, g1_ref, g2_ref, h_ref, wb_ref, wo_ref, lg_ref, lb_ref,
                  o32_ref, slab_ref, *, tm):
    def gated(x_ref, g_ref, lo, hi):
        br = jnp.dot(x_ref[...], wb_ref[lo:hi, :], preferred_element_type=F32)
        return jax.nn.sigmoid(g_ref[...].astype(F32)) * br

    merged = gated(attn_ref, g0_ref, 0, ATTN_WIDTH)
    merged += gated(ssm_ref, g1_ref, ATTN_WIDTH, ATTN_WIDTH + SSM_WIDTH)
    merged += gated(mem_ref, g2_ref, ATTN_WIDTH + SSM_WIDTH, ATTN_WIDTH + SSM_WIDTH + MEM_WIDTH)
    mix = jnp.dot(merged.astype(BF16), wo_ref[...], preferred_element_type=F32)
    y = _layer_norm_rows(DEEPNORM_ALPHA * h_ref[...] + mix, lg_ref[...], lb_ref[...])
    o32_ref[...] = y
    _store_slabs(slab_ref, y, tm)


def _merge_out_ln(attn, ssm, mem, proj, h, w_branch, w_out, ln_g, ln_b, tm=256):
    t, d = h.shape
    row = lambda w, col=0: pl.BlockSpec((tm, w), lambda i: (i, col))
    const = lambda shape: pl.BlockSpec(shape, lambda i: (0, 0))
    return pl.pallas_call(
        functools.partial(_merge_kernel, tm=tm),
        grid=(t // tm,),
        in_specs=[row(ATTN_WIDTH), row(SSM_WIDTH), row(MEM_WIDTH),
                  row(d, G_OFF // d), row(d, G_OFF // d + 1), row(d, G_OFF // d + 2),
                  row(d), const(w_branch.shape), const(w_out.shape), const((1, d)), const((1, d))],
        out_specs=[row(d), pl.BlockSpec((tm * SLAB_ROWS, LANES), lambda i: (i, 0))],
        out_shape=[jax.ShapeDtypeStruct((t, d), F32), jax.ShapeDtypeStruct((t * SLAB_ROWS, LANES), U32)],
        compiler_params=_cparams(("parallel",)),
        name="merge_out_ln1",
    )(attn, ssm, mem, proj, proj, proj, h, w_branch, w_out, ln_g.reshape(1, d), ln_b.reshape(1, d))


def _split_bf16(x):
    hi = x.astype(BF16)
    return hi, (x - hi.astype(F32)).astype(BF16)


def _router_kernel(h_ref, wt_ref, b_ref, idx_ref, gate_ref, rank_ref, cnt_ref, carry_ref, *, tm):
    @pl.when(pl.program_id(0) == 0)
    def _():
        carry_ref[...] = jnp.zeros_like(carry_ref)

    nt = (((1,), (1,)), ((), ()))
    h_hi, h_lo = _split_bf16(h_ref[...])
    w_hi, w_lo = _split_bf16(wt_ref[...])
    lg = (lax.dot_general(w_hi, h_hi, nt, preferred_element_type=F32)
          + lax.dot_general(w_hi, h_lo, nt, preferred_element_type=F32)
          + lax.dot_general(w_lo, h_hi, nt, preferred_element_type=F32)) + b_ref[...]

    e_iota = lax.broadcasted_iota(I32, (N_EXPERTS, tm), 0)
    chosen = jnp.zeros((N_EXPERTS, tm), F32)
    vals, sels = [], []
    for k in range(TOP_K):
        m = jnp.max(lg, axis=0, keepdims=True)
        idx = jnp.min(jnp.where(lg == m, e_iota, N_EXPERTS), axis=0, keepdims=True)
        sel = e_iota == idx
        idx_ref[k:k + 1, :] = idx
        vals.append(m)
        sels.append(sel)
        chosen = jnp.where(sel, 1.0, chosen)
        lg = jnp.where(sel, -jnp.inf, lg)

    ex = [jnp.exp(v - vals[0]) for v in vals]
    inv = 1.0 / (ex[0] + ex[1] + ex[2] + ex[3])
    for k in range(TOP_K):
        gate_ref[k:k + 1, :] = ex[k] * inv

    r = lax.broadcasted_iota(I32, (tm, tm), 0)
    c = lax.broadcasted_iota(I32, (tm, tm), 1)
    before = jnp.where(r < c, 1.0, 0.0).astype(BF16)
    earlier = jnp.dot(chosen.astype(BF16), before, preferred_element_type=F32) + carry_ref[...]
    for k in range(TOP_K):
        rank_ref[k:k + 1, :] = jnp.sum(jnp.where(sels[k], earlier, 0.0), axis=0, keepdims=True).astype(I32)
    carry_ref[...] += jnp.sum(chosen, axis=1, keepdims=True)
    cnt_ref[...] = carry_ref[...].astype(I32)


def _router(h, w_router, b_router, tm=1024):
    t, d = h.shape
    out = lambda dt: jax.ShapeDtypeStruct((TOP_K, t), dt)
    tok = pl.BlockSpec((TOP_K, tm), lambda i: (0, i))
    return pl.pallas_call(
        functools.partial(_router_kernel, tm=tm),
        grid=(t // tm,),
        in_specs=[pl.BlockSpec((tm, d), lambda i: (i, 0)),
                  pl.BlockSpec((N_EXPERTS, d), lambda i: (0, 0)),
                  pl.BlockSpec((N_EXPERTS, 1), lambda i: (0, 0))],
        out_specs=[tok, tok, tok, pl.BlockSpec((N_EXPERTS, 1), lambda i: (0, 0))],
        out_shape=[out(I32), out(F32), out(I32), jax.ShapeDtypeStruct((N_EXPERTS, 1), I32)],
        scratch_shapes=[pltpu.VMEM((N_EXPERTS, 1), F32)],
        compiler_params=_cparams(("arbitrary",)),
        name="router",
    )(h, w_router.T, b_router.reshape(N_EXPERTS, 1))


def _slab(ref, index):
    return ref.at[pl.ds(pl.multiple_of(index * SLAB_ROWS, SLAB_ROWS), SLAB_ROWS)]


ZERO_RUNS = tuple(1 << s for s in reversed(range(MOE_BLOCK.bit_length() - 1)))


def _dispatch_kernel(dest_ref, pad_at_ref, pad_len_ref, src_ref, x_ref, zero_ref, sem, zero_sem, *, tm):
    def zero_fill(act):
        def per_expert(e, c):
            at, n = pad_at_ref[e], pad_len_ref[e]
            for run in ZERO_RUNS:
                take = n & run

                @pl.when(take != 0)
                def _(at=at, run=run):
                    rows = run * SLAB_ROWS
                    dst = x_ref.at[pl.ds(pl.multiple_of(at * SLAB_ROWS, SLAB_ROWS), rows)]
                    act(pltpu.make_async_copy(zero_ref.at[pl.ds(0, rows)], dst, zero_sem))
                at = at + take
            return c
        lax.fori_loop(0, N_EXPERTS, per_expert, 0)

    @pl.when(pl.program_id(0) == 0)
    def _():
        zero_ref[...] = jnp.zeros_like(zero_ref)
        zero_fill(lambda cp: cp.start())
        zero_fill(lambda cp: cp.wait())

    def copy(t, dst):
        return pltpu.make_async_copy(_slab(src_ref, t), _slab(x_ref, dst), sem)

    def start(g, c):
        toks = [g * DMA_GROUP + j for j in range(DMA_GROUP)]
        dsts = [[dest_ref[t * TOP_K + k] for k in range(TOP_K)] for t in toks]
        for t, row in zip(toks, dsts):
            for k, dst in enumerate(row):
                copy(t, dst).start(priority=k % DMA_PRIORITIES)
        return c

    def wait(t, c):
        for k in range(TOP_K):
            copy(t, 0).wait()
        return c

    lax.fori_loop(0, tm // DMA_GROUP, start, 0)
    lax.fori_loop(0, tm, wait, 0, unroll=4)


def _dispatch(h_slabs, dest, pad_at, pad_len, n_rows, tm=1024):
    t = dest.shape[0] // TOP_K
    smem = pl.BlockSpec(memory_space=pltpu.SMEM)
    return pl.pallas_call(
        functools.partial(_dispatch_kernel, tm=tm),
        grid=(t // tm,),
        in_specs=[pl.BlockSpec((TOP_K * tm,), lambda i: (i,), memory_space=pltpu.SMEM), smem, smem,
                  pl.BlockSpec((tm * SLAB_ROWS, LANES), lambda i: (i, 0))],
        out_specs=pl.BlockSpec(memory_space=pl.ANY),
        out_shape=jax.ShapeDtypeStruct((n_rows * SLAB_ROWS, LANES), U32),
        scratch_shapes=[pltpu.VMEM((ZERO_RUNS[0] * SLAB_ROWS, LANES), U32),
                        pltpu.SemaphoreType.DMA(()), pltpu.SemaphoreType.DMA(())],
        compiler_params=_cparams(("arbitrary",)),
        name="moe_dispatch",
    )(dest, pad_at, pad_len, h_slabs)


UP_CHUNK = 256
UP_HALF = UP_CHUNK // 2


PREP_PARTS = 4


def _in_proj_kernel(a_ref, w_ref, wup_ref, wdn_ref, o_ref, oup_ref, odn_ref):
    o_ref[...] = jnp.dot(a_ref[...], w_ref[...], preferred_element_type=F32).astype(o_ref.dtype)
    r = lax.broadcasted_iota(I32, (UP_CHUNK, UP_CHUNK), 0)
    c = lax.broadcasted_iota(I32, (UP_CHUNK, UP_CHUNK), 1)
    src = jnp.where(c < UP_HALF, 2 * c, 2 * (c - UP_HALF) + 1)
    perm = jnp.where(r == src, 1.0, 0.0).astype(BF16)
    for ch in range(wup_ref.shape[-1] // UP_CHUNK):
        cols = slice(ch * UP_CHUNK, (ch + 1) * UP_CHUNK)
        oup_ref[0, :, cols] = jnp.dot(wup_ref[0, :, cols].astype(BF16), perm,
                                      preferred_element_type=F32).astype(BF16)
    odn_ref[0] = wdn_ref[0].astype(BF16)


def _in_proj_and_moe_weights(a, w, w_up, w_down, layer, tm_max=2048, tn=768):
    m, k = a.shape
    n = w.shape[1]
    _, e, d, n_up = w_up.shape
    f = w_down.shape[2]
    nj = n // tn
    pieces = e * PREP_PARTS
    tm = tm_max
    while (m // tm) * nj < pieces:
        tm //= 2
    assert n_up % (PREP_PARTS * UP_CHUNK) == 0 and f % PREP_PARTS == 0 and m % tm == 0
    up_w = n_up // PREP_PARTS
    fp = f // PREP_PARTS

    def piece(i, j):
        u = jnp.minimum(i * nj + j, pieces - 1)
        return u // PREP_PARTS, u % PREP_PARTS

    def up_in(i, j):
        ex, part = piece(i, j)
        return layer, ex, 0, part

    def up_out(i, j):
        ex, part = piece(i, j)
        return ex, 0, part

    def down_in(i, j):
        ex, part = piece(i, j)
        return layer, ex, part, 0

    def down_out(i, j):
        ex, part = piece(i, j)
        return ex, part, 0

    return pl.pallas_call(
        _in_proj_kernel,
        grid=(m // tm, nj),
        in_specs=[pl.BlockSpec((tm, k), lambda i, j: (i, 0)), pl.BlockSpec((k, tn), lambda i, j: (0, j)),
                  pl.BlockSpec((None, 1, d, up_w), up_in), pl.BlockSpec((None, 1, fp, d), down_in)],
        out_specs=[pl.BlockSpec((tm, tn), lambda i, j: (i, j)),
                   pl.BlockSpec((1, d, up_w), up_out), pl.BlockSpec((1, fp, d), down_out)],
        out_shape=[jax.ShapeDtypeStruct((m, n), BF16), jax.ShapeDtypeStruct((e, d, n_up), BF16),
                   jax.ShapeDtypeStruct((e, f, d), BF16)],
        compiler_params=_cparams(("arbitrary", "arbitrary")),
        name="in_proj",
    )(a, w, w_up, w_down)


def _pair_split_bias(b_up):
    e, n = b_up.shape
    return b_up.reshape(e, n // UP_CHUNK, UP_HALF, 2).transpose(0, 1, 3, 2).reshape(e, 1, n).astype(F32)


def _expert_kernel(be_ref, nu_ref, x_ref, wu_ref, bu_ref, wd_ref, bd_ref, o_ref):
    del be_ref
    i = pl.program_id(0)

    @pl.when(i < nu_ref[0])
    def _():
        los, his = _load_slabs(x_ref, 0, MOE_BLOCK)
        x = jnp.concatenate([p.astype(BF16) for p in los + his], axis=-1)
        up = jnp.dot(x, wu_ref[0], preferred_element_type=F32) + bu_ref[0]
        acts = []
        for ch in range(up.shape[-1] // UP_CHUNK):
            x_glu = jnp.minimum(up[:, ch * UP_CHUNK:ch * UP_CHUNK + UP_HALF], SWIGLU_LIMIT)
            x_lin = jnp.clip(up[:, ch * UP_CHUNK + UP_HALF:(ch + 1) * UP_CHUNK], -SWIGLU_LIMIT, SWIGLU_LIMIT)
            acts.append((x_glu * jax.nn.sigmoid(SWIGLU_ALPHA * x_glu) * (x_lin + 1.0)).astype(BF16))
        act = jnp.concatenate(acts, axis=-1)
        out = jnp.dot(act, wd_ref[0], preferred_element_type=F32) + bd_ref[0]
        _store_slabs(o_ref, out, MOE_BLOCK)

    @pl.when(i >= nu_ref[0])
    def _():
        o_ref[...] = jnp.zeros_like(o_ref)


def _experts(x_pad, block_e, n_used, wu, bu, wd, bd):
    n_blocks = x_pad.shape[0] // (MOE_BLOCK * SLAB_ROWS)
    row = pl.BlockSpec((MOE_BLOCK * SLAB_ROWS, LANES), lambda i, be, nu: (i, 0))
    used_row = pl.BlockSpec((MOE_BLOCK * SLAB_ROWS, LANES), lambda i, be, nu: (jnp.minimum(i, nu[0] - 1), 0))
    per_e = lambda a: pl.BlockSpec((1,) + a.shape[1:], lambda i, be, nu: (be[i], 0, 0))
    return pl.pallas_call(
        _expert_kernel,
        grid_spec=pltpu.PrefetchScalarGridSpec(
            num_scalar_prefetch=2,
            grid=(n_blocks,),
            in_specs=[used_row, per_e(wu), per_e(bu), per_e(wd), per_e(bd)],
            out_specs=row,
        ),
        out_shape=jax.ShapeDtypeStruct(x_pad.shape, U32),
        compiler_params=_cparams(("arbitrary",)),
        name="moe_experts",
    )(block_e, n_used, x_pad, wu, bu, wd, bd)


def _combine_kernel(dest_ref, next_dest_ref, y_ref, gate_ref, h_ref, lg_ref, lb_ref, o32_ref, o16_ref,
                    buf0_ref, buf1_ref, sems, *, tm, n_tiles):
    i = pl.program_id(0)
    bufs = (buf0_ref, buf1_ref)

    def copy(src, t, k, sl):
        return pltpu.make_async_copy(_slab(y_ref, src), _slab(bufs[sl], k * tm + t), sems.at[sl])

    def start_group(d_ref, g, sl):
        toks = [g * DMA_GROUP + j for j in range(DMA_GROUP)]
        srcs = [[d_ref[t * TOP_K + k] for k in range(TOP_K)] for t in toks]
        for t, row in zip(toks, srcs):
            for k, src in enumerate(row):
                copy(src, t, k, sl).start(priority=k % DMA_PRIORITIES)

    def wait_tile(sl):
        def wait(t, c):
            for k in range(TOP_K):
                copy(0, t, k, sl).wait()
            return c
        lax.fori_loop(0, tm, wait, 0, unroll=4)

    def combine(buf_ref):
        g = gate_ref[...]
        gk = [jnp.broadcast_to(g[:, k:k + 1], (tm, LANES)) for k in range(TOP_K)]
        lo_acc, hi_acc = None, None
        for k in range(TOP_K):
            los, his = _load_slabs(buf_ref, k * tm * SLAB_ROWS, tm)
            los = [p * gk[k] for p in los]
            his = [p * gk[k] for p in his]
            lo_acc = los if lo_acc is None else [a + p for a, p in zip(lo_acc, los)]
            hi_acc = his if hi_acc is None else [a + p for a, p in zip(hi_acc, his)]
        ffn = jnp.concatenate(lo_acc + hi_acc, axis=-1)
        y = _layer_norm_rows(DEEPNORM_ALPHA * h_ref[...] + ffn, lg_ref[...], lb_ref[...])
        o32_ref[...] = y
        o16_ref[...] = y.astype(BF16)

    @pl.when(i == 0)
    def _():
        def first(g, c):
            start_group(dest_ref, g, 0)
            return c
        lax.fori_loop(0, tm // DMA_GROUP, first, 0)

    def step(sl):
        wait_tile(sl)
        for g in range(tm // DMA_GROUP):
            start_group(next_dest_ref, g, 1 - sl)
        combine(bufs[sl])

    for sl in range(2):
        pl.when(i % 2 == sl)(functools.partial(step, sl))

    @pl.when(i == n_tiles - 1)
    def _():
        wait_tile(1 - (n_tiles - 1) % 2)


def _combine_ln(out_pad, dest, gate_t, h, ln_g, ln_b, tm=256):
    t, d = h.shape
    n = t // tm
    row = pl.BlockSpec((tm, d), lambda i: (i, 0))
    vec = pl.BlockSpec((1, d), lambda i: (0, 0))
    return pl.pallas_call(
        functools.partial(_combine_kernel, tm=tm, n_tiles=n),
        grid=(n,),
        in_specs=[pl.BlockSpec((TOP_K * tm,), lambda i: (i,), memory_space=pltpu.SMEM),
                  pl.BlockSpec((TOP_K * tm,), lambda i: (jnp.minimum(i + 1, n - 1),), memory_space=pltpu.SMEM),
                  pl.BlockSpec(memory_space=pl.ANY),
                  pl.BlockSpec((tm, TOP_K), lambda i: (i, 0)),
                  row, vec, vec],
        out_specs=[row, row],
        out_shape=[jax.ShapeDtypeStruct((t, d), F32), jax.ShapeDtypeStruct((t, d), BF16)],
        scratch_shapes=[pltpu.VMEM((TOP_K * tm * SLAB_ROWS, LANES), U32),
                        pltpu.VMEM((TOP_K * tm * SLAB_ROWS, LANES), U32), pltpu.SemaphoreType.DMA((2,))],
        compiler_params=_cparams(("arbitrary",)),
        name="moe_combine_ln2",
    )(dest, dest, out_pad, gate_t, h, ln_g.reshape(1, d), ln_b.reshape(1, d))


def _moe_layout(idx, rank, counts, n_blocks):
    counts = counts.reshape(N_EXPERTS)
    padded = (counts + MOE_BLOCK - 1) // MOE_BLOCK * MOE_BLOCK
    pad_end = jnp.cumsum(padded)
    pad_start = pad_end - padded
    experts = jnp.arange(N_EXPERTS, dtype=I32)[:, None, None]
    dest = rank + jnp.sum(jnp.where(idx[None] == experts, pad_start[:, None, None], 0), axis=0)
    block_rows = jnp.arange(n_blocks, dtype=I32) * MOE_BLOCK
    block_e = jnp.minimum(jnp.sum(block_rows[:, None] >= pad_end[None, :], axis=1), N_EXPERTS - 1).astype(I32)
    n_used = (pad_end[-1:] // MOE_BLOCK).astype(I32)
    pad_at = (pad_start + counts).astype(I32)
    pad_len = (padded - counts).astype(I32)
    return dest.astype(I32).T.reshape(-1), block_e, n_used, pad_at, pad_len


def _permute_w_in(w):
    pieces = [w[:, QM_END:], w[:, :Q_END], w[:, V_END:U_END], w[:, U_END:QM_END], w[:, Q_END:K_END], w[:, K_END:V_END]]
    return jnp.concatenate(pieces, axis=1).astype(BF16)


def kernel(x, mem, ln_in_g, ln_in_b, w_in, attn_sinks, ssm_lambda_re, ssm_lambda_im, ssm_log_dt, ssm_b_re, ssm_b_im, ssm_c_re, ssm_c_im, ssm_d, w_glu, w_mem_kv, w_branch, w_out, ln1_g, ln1_b, w_router, b_router, w_up, b_up, w_down, b_down, ln2_g, ln2_b):
    b, s, d = x.shape
    t = b * s
    n_blocks = -(-(t * TOP_K) // MOE_BLOCK) + N_EXPERTS
    mem16 = mem.reshape(b * N_MEM, d).astype(BF16)

    h32, h16 = _layer_norm_in(x.reshape(t, d), ln_in_g, ln_in_b)
    for l in range(DEPTH):
        proj, wu16, wd16 = _in_proj_and_moe_weights(h16, _permute_w_in(w_in[l]), w_up, w_down, l)
        mem_kv = _matmul(mem16, w_mem_kv[l].astype(BF16), tm=1024, tn=512, name="mem_kv")

        attn = _sliding_window_attention(proj, attn_sinks[l].astype(F32), b, s)
        mem_out = _memory_attention(proj, mem_kv, b, s)
        ssm_w = _ssm_weights(ssm_lambda_re[l], ssm_lambda_im[l], ssm_log_dt[l], ssm_b_re[l], ssm_b_im[l],
                             ssm_c_re[l], ssm_c_im[l], ssm_d[l])
        u_tm = proj[:, U_OFF:U_OFF + SSM_WIDTH].reshape(b, s, SSM_WIDTH).transpose(1, 0, 2).reshape(t, SSM_WIDTH)
        ssm_tm = _ssm_branch(u_tm, ssm_w, w_glu[l].astype(BF16), b, s)
        ssm_out = ssm_tm.reshape(s, b, SSM_WIDTH).transpose(1, 0, 2).reshape(t, SSM_WIDTH)

        h1, h1_slabs = _merge_out_ln(attn, ssm_out, mem_out, proj, h32, w_branch[l].astype(BF16),
                                     w_out[l].astype(BF16), ln1_g[l], ln1_b[l])

        idx, gate, rank, counts = _router(h1, w_router[l], b_router[l])
        dest, block_e, n_used, pad_at, pad_len = _moe_layout(idx, rank, counts, n_blocks)
        x_pad = _dispatch(h1_slabs, dest, pad_at, pad_len, n_blocks * MOE_BLOCK)
        out_pad = _experts(x_pad, block_e, n_used, wu16, _pair_split_bias(b_up[l]), wd16,
                           b_down[l][:, None, :].astype(F32))
        h32, h16 = _combine_ln(out_pad, dest, gate.T, h1, ln2_g[l], ln2_b[l])
    return h32.reshape(b, s, d)
```

```python
import functools
import math

import jax
import jax.numpy as jnp
from jax import lax
from jax.experimental import pallas as pl
from jax.experimental.pallas import tpu as pltpu

F32 = jnp.float32
BF16 = jnp.bfloat16
I32 = jnp.int32

D_MODEL = 2048
DEPTH = 2
N_HEADS = 16
N_KV_HEADS = 2
HEAD_DIM = 64
WINDOW = 128
BLOCK = 128
ATTN_WIDTH = N_HEADS * HEAD_DIM
KV_WIDTH = N_KV_HEADS * HEAD_DIM
SSM_WIDTH = D_MODEL // 4
SSM_GROUP = 16
N_SSM_GROUPS = SSM_WIDTH // SSM_GROUP
SSM_STATE = 64
N_MEM = 256
MEM_HEADS = 4
MEM_HEAD_DIM = 128
MEM_WIDTH = MEM_HEADS * MEM_HEAD_DIM
N_BRANCHES = 3
Q_END = ATTN_WIDTH
K_END = Q_END + KV_WIDTH
V_END = K_END + KV_WIDTH
U_END = V_END + SSM_WIDTH
QM_END = U_END + MEM_WIDTH
IN_WIDTH = QM_END + N_BRANCHES * D_MODEL
N_EXPERTS = 32
TOP_K = 4
D_FF = D_MODEL // 2
MOE_BLOCK = 512
SWIGLU_ALPHA = 1.702
SWIGLU_LIMIT = 7.0
LN_EPS = 1e-5
DEEPNORM_ALPHA = (2.0 * DEPTH) ** 0.25

G_OFF = 0
Q_OFF = N_BRANCHES * D_MODEL
U_OFF = Q_OFF + ATTN_WIDTH
QM_OFF = U_OFF + SSM_WIDTH
K_OFF = QM_OFF + MEM_WIDTH
V_OFF = K_OFF + KV_WIDTH

SSM_COMPLEX = N_SSM_GROUPS * SSM_STATE
SSM_HALVES = 2
SSM_SCAN_STRIP = 512

VMEM_LIMIT = 60 * 1024 * 1024


def _cparams(sem):
    return pltpu.CompilerParams(dimension_semantics=sem, vmem_limit_bytes=VMEM_LIMIT)


def _layer_norm_rows(x, g, b):
    mu = jnp.mean(x, axis=-1, keepdims=True)
    xc = x - mu
    var = jnp.mean(xc * xc, axis=-1, keepdims=True)
    return xc * lax.rsqrt(var + LN_EPS) * g + b


LANES = 128
SLAB_ROWS = D_MODEL // 2 // LANES
DMA_GROUP = 4
DMA_PRIORITIES = 2
U32 = jnp.uint32
HIGH_HALF = 0xFFFF0000


def _pack_pair(lo, hi):
    lo = lax.bitcast_convert_type(lo.astype(BF16).astype(F32), U32) >> 16
    hi = lax.bitcast_convert_type(hi.astype(BF16).astype(F32), U32) & U32(HIGH_HALF)
    return lo | hi


def _unpack_pair(w):
    return lax.bitcast_convert_type(w << 16, F32), lax.bitcast_convert_type(w & U32(HIGH_HALF), F32)


def _store_slabs(slab_ref, x, n, first_slab=0):
    half = D_MODEL // 2
    for c in range(SLAB_ROWS):
        lo = x[:, c * LANES:(c + 1) * LANES]
        hi = x[:, half + c * LANES:half + (c + 1) * LANES]
        slab_ref[pl.ds(first_slab * SLAB_ROWS + c, n, stride=SLAB_ROWS), :] = _pack_pair(lo, hi)


def _load_slabs(slab_ref, first_row, n):
    los, his = [], []
    for c in range(SLAB_ROWS):
        lo, hi = _unpack_pair(slab_ref[pl.ds(first_row + c, n, stride=SLAB_ROWS), :])
        los.append(lo)
        his.append(hi)
    return los, his


def _ln_kernel(x_ref, g_ref, b_ref, o32_ref, o16_ref):
    y = _layer_norm_rows(x_ref[...], g_ref[...], b_ref[...])
    o32_ref[...] = y
    o16_ref[...] = y.astype(BF16)


def _layer_norm_in(x, g, b, tm=1024):
    t, d = x.shape
    row = pl.BlockSpec((tm, d), lambda i: (i, 0))
    vec = pl.BlockSpec((1, d), lambda i: (0, 0))
    return pl.pallas_call(
        _ln_kernel,
        grid=(t // tm,),
        in_specs=[row, vec, vec],
        out_specs=[row, row],
        out_shape=[jax.ShapeDtypeStruct((t, d), F32), jax.ShapeDtypeStruct((t, d), BF16)],
        compiler_params=_cparams(("parallel",)),
        name="ln_in",
    )(x, g.reshape(1, d), b.reshape(1, d))


def _mm_kernel(a_ref, w_ref, o_ref):
    o_ref[...] = jnp.dot(a_ref[...], w_ref[...], preferred_element_type=F32).astype(o_ref.dtype)


def _matmul(a, w, tm, tn, name):
    m, k = a.shape
    n = w.shape[1]
    return pl.pallas_call(
        _mm_kernel,
        grid=(m // tm, n // tn),
        in_specs=[pl.BlockSpec((tm, k), lambda i, j: (i, 0)), pl.BlockSpec((k, tn), lambda i, j: (0, j))],
        out_specs=pl.BlockSpec((tm, tn), lambda i, j: (i, j)),
        out_shape=jax.ShapeDtypeStruct((m, n), BF16),
        compiler_params=_cparams(("parallel", "parallel")),
        name=name,
    )(a, w)


def _swa_kernel(sink_ref, q_ref, kc_ref, vc_ref, kp_ref, vp_ref, o_ref, *, nblk):
    i = pl.program_id(1)
    grp = N_HEADS // N_KV_HEADS
    pairs = grp // 2
    pw = 2 * HEAD_DIM
    nrow = pairs * BLOCK
    nkey = 2 * BLOCK
    log2e = math.log2(math.e)
    scale = HEAD_DIM ** -0.5 * log2e
    row = lax.broadcasted_iota(I32, (nrow, nkey), 0)
    kj = lax.broadcasted_iota(I32, (nrow, nkey), 1)
    rel = (row & (BLOCK - 1)) + BLOCK - kj
    band_ok = (rel >= 0) & (rel < WINDOW)
    row_pair = lax.broadcasted_iota(I32, (nrow, 1), 0) // BLOCK
    key0 = lax.broadcasted_iota(I32, (nrow, BLOCK), 1) == 0
    low_lanes = lax.broadcasted_iota(I32, (nkey, pw), 1) < HEAD_DIM
    band_key0 = lax.broadcasted_iota(I32, (nkey, pw), 0) == 0
    ones_cols = jnp.concatenate([jnp.where(low_lanes, 1.0, 0.0), jnp.where(low_lanes, 0.0, 1.0)], axis=0)
    first_lo = jnp.where(i > 0, 0, BLOCK)

    def block_diag(band, g):
        swapped = pltpu.roll(band, HEAD_DIM, axis=1)
        top = jnp.where(low_lanes, band if g == 0 else swapped, 0.0)
        bottom = jnp.where(low_lanes, 0.0, swapped if g == 0 else band)
        return jnp.concatenate([top, bottom], axis=0)

    for j in range(nblk):
        rows = slice(j * BLOCK, (j + 1) * BLOCK)
        if j == 0:
            k_prev, v_prev = kp_ref[...], vp_ref[...]
            valid = band_ok & (kj >= first_lo)
        else:
            prev = slice((j - 1) * BLOCK, j * BLOCK)
            k_prev, v_prev = kc_ref[prev, :], vc_ref[prev, :]
            valid = band_ok
        k_band = jnp.concatenate([k_prev, kc_ref[rows, :]], axis=0).astype(F32)
        v_band = jnp.concatenate([v_prev, vc_ref[rows, :]], axis=0).astype(F32)
        v_band = jnp.where(band_key0, 0.0, v_band)
        for g in range(N_KV_HEADS):
            k2 = block_diag(k_band, g).astype(BF16)
            v2 = jnp.concatenate([block_diag(v_band, g), ones_cols], axis=1).astype(BF16)
            cols = [slice((g * pairs + p) * pw, (g * pairs + p + 1) * pw) for p in range(pairs)]
            q2 = jnp.concatenate([q_ref[rows, c] for c in cols], axis=0)
            s = lax.dot_general(q2, k2, (((1,), (1,)), ((), ())), preferred_element_type=F32) * scale
            weights = []
            for half in range(2):
                sink = sink_ref[g * grp + half] * log2e
                for p in range(1, pairs):
                    sink = jnp.where(row_pair == p, sink_ref[g * grp + 2 * p + half] * log2e, sink)
                s_old = s[:, half * nkey:half * nkey + BLOCK]
                s_new = s[:, half * nkey + BLOCK:(half + 1) * nkey]
                s_old = jnp.where(valid[:, :BLOCK], s_old, jnp.where(key0, sink, -jnp.inf))
                s_new = jnp.where(valid[:, BLOCK:], s_new, -jnp.inf)
                m = jnp.max(jnp.maximum(s_old, s_new), axis=-1, keepdims=True)
                weights += [jnp.exp2(s_old - m).astype(BF16), jnp.exp2(s_new - m).astype(BF16)]
            out = jnp.dot(jnp.concatenate(weights, axis=-1), v2, preferred_element_type=F32)
            out = out[:, :pw] * (1.0 / out[:, pw:])
            for p, c in enumerate(cols):
                o_ref[rows, c] = out[p * BLOCK:(p + 1) * BLOCK, :].astype(o_ref.dtype)


def _sliding_window_attention(proj, sinks, b, s, tq=1024):
    t = b * s
    tq = min(tq, s)
    nq = s // tq
    nblk = tq // BLOCK
    sb = s // BLOCK
    cur = lambda col: (lambda bi, i: (bi * nq + i, col))
    prev = lambda col: (lambda bi, i: (bi * sb + jnp.maximum(i * nblk - 1, 0), col))
    return pl.pallas_call(
        functools.partial(_swa_kernel, nblk=nblk),
        grid=(b, nq),
        in_specs=[
            pl.BlockSpec(memory_space=pltpu.SMEM),
            pl.BlockSpec((tq, ATTN_WIDTH), cur(Q_OFF // ATTN_WIDTH)),
            pl.BlockSpec((tq, KV_WIDTH), cur(K_OFF // KV_WIDTH)),
            pl.BlockSpec((tq, KV_WIDTH), cur(V_OFF // KV_WIDTH)),
            pl.BlockSpec((BLOCK, KV_WIDTH), prev(K_OFF // KV_WIDTH)),
            pl.BlockSpec((BLOCK, KV_WIDTH), prev(V_OFF // KV_WIDTH)),
        ],
        out_specs=pl.BlockSpec((tq, ATTN_WIDTH), lambda bi, i: (bi * nq + i, 0)),
        out_shape=jax.ShapeDtypeStruct((t, ATTN_WIDTH), BF16),
        compiler_params=_cparams(("parallel", "parallel")),
        name="swa",
    )(sinks, proj, proj, proj, proj, proj)


def _mem_attn_kernel(q_ref, k_ref, v_ref, o_ref):
    s = lax.dot_general(q_ref[...], k_ref[...], (((1,), (1,)), ((), ())),
                        preferred_element_type=F32) * (MEM_HEAD_DIM ** -0.5 * math.log2(math.e))
    e = jnp.exp2(s - jnp.max(s, axis=-1, keepdims=True)).astype(BF16)
    v = v_ref[...]
    out = jnp.dot(e, jnp.concatenate([v, jnp.ones_like(v)], axis=1), preferred_element_type=F32)
    o_ref[...] = (out[:, :MEM_HEAD_DIM] * (1.0 / out[:, MEM_HEAD_DIM:])).astype(o_ref.dtype)


def _memory_attention(proj, mem_kv, b, s):
    t = b * s
    qcol = QM_OFF // MEM_HEAD_DIM
    return pl.pallas_call(
        _mem_attn_kernel,
        grid=(b, MEM_HEADS),
        in_specs=[
            pl.BlockSpec((s, MEM_HEAD_DIM), lambda bi, h: (bi, qcol + h)),
            pl.BlockSpec((N_MEM, MEM_HEAD_DIM), lambda bi, h: (bi, h)),
            pl.BlockSpec((N_MEM, MEM_HEAD_DIM), lambda bi, h: (bi, MEM_HEADS + h)),
        ],
        out_specs=pl.BlockSpec((s, MEM_HEAD_DIM), lambda bi, h: (bi, h)),
        out_shape=jax.ShapeDtypeStruct((t, MEM_WIDTH), BF16),
        compiler_params=_cparams(("parallel", "parallel")),
        name="mem_attn",
    )(proj, mem_kv, mem_kv)


def _gelu_tanh(x):
    return 0.5 * x * (1.0 + jnp.tanh(math.sqrt(2.0 / math.pi) * (x + 0.044715 * (x * x * x))))


def _ssm_kernel(u_ref, wbr_ref, wbi_ref, lr_ref, li_ref, wcr_ref, wci_ref, d_ref, wglu_ref, o_ref,
                xr_ref, xi_ref, sr_ref, si_ref, *, nb, tc):
    @pl.when(pl.program_id(0) == 0)
    def _():
        sr_ref[...] = jnp.zeros_like(sr_ref)
        si_ref[...] = jnp.zeros_like(si_ref)

    u = u_ref[...]
    uw = SSM_WIDTH // SSM_HALVES
    xw = SSM_COMPLEX // SSM_HALVES
    for hf in range(SSM_HALVES):
        uh = u[:, hf * uw:(hf + 1) * uw]
        xr_ref[:, hf * xw:(hf + 1) * xw] = jnp.dot(uh, wbr_ref[hf], preferred_element_type=F32)
        xi_ref[:, hf * xw:(hf + 1) * xw] = jnp.dot(uh, wbi_ref[hf], preferred_element_type=F32)

    def scan_strip(st):
        cols = slice(st * SSM_SCAN_STRIP, (st + 1) * SSM_SCAN_STRIP)
        lr = jnp.broadcast_to(lr_ref[:, cols], (nb, SSM_SCAN_STRIP))
        li = jnp.broadcast_to(li_ref[:, cols], (nb, SSM_SCAN_STRIP))

        def step(t, carry):
            sr, si = carry
            r0 = pl.multiple_of(t * nb, nb)
            nr = lr * sr - li * si + xr_ref[pl.ds(r0, nb), cols]
            ni = lr * si + li * sr + xi_ref[pl.ds(r0, nb), cols]
            xr_ref[pl.ds(r0, nb), cols] = nr
            xi_ref[pl.ds(r0, nb), cols] = ni
            return nr, ni

        sr, si = lax.fori_loop(0, tc, step, (sr_ref[:, cols], si_ref[:, cols]), unroll=True)
        sr_ref[:, cols] = sr
        si_ref[:, cols] = si

    strips_per_half = xw // SSM_SCAN_STRIP
    ys = []
    for hf in range(SSM_HALVES):
        for st in range(hf * strips_per_half, (hf + 1) * strips_per_half):
            scan_strip(st)
        xs = slice(hf * xw, (hf + 1) * xw)
        yr = jnp.dot(xr_ref[:, xs].astype(BF16), wcr_ref[hf], preferred_element_type=F32)
        yi = jnp.dot(xi_ref[:, xs].astype(BF16), wci_ref[hf], preferred_element_type=F32)
        ys.append(yr - yi)
    y = jnp.concatenate(ys, axis=-1) + d_ref[...] * u.astype(F32)
    zg = jnp.dot(_gelu_tanh(y).astype(BF16), wglu_ref[...], preferred_element_type=F32)
    o_ref[...] = (zg[:, :SSM_WIDTH] * jax.nn.sigmoid(zg[:, SSM_WIDTH:])).astype(o_ref.dtype)


def _ssm_weights(lambda_re, lambda_im, log_dt, b_re, b_im, c_re, c_im, d_skip):
    g, p, h = N_SSM_GROUPS, SSM_STATE, SSM_GROUP
    gh = g // SSM_HALVES
    lam = lax.complex(lambda_re.astype(F32), lambda_im.astype(F32))
    dt = jnp.exp(log_dt.astype(F32))[:, None]
    lam_bar = jnp.exp(lam * dt)
    b_bar = ((lam_bar - 1.0) / lam)[..., None] * lax.complex(b_re.astype(F32), b_im.astype(F32))
    eye = jnp.eye(gh, dtype=F32)

    def blockdiag_in(m):
        m = m.reshape(SSM_HALVES, gh, p, h)
        return jnp.einsum('xgph,gk->xghkp', m, eye).reshape(SSM_HALVES, gh * h, gh * p).astype(BF16)

    def blockdiag_out(m):
        m = m.reshape(SSM_HALVES, gh, h, p)
        return jnp.einsum('xghp,gk->xgpkh', m, eye).reshape(SSM_HALVES, gh * p, gh * h).astype(BF16)

    return (blockdiag_in(jnp.real(b_bar)), blockdiag_in(jnp.imag(b_bar)),
            jnp.real(lam_bar).reshape(1, g * p), jnp.imag(lam_bar).reshape(1, g * p),
            blockdiag_out(c_re.astype(F32)), blockdiag_out(c_im.astype(F32)),
            d_skip.astype(F32).reshape(1, g * h))


def _ssm_branch(u_tm, ssm_w, w_glu, nb, s, tc=64):
    wbr, wbi, lr, li, wcr, wci, d = ssm_w
    rows = tc * nb
    full = lambda a: pl.BlockSpec(a.shape, lambda i: (0,) * a.ndim)
    return pl.pallas_call(
        functools.partial(_ssm_kernel, nb=nb, tc=tc),
        grid=(s // tc,),
        in_specs=[pl.BlockSpec((rows, SSM_WIDTH), lambda i: (i, 0)),
                  full(wbr), full(wbi), full(lr), full(li), full(wcr), full(wci), full(d), full(w_glu)],
        out_specs=pl.BlockSpec((rows, SSM_WIDTH), lambda i: (i, 0)),
        out_shape=jax.ShapeDtypeStruct((s * nb, SSM_WIDTH), BF16),
        scratch_shapes=[pltpu.VMEM((rows, SSM_COMPLEX), F32), pltpu.VMEM((rows, SSM_COMPLEX), F32),
                        pltpu.VMEM((nb, SSM_COMPLEX), F32), pltpu.VMEM((nb, SSM_COMPLEX), F32)],
        compiler_params=_cparams(("arbitrary",)),
        name="ssm",
    )(u_tm, wbr, wbi, lr, li, wcr, wci, d, w_glu)


def _merge_kernel(attn_ref, ssm_ref, mem_ref, g0_ref, g1_ref, g2_ref, h_ref, wb_ref, wo_ref, lg_ref, lb_ref,
                  o32_ref, slab_ref, *, tm):
    def gated(x_ref, g_ref, lo, hi):
        br = jnp.dot(x_ref[...], wb_ref[lo:hi, :], preferred_element_type=F32)
        return jax.nn.sigmoid(g_ref[...].astype(F32)) * br

    merged = gated(attn_ref, g0_ref, 0, ATTN_WIDTH)
    merged += gated(ssm_ref, g1_ref, ATTN_WIDTH, ATTN_WIDTH + SSM_WIDTH)
    merged += gated(mem_ref, g2_ref, ATTN_WIDTH + SSM_WIDTH, ATTN_WIDTH + SSM_WIDTH + MEM_WIDTH)
    mix = jnp.dot(merged.astype(BF16), wo_ref[...], preferred_element_type=F32)
    y = _layer_norm_rows(DEEPNORM_ALPHA * h_ref[...] + mix, lg_ref[...], lb_ref[...])
    o32_ref[...] = y
    _store_slabs(slab_ref, y, tm)


def _merge_out_ln(attn, ssm, mem, proj, h, w_branch, w_out, ln_g, ln_b, tm=256):
    t, d = h.shape
    row = lambda w, col=0: pl.BlockSpec((tm, w), lambda i: (i, col))
    const = lambda shape: pl.BlockSpec(shape, lambda i: (0, 0))
    return pl.pallas_call(
        functools.partial(_merge_kernel, tm=tm),
        grid=(t // tm,),
        in_specs=[row(ATTN_WIDTH), row(SSM_WIDTH), row(MEM_WIDTH),
                  row(d, G_OFF // d), row(d, G_OFF // d + 1), row(d, G_OFF // d + 2),
                  row(d), const(w_branch.shape), const(w_out.shape), const((1, d)), const((1, d))],
        out_specs=[row(d), pl.BlockSpec((tm * SLAB_ROWS, LANES), lambda i: (i, 0))],
        out_shape=[jax.ShapeDtypeStruct((t, d), F32), jax.ShapeDtypeStruct((t * SLAB_ROWS, LANES), U32)],
        compiler_params=_cparams(("parallel",)),
        name="merge_out_ln1",
    )(attn, ssm, mem, proj, proj, proj, h, w_branch, w_out, ln_g.reshape(1, d), ln_b.reshape(1, d))


def _split_bf16(x):
    hi = x.astype(BF16)
    return hi, (x - hi.astype(F32)).astype(BF16)


def _router_kernel(h_ref, wt_ref, b_ref, idx_ref, gate_ref, rank_ref, cnt_ref, carry_ref, *, tm):
    @pl.when(pl.program_id(0) == 0)
    def _():
        carry_ref[...] = jnp.zeros_like(carry_ref)

    nt = (((1,), (1,)), ((), ()))
    h_hi, h_lo = _split_bf16(h_ref[...])
    w_hi, w_lo = _split_bf16(wt_ref[...])
    lg = (lax.dot_general(w_hi, h_hi, nt, preferred_element_type=F32)
          + lax.dot_general(w_hi, h_lo, nt, preferred_element_type=F32)
          + lax.dot_general(w_lo, h_hi, nt, preferred_element_type=F32)) + b_ref[...]

    e_iota = lax.broadcasted_iota(I32, (N_EXPERTS, tm), 0)
    chosen = jnp.zeros((N_EXPERTS, tm), F32)
    vals, sels = [], []
    for k in range(TOP_K):
        m = jnp.max(lg, axis=0, keepdims=True)
        idx = jnp.min(jnp.where(lg == m, e_iota, N_EXPERTS), axis=0, keepdims=True)
        sel = e_iota == idx
        idx_ref[k:k + 1, :] = idx
        vals.append(m)
        sels.append(sel)
        chosen = jnp.where(sel, 1.0, chosen)
        lg = jnp.where(sel, -jnp.inf, lg)

    ex = [jnp.exp(v - vals[0]) for v in vals]
    inv = 1.0 / (ex[0] + ex[1] + ex[2] + ex[3])
    for k in range(TOP_K):
        gate_ref[k:k + 1, :] = ex[k] * inv

    r = lax.broadcasted_iota(I32, (tm, tm), 0)
    c = lax.broadcasted_iota(I32, (tm, tm), 1)
    before = jnp.where(r < c, 1.0, 0.0).astype(BF16)
    earlier = jnp.dot(chosen.astype(BF16), before, preferred_element_type=F32) + carry_ref[...]
    for k in range(TOP_K):
        rank_ref[k:k + 1, :] = jnp.sum(jnp.where(sels[k], earlier, 0.0), axis=0, keepdims=True).astype(I32)
    carry_ref[...] += jnp.sum(chosen, axis=1, keepdims=True)
    cnt_ref[...] = carry_ref[...].astype(I32)


def _router(h, w_router, b_router, tm=1024):
    t, d = h.shape
    out = lambda dt: jax.ShapeDtypeStruct((TOP_K, t), dt)
    tok = pl.BlockSpec((TOP_K, tm), lambda i: (0, i))
    return pl.pallas_call(
        functools.partial(_router_kernel, tm=tm),
        grid=(t // tm,),
        in_specs=[pl.BlockSpec((tm, d), lambda i: (i, 0)),
                  pl.BlockSpec((N_EXPERTS, d), lambda i: (0, 0)),
                  pl.BlockSpec((N_EXPERTS, 1), lambda i: (0, 0))],
        out_specs=[tok, tok, tok, pl.BlockSpec((N_EXPERTS, 1), lambda i: (0, 0))],
        out_shape=[out(I32), out(F32), out(I32), jax.ShapeDtypeStruct((N_EXPERTS, 1), I32)],
        scratch_shapes=[pltpu.VMEM((N_EXPERTS, 1), F32)],
        compiler_params=_cparams(("arbitrary",)),
        name="router",
    )(h, w_router.T, b_router.reshape(N_EXPERTS, 1))


def _slab(ref, index):
    return ref.at[pl.ds(pl.multiple_of(index * SLAB_ROWS, SLAB_ROWS), SLAB_ROWS)]


ZERO_RUNS = tuple(1 << s for s in reversed(range(MOE_BLOCK.bit_length() - 1)))


def _dispatch_kernel(dest_ref, pad_at_ref, pad_len_ref, src_ref, x_ref, zero_ref, sem, zero_sem, *, tm):
    def zero_fill(act):
        def per_expert(e, c):
            at, n = pad_at_ref[e], pad_len_ref[e]
            for run in ZERO_RUNS:
                take = n & run

                @pl.when(take != 0)
                def _(at=at, run=run):
                    rows = run * SLAB_ROWS
                    dst = x_ref.at[pl.ds(pl.multiple_of(at * SLAB_ROWS, SLAB_ROWS), rows)]
                    act(pltpu.make_async_copy(zero_ref.at[pl.ds(0, rows)], dst, zero_sem))
                at = at + take
            return c
        lax.fori_loop(0, N_EXPERTS, per_expert, 0)

    @pl.when(pl.program_id(0) == 0)
    def _():
        zero_ref[...] = jnp.zeros_like(zero_ref)
        zero_fill(lambda cp: cp.start())
        zero_fill(lambda cp: cp.wait())

    def copy(t, dst):
        return pltpu.make_async_copy(_slab(src_ref, t), _slab(x_ref, dst), sem)

    def start(g, c):
        toks = [g * DMA_GROUP + j for j in range(DMA_GROUP)]
        dsts = [[dest_ref[t * TOP_K + k] for k in range(TOP_K)] for t in toks]
        for t, row in zip(toks, dsts):
            for k, dst in enumerate(row):
                copy(t, dst).start(priority=k % DMA_PRIORITIES)
        return c

    def wait(t, c):
        for k in range(TOP_K):
            copy(t, 0).wait()
        return c

    lax.fori_loop(0, tm // DMA_GROUP, start, 0)
    lax.fori_loop(0, tm, wait, 0, unroll=4)


def _dispatch(h_slabs, dest, pad_at, pad_len, n_rows, tm=1024):
    t = dest.shape[0] // TOP_K
    smem = pl.BlockSpec(memory_space=pltpu.SMEM)
    return pl.pallas_call(
        functools.partial(_dispatch_kernel, tm=tm),
        grid=(t // tm,),
        in_specs=[pl.BlockSpec((TOP_K * tm,), lambda i: (i,), memory_space=pltpu.SMEM), smem, smem,
                  pl.BlockSpec((tm * SLAB_ROWS, LANES), lambda i: (i, 0))],
        out_specs=pl.BlockSpec(memory_space=pl.ANY),
        out_shape=jax.ShapeDtypeStruct((n_rows * SLAB_ROWS, LANES), U32),
        scratch_shapes=[pltpu.VMEM((ZERO_RUNS[0] * SLAB_ROWS, LANES), U32),
                        pltpu.SemaphoreType.DMA(()), pltpu.SemaphoreType.DMA(())],
        compiler_params=_cparams(("arbitrary",)),
        name="moe_dispatch",
    )(dest, pad_at, pad_len, h_slabs)


UP_CHUNK = 256
UP_HALF = UP_CHUNK // 2


PREP_PARTS = 4


def _in_proj_kernel(a_ref, w_ref, wup_ref, wdn_ref, o_ref, oup_ref, odn_ref):
    o_ref[...] = jnp.dot(a_ref[...], w_ref[...], preferred_element_type=F32).astype(o_ref.dtype)
    r = lax.broadcasted_iota(I32, (UP_CHUNK, UP_CHUNK), 0)
    c = lax.broadcasted_iota(I32, (UP_CHUNK, UP_CHUNK), 1)
    src = jnp.where(c < UP_HALF, 2 * c, 2 * (c - UP_HALF) + 1)
    perm = jnp.where(r == src, 1.0, 0.0).astype(BF16)
    for ch in range(wup_ref.shape[-1] // UP_CHUNK):
        cols = slice(ch * UP_CHUNK, (ch + 1) * UP_CHUNK)
        oup_ref[0, :, cols] = jnp.dot(wup_ref[0, :, cols].astype(BF16), perm,
                                      preferred_element_type=F32).astype(BF16)
    odn_ref[0] = wdn_ref[0].astype(BF16)


def _in_proj_and_moe_weights(a, w, w_up, w_down, layer, tm_max=2048, tn=768):
    m, k = a.shape
    n = w.shape[1]
    _, e, d, n_up = w_up.shape
    f = w_down.shape[2]
    nj = n // tn
    pieces = e * PREP_PARTS
    tm = tm_max
    while (m // tm) * nj < pieces:
        tm //= 2
    assert n_up % (PREP_PARTS * UP_CHUNK) == 0 and f % PREP_PARTS == 0 and m % tm == 0
    up_w = n_up // PREP_PARTS
    fp = f // PREP_PARTS

    def piece(i, j):
        u = jnp.minimum(i * nj + j, pieces - 1)
        return u // PREP_PARTS, u % PREP_PARTS

    def up_in(i, j):
        ex, part = piece(i, j)
        return layer, ex, 0, part

    def up_out(i, j):
        ex, part = piece(i, j)
        return ex, 0, part

    def down_in(i, j):
        ex, part = piece(i, j)
        return layer, ex, part, 0

    def down_out(i, j):
        ex, part = piece(i, j)
        return ex, part, 0

    return pl.pallas_call(
        _in_proj_kernel,
        grid=(m // tm, nj),
        in_specs=[pl.BlockSpec((tm, k), lambda i, j: (i, 0)), pl.BlockSpec((k, tn), lambda i, j: (0, j)),
                  pl.BlockSpec((None, 1, d, up_w), up_in), pl.BlockSpec((None, 1, fp, d), down_in)],
        out_specs=[pl.BlockSpec((tm, tn), lambda i, j: (i, j)),
                   pl.BlockSpec((1, d, up_w), up_out), pl.BlockSpec((1, fp, d), down_out)],
        out_shape=[jax.ShapeDtypeStruct((m, n), BF16), jax.ShapeDtypeStruct((e, d, n_up), BF16),
                   jax.ShapeDtypeStruct((e, f, d), BF16)],
        compiler_params=_cparams(("arbitrary", "arbitrary")),
        name="in_proj",
    )(a, w, w_up, w_down)


def _pair_split_bias(b_up):
    e, n = b_up.shape
    return b_up.reshape(e, n // UP_CHUNK, UP_HALF, 2).transpose(0, 1, 3, 2).reshape(e, 1, n).astype(F32)


def _expert_kernel(be_ref, nu_ref, x_ref, wu_ref, bu_ref, wd_ref, bd_ref, o_ref):
    del be_ref
    i = pl.program_id(0)

    @pl.when(i < nu_ref[0])
    def _():
        los, his = _load_slabs(x_ref, 0, MOE_BLOCK)
        x = jnp.concatenate([p.astype(BF16) for p in los + his], axis=-1)
        up = jnp.dot(x, wu_ref[0], preferred_element_type=F32) + bu_ref[0]
        acts = []
        for ch in range(up.shape[-1] // UP_CHUNK):
            x_glu = jnp.minimum(up[:, ch * UP_CHUNK:ch * UP_CHUNK + UP_HALF], SWIGLU_LIMIT)
            x_lin = jnp.clip(up[:, ch * UP_CHUNK + UP_HALF:(ch + 1) * UP_CHUNK], -SWIGLU_LIMIT, SWIGLU_LIMIT)
            acts.append((x_glu * jax.nn.sigmoid(SWIGLU_ALPHA * x_glu) * (x_lin + 1.0)).astype(BF16))
        act = jnp.concatenate(acts, axis=-1)
        out = jnp.dot(act, wd_ref[0], preferred_element_type=F32) + bd_ref[0]
        _store_slabs(o_ref, out, MOE_BLOCK)

    @pl.when(i >= nu_ref[0])
    def _():
        o_ref[...] = jnp.zeros_like(o_ref)


def _experts(x_pad, block_e, n_used, wu, bu, wd, bd):
    n_blocks = x_pad.shape[0] // (MOE_BLOCK * SLAB_ROWS)
    row = pl.BlockSpec((MOE_BLOCK * SLAB_ROWS, LANES), lambda i, be, nu: (i, 0))
    used_row = pl.BlockSpec((MOE_BLOCK * SLAB_ROWS, LANES), lambda i, be, nu: (jnp.minimum(i, nu[0] - 1), 0))
    per_e = lambda a: pl.BlockSpec((1,) + a.shape[1:], lambda i, be, nu: (be[i], 0, 0))
    return pl.pallas_call(
        _expert_kernel,
        grid_spec=pltpu.PrefetchScalarGridSpec(
            num_scalar_prefetch=2,
            grid=(n_blocks,),
            in_specs=[used_row, per_e(wu), per_e(bu), per_e(wd), per_e(bd)],
            out_specs=row,
        ),
        out_shape=jax.ShapeDtypeStruct(x_pad.shape, U32),
        compiler_params=_cparams(("arbitrary",)),
        name="moe_experts",
    )(block_e, n_used, x_pad, wu, bu, wd, bd)


def _combine_kernel(dest_ref, next_dest_ref, y_ref, gate_ref, h_ref, lg_ref, lb_ref, o32_ref, o16_ref,
                    buf0_ref, buf1_ref, sems, *, tm, n_tiles):
    i = pl.program_id(0)
    bufs = (buf0_ref, buf1_ref)

    def copy(src, t, k, sl):
        return pltpu.make_async_copy(_slab(y_ref, src), _slab(bufs[sl], k * tm + t), sems.at[sl])

    def start_group(d_ref, g, sl):
        toks = [g * DMA_GROUP + j for j in range(DMA_GROUP)]
        srcs = [[d_ref[t * TOP_K + k] for k in range(TOP_K)] for t in toks]
        for t, row in zip(toks, srcs):
            for k, src in enumerate(row):
                copy(src, t, k, sl).start(priority=k % DMA_PRIORITIES)

    def wait_tile(sl):
        def wait(t, c):
            for k in range(TOP_K):
                copy(0, t, k, sl).wait()
            return c
        lax.fori_loop(0, tm, wait, 0, unroll=4)

    def combine(buf_ref):
        g = gate_ref[...]
        gk = [jnp.broadcast_to(g[:, k:k + 1], (tm, LANES)) for k in range(TOP_K)]
        lo_acc, hi_acc = None, None
        for k in range(TOP_K):
            los, his = _load_slabs(buf_ref, k * tm * SLAB_ROWS, tm)
            los = [p * gk[k] for p in los]
            his = [p * gk[k] for p in his]
            lo_acc = los if lo_acc is None else [a + p for a, p in zip(lo_acc, los)]
            hi_acc = his if hi_acc is None else [a + p for a, p in zip(hi_acc, his)]
        ffn = jnp.concatenate(lo_acc + hi_acc, axis=-1)
        y = _layer_norm_rows(DEEPNORM_ALPHA * h_ref[...] + ffn, lg_ref[...], lb_ref[...])
        o32_ref[...] = y
        o16_ref[...] = y.astype(BF16)

    @pl.when(i == 0)
    def _():
        def first(g, c):
            start_group(dest_ref, g, 0)
            return c
        lax.fori_loop(0, tm // DMA_GROUP, first, 0)

    def step(sl):
        wait_tile(sl)
        for g in range(tm // DMA_GROUP):
            start_group(next_dest_ref, g, 1 - sl)
        combine(bufs[sl])

    for sl in range(2):
        pl.when(i % 2 == sl)(functools.partial(step, sl))

    @pl.when(i == n_tiles - 1)
    def _():
        wait_tile(1 - (n_tiles - 1) % 2)


def _combine_ln(out_pad, dest, gate_t, h, ln_g, ln_b, tm=256):
    t, d = h.shape
    n = t // tm
    row = pl.BlockSpec((tm, d), lambda i: (i, 0))
    vec = pl.BlockSpec((1, d), lambda i: (0, 0))
    return pl.pallas_call(
        functools.partial(_combine_kernel, tm=tm, n_tiles=n),
        grid=(n,),
        in_specs=[pl.BlockSpec((TOP_K * tm,), lambda i: (i,), memory_space=pltpu.SMEM),
                  pl.BlockSpec((TOP_K * tm,), lambda i: (jnp.minimum(i + 1, n - 1),), memory_space=pltpu.SMEM),
                  pl.BlockSpec(memory_space=pl.ANY),
                  pl.BlockSpec((tm, TOP_K), lambda i: (i, 0)),
                  row, vec, vec],
        out_specs=[row, row],
        out_shape=[jax.ShapeDtypeStruct((t, d), F32), jax.ShapeDtypeStruct((t, d), BF16)],
        scratch_shapes=[pltpu.VMEM((TOP_K * tm * SLAB_ROWS, LANES), U32),
                        pltpu.VMEM((TOP_K * tm * SLAB_ROWS, LANES), U32), pltpu.SemaphoreType.DMA((2,))],
        compiler_params=_cparams(("arbitrary",)),
        name="moe_combine_ln2",
    )(dest, dest, out_pad, gate_t, h, ln_g.reshape(1, d), ln_b.reshape(1, d))


def _moe_layout(idx, rank, counts, n_blocks):
    counts = counts.reshape(N_EXPERTS)
    padded = (counts + MOE_BLOCK - 1) // MOE_BLOCK * MOE_BLOCK
    pad_end = jnp.cumsum(padded)
    pad_start = pad_end - padded
    experts = jnp.arange(N_EXPERTS, dtype=I32)[:, None, None]
    dest = rank + jnp.sum(jnp.where(idx[None] == experts, pad_start[:, None, None], 0), axis=0)
    block_rows = jnp.arange(n_blocks, dtype=I32) * MOE_BLOCK
    block_e = jnp.minimum(jnp.sum(block_rows[:, None] >= pad_end[None, :], axis=1), N_EXPERTS - 1).astype(I32)
    n_used = (pad_end[-1:] // MOE_BLOCK).astype(I32)
    pad_at = (pad_start + counts).astype(I32)
    pad_len = (padded - counts).astype(I32)
    return dest.astype(I32).T.reshape(-1), block_e, n_used, pad_at, pad_len


def _permute_w_in(w):
    pieces = [w[:, QM_END:], w[:, :Q_END], w[:, V_END:U_END], w[:, U_END:QM_END], w[:, Q_END:K_END], w[:, K_END:V_END]]
    return jnp.concatenate(pieces, axis=1).astype(BF16)


def kernel(x, mem, ln_in_g, ln_in_b, w_in, attn_sinks, ssm_lambda_re, ssm_lambda_im, ssm_log_dt, ssm_b_re, ssm_b_im, ssm_c_re, ssm_c_im, ssm_d, w_glu, w_mem_kv, w_branch, w_out, ln1_g, ln1_b, w_router, b_router, w_up, b_up, w_down, b_down, ln2_g, ln2_b):
    b, s, d = x.shape
    t = b * s
    n_blocks = -(-(t * TOP_K) // MOE_BLOCK) + N_EXPERTS
    mem16 = mem.reshape(b * N_MEM, d).astype(BF16)

    h32, h16 = _layer_norm_in(x.reshape(t, d), ln_in_g, ln_in_b)
    for l in range(DEPTH):
        proj, wu16, wd16 = _in_proj_and_moe_weights(h16, _permute_w_in(w_in[l]), w_up, w_down, l)
        mem_kv = _matmul(mem16, w_mem_kv[l].astype(BF16), tm=1024, tn=512, name="mem_kv")

        attn = _sliding_window_attention(proj, attn_sinks[l].astype(F32), b, s)
        mem_out = _memory_attention(proj, mem_kv, b, s)
        ssm_w = _ssm_weights(ssm_lambda_re[l], ssm_lambda_im[l], ssm_log_dt[l], ssm_b_re[l], ssm_b_im[l],
                             ssm_c_re[l], ssm_c_im[l], ssm_d[l])
        u_tm = proj[:, U_OFF:U_OFF + SSM_WIDTH].reshape(b, s, SSM_WIDTH).transpose(1, 0, 2).reshape(t, SSM_WIDTH)
        ssm_tm = _ssm_branch(u_tm, ssm_w, w_glu[l].astype(BF16), b, s)
        ssm_out = ssm_tm.reshape(s, b, SSM_WIDTH).transpose(1, 0, 2).reshape(t, SSM_WIDTH)

        h1, h1_slabs = _merge_out_ln(attn, ssm_out, mem_out, proj, h32, w_branch[l].astype(BF16),
                                     w_out[l].astype(BF16), ln1_g[l], ln1_b[l])

        idx, gate, rank, counts = _router(h1, w_router[l], b_router[l])
        dest, block_e, n_used, pad_at, pad_len = _moe_layout(idx, rank, counts, n_blocks)
        x_pad = _dispatch(h1_slabs, dest, pad_at, pad_len, n_blocks * MOE_BLOCK)
        out_pad = _experts(x_pad, block_e, n_used, wu16, _pair_split_bias(b_up[l]), wd16,
                           b_down[l][:, None, :].astype(F32))
        h32, h16 = _combine_ln(out_pad, dest, gate.T, h1, ln2_g[l], ln2_b[l])
    return h32.reshape(b, s, d)
```
